```python
import jax, jax.numpy as jnp
from jax import lax
import numpy as np

D_MODEL = 1024
BATCH = 32
SEQ = 256
DEPTH = 4
DEC_BATCH = 2
DEC_SEQ = 2048
PAST_LEN = 256

GRID_W = 64
HEAD_DIM = 64
N_HEADS = D_MODEL // HEAD_DIM
N_KV_HEADS = N_HEADS // 4
NA_KH = 8
NA_KW = 16
D_FF = ((8 * D_MODEL // 3 + 127) // 128) * 128
CONV_W = 3
Q_BLOCK = 128
ROPE_BASE = 10000.0
EPS = 1e-6
N_MIXERS = 2

kernel_name = 'hybrid_na_gqa_diffusion_step'


def rmsnorm(x, g):
    xf = x.astype(jnp.float32)
    y = xf * lax.rsqrt(jnp.mean(xf * xf, axis=-1, keepdims=True) + EPS)
    return (y * g.astype(jnp.float32)).astype(x.dtype)


def adaln(cond, w, b):
    m = jax.nn.silu(cond) @ w + b
    return jnp.split(m[..., None, :], 6, axis=-1)


def axial_rope(x):
    t = x.shape[1]
    pos = jnp.arange(t)
    rows = (pos // GRID_W).astype(jnp.float32)
    cols = (pos % GRID_W).astype(jnp.float32)
    half = HEAD_DIM // 2
    quarter = half // 2
    freqs = ROPE_BASE ** (-jnp.arange(quarter, dtype=jnp.float32) / quarter)

    def rot(xp, p):
        ang = p[:, None] * freqs[None, :]
        cos = jnp.cos(ang)[None, :, None, :].astype(x.dtype)
        sin = jnp.sin(ang)[None, :, None, :].astype(x.dtype)
        x1, x2 = xp[..., :quarter], xp[..., quarter:]
        return jnp.concatenate([x1 * cos - x2 * sin, x1 * sin + x2 * cos], axis=-1)

    return jnp.concatenate([rot(x[..., :half], rows), rot(x[..., half:], cols)], axis=-1)


def blocked_attention(q, k, v):
    b, t, h, dh = q.shape
    hkv = k.shape[2]
    g = h // hkv
    nblk = t // Q_BLOCK
    qb = q.reshape(b, nblk, Q_BLOCK, hkv, g, dh).transpose(1, 0, 2, 3, 4, 5)
    scale = dh ** -0.5

    def one(qi):
        s = jnp.einsum('bqkgd,bskd->bkgqs', qi, k).astype(jnp.float32) * scale
        p = jax.nn.softmax(s, axis=-1).astype(v.dtype)
        return jnp.einsum('bkgqs,bskd->bqkgd', p, v)

    o = lax.map(one, qb)
    return o.transpose(1, 0, 2, 3, 4, 5).reshape(b, t, h, dh)


def neighbourhood_attention(q, k, v, k_ctx, v_ctx, rpb):
    b, t, h, dh = q.shape
    rows = t // GRID_W
    kh = min(NA_KH, rows)
    kw = NA_KW
    r = jnp.arange(rows)
    col = jnp.arange(GRID_W)
    r_start = jnp.clip(r - kh // 2, 0, rows - kh)
    row_idx = r_start[:, None] + jnp.arange(kh)[None, :]
    c_start = jnp.clip(col - kw // 2, 0, GRID_W - kw)
    col_ok = (col[None, :] >= c_start[:, None]) & (col[None, :] < c_start[:, None] + kw)
    qg = q.reshape(b, rows, GRID_W, h, dh)
    kg = k.reshape(b, rows, GRID_W, h, dh)[:, row_idx]
    vg = v.reshape(b, rows, GRID_W, h, dh)[:, row_idx]
    scale = dh ** -0.5
    s_lat = jnp.einsum('brqhd,brkwhd->bhrqkw', qg, kg).astype(jnp.float32) * scale
    dr = row_idx - r[:, None] + (NA_KH - 1)
    dc = jnp.clip(col[None, :] - col[:, None] + (NA_KW - 1), 0, 2 * NA_KW - 2)
    bias = rpb[:, dr[:, None, :, None], dc[None, :, None, :]].astype(jnp.float32)
    s_lat = jnp.where(col_ok[None, None, None, :, None, :], s_lat + bias[None], -jnp.inf)
    s_ctx = jnp.einsum('brqhd,bchd->bhrqc', qg, k_ctx).astype(jnp.float32) * scale
    n_lat = kh * GRID_W
    s = jnp.concatenate([s_lat.reshape(b, h, rows, GRID_W, n_lat), s_ctx], axis=-1)
    p = jax.nn.softmax(s, axis=-1).astype(v.dtype)
    p_lat = p[..., :n_lat].reshape(b, h, rows, GRID_W, kh, GRID_W)
    p_ctx = p[..., n_lat:]
    o = jnp.einsum('bhrqkw,brkwhd->brqhd', p_lat, vg) + jnp.einsum('bhrqc,bchd->brqhd', p_ctx, v_ctx)
    return o.reshape(b, t, h, dh)


def na_qkv(h, w_qkv):
    b, t, _ = h.shape
    q, k, v = jnp.split(h @ w_qkv, 3, axis=-1)
    return (q.reshape(b, t, N_HEADS, HEAD_DIM), k.reshape(b, t, N_HEADS, HEAD_DIM),
            v.reshape(b, t, N_HEADS, HEAD_DIM))


def gqa_qkv(h, w_qkv, q_norm, k_norm):
    b, t, _ = h.shape
    qk = N_HEADS * HEAD_DIM
    kv = N_KV_HEADS * HEAD_DIM
    q, k, v = jnp.split(h @ w_qkv, [qk, qk + kv], axis=-1)
    q = rmsnorm(q.reshape(b, t, N_HEADS, HEAD_DIM), q_norm)
    k = rmsnorm(k.reshape(b, t, N_KV_HEADS, HEAD_DIM), k_norm)
    return q, k, v.reshape(b, t, N_KV_HEADS, HEAD_DIM)


def conv_ffn(h, w_up, conv_w, conv_b, w_down):
    u = h @ w_up
    up = jnp.pad(u, ((0, 0), (1, 1), (0, 0)))
    u = up[:, :-2] * conv_w[0] + up[:, 1:-1] * conv_w[1] + up[:, 2:] * conv_w[2] + conv_b
    a, gt = jnp.split(u, 2, axis=-1)
    return (jax.nn.silu(a) * gt) @ w_down


def setup_inputs(seed: int = 0) -> dict:
    key = jax.random.key(seed)
    ks = jax.random.split(key, 24)
    n_a = (DEPTH + N_MIXERS - 1) // N_MIXERS
    n_b = DEPTH // N_MIXERS
    f32 = jnp.float32
    D = D_MODEL

    def nrm(k, shape, scale):
        return jax.random.normal(k, shape, f32) * scale

    def gain(k, shape):
        return 1.0 + 0.1 * jax.random.normal(k, shape, f32)

    return {
        'x_prompt': nrm(ks[0], (BATCH, SEQ, D), 1.0),
        'x_sample': nrm(ks[1], (DEC_BATCH, DEC_SEQ, D), 1.0),
        'cache_na_k': nrm(ks[2], (DEC_BATCH, n_a, PAST_LEN, N_HEADS, HEAD_DIM), 1.0),
        'cache_na_v': nrm(ks[3], (DEC_BATCH, n_a, PAST_LEN, N_HEADS, HEAD_DIM), 1.0),
        'cache_gqa_k': nrm(ks[4], (DEC_BATCH, n_b, PAST_LEN, N_KV_HEADS, HEAD_DIM), 1.0),
        'cache_gqa_v': nrm(ks[5], (DEC_BATCH, n_b, PAST_LEN, N_KV_HEADS, HEAD_DIM), 1.0),
        'c': nrm(ks[6], (DEC_BATCH, D), 1.0),
        'c_ctx': nrm(ks[7], (D,), 1.0),
        'ada_w': nrm(ks[8], (DEPTH, D, 6 * D), 0.5 * D ** -0.5),
        'ada_b': nrm(ks[9], (DEPTH, 6 * D), 0.02),
        'norm_mix_pre': gain(ks[10], (DEPTH, D)),
        'norm_mix_post': gain(ks[11], (DEPTH, D)),
        'norm_ffn_pre': gain(ks[12], (DEPTH, D)),
        'norm_ffn_post': gain(ks[13], (DEPTH, D)),
        'na_w_qkv': nrm(ks[14], (n_a, D, 3 * N_HEADS * HEAD_DIM), D ** -0.5),
        'na_w_o': nrm(ks[15], (n_a, N_HEADS * HEAD_DIM, D), (N_HEADS * HEAD_DIM) ** -0.5),
        'na_rpb': nrm(ks[16], (n_a, N_HEADS, 2 * NA_KH - 1, 2 * NA_KW - 1), 0.5),
        'gqa_w_qkv': nrm(ks[17], (n_b, D, (N_HEADS + 2 * N_KV_HEADS) * HEAD_DIM), D ** -0.5),
        'gqa_w_o': nrm(ks[18], (n_b, N_HEADS * HEAD_DIM, D), (N_HEADS * HEAD_DIM) ** -0.5),
        'gqa_q_norm': gain(ks[19], (n_b, HEAD_DIM)),
        'gqa_k_norm': gain(ks[20], (n_b, HEAD_DIM)),
        'ffn_w_up': nrm(ks[21], (DEPTH, D, 2 * D_FF), D ** -0.5),
        'ffn_conv_w': nrm(ks[22], (DEPTH, CONV_W, 2 * D_FF), CONV_W ** -0.5),
        'ffn_conv_b': nrm(jax.random.fold_in(ks[22], 1), (DEPTH, 2 * D_FF), 0.02),
        'ffn_w_down': nrm(ks[23], (DEPTH, D_FF, D), D_FF ** -0.5),
    }


def reference(x_prompt, x_sample, cache_na_k, cache_na_v, cache_gqa_k, cache_gqa_v, c, c_ctx,
              ada_w, ada_b, norm_mix_pre, norm_mix_post, norm_ffn_pre, norm_ffn_post,
              na_w_qkv, na_w_o, na_rpb, gqa_w_qkv, gqa_w_o, gqa_q_norm, gqa_k_norm,
              ffn_w_up, ffn_conv_w, ffn_conv_b, ffn_w_down):
    xp, xs = x_prompt, x_sample
    bp, tp, _ = xp.shape
    bs, ts, _ = xs.shape
    new_na_k, new_na_v, new_gqa_k, new_gqa_v = [], [], [], []
    for l in range(DEPTH):
        i = l // N_MIXERS
        sh_ap, sc_ap, g_ap, sh_fp, sc_fp, g_fp = adaln(c_ctx, ada_w[l], ada_b[l])
        sh_as, sc_as, g_as, sh_fs, sc_fs, g_fs = adaln(c, ada_w[l], ada_b[l])
        hp = rmsnorm(xp, norm_mix_pre[l]) * (1.0 + sc_ap) + sh_ap
        hs = rmsnorm(xs, norm_mix_pre[l]) * (1.0 + sc_as) + sh_as
        if l % N_MIXERS == 0:
            qp, kp, vp = na_qkv(hp, na_w_qkv[i])
            op = blocked_attention(qp, kp, vp)
            new_na_k.append(kp)
            new_na_v.append(vp)
            qs, ks_, vs = na_qkv(hs, na_w_qkv[i])
            os_ = neighbourhood_attention(qs, ks_, vs, cache_na_k[:, i], cache_na_v[:, i], na_rpb[i])
            w_o = na_w_o[i]
        else:
            qp, kp, vp = gqa_qkv(hp, gqa_w_qkv[i], gqa_q_norm[i], gqa_k_norm[i])
            op = blocked_attention(qp, kp, vp)
            new_gqa_k.append(kp)
            new_gqa_v.append(vp)
            qs, ks_, vs = gqa_qkv(hs, gqa_w_qkv[i], gqa_q_norm[i], gqa_k_norm[i])
            qs = axial_rope(qs)
            ks_ = axial_rope(ks_)
            k_all = jnp.concatenate([ks_, cache_gqa_k[:, i]], axis=1)
            v_all = jnp.concatenate([vs, cache_gqa_v[:, i]], axis=1)
            os_ = blocked_attention(qs, k_all, v_all)
            w_o = gqa_w_o[i]
        xp = xp + g_ap * rmsnorm(op.reshape(bp, tp, -1) @ w_o, norm_mix_post[l])
        xs = xs + g_as * rmsnorm(os_.reshape(bs, ts, -1) @ w_o, norm_mix_post[l])
        hp = rmsnorm(xp, norm_ffn_pre[l]) * (1.0 + sc_fp) + sh_fp
        hs = rmsnorm(xs, norm_ffn_pre[l]) * (1.0 + sc_fs) + sh_fs
        fp = conv_ffn(hp, ffn_w_up[l], ffn_conv_w[l], ffn_conv_b[l], ffn_w_down[l])
        fs = conv_ffn(hs, ffn_w_up[l], ffn_conv_w[l], ffn_conv_b[l], ffn_w_down[l])
        xp = xp + g_fp * rmsnorm(fp, norm_ffn_post[l])
        xs = xs + g_fs * rmsnorm(fs, norm_ffn_post[l])
    na_k_out = jnp.stack(new_na_k, axis=1)
    na_v_out = jnp.stack(new_na_v, axis=1)
    gqa_k_out = jnp.stack(new_gqa_k, axis=1)
    gqa_v_out = jnp.stack(new_gqa_v, axis=1)
    return (xp, xs, na_k_out, na_v_out, gqa_k_out, gqa_v_out)
```

```python
import functools

import jax
import jax.numpy as jnp
from jax import lax
from jax.experimental import pallas as pl
from jax.experimental.pallas import tpu as pltpu

D = 1024
HD = 64
NH = 16
NKV = 4
DFF = 2816
DEPTH = 4
GRID_W = 64
ROWS = 32
NA_KH = 8
NA_KW = 16
ROPE_BASE = 10000.0
EPS = 1e-6
BATCH, SEQ = 32, 256
DEC_BATCH, DEC_SEQ = 2, 2048
PAST = 256
NP = BATCH * SEQ
NS = DEC_BATCH * DEC_SEQ

BF = jnp.bfloat16
F32 = jnp.float32
NEG = -1e30
LANES = 128
SUBLANES = 8
FF_CHUNK = 256
N_CHUNK = DFF // FF_CHUNK
VMEM_LIMIT = 50 * 1024 * 1024


def _params(sem):
    return pltpu.CompilerParams(dimension_semantics=sem, vmem_limit_bytes=VMEM_LIMIT)


def _rms(x, g):
    return x * lax.rsqrt(jnp.mean(x * x, axis=-1, keepdims=True) + EPS) * g


def _dot(a, b):
    return jnp.dot(a, b, preferred_element_type=F32)


def _dot_nt(a, b):
    return lax.dot_general(a, b, (((1,), (1,)), ((), ())), preferred_element_type=F32)


def _mod_kernel(cond_ref, w_ref, b_ref, o_ref):
    s = cond_ref[...]
    s = s * jax.nn.sigmoid(s)
    o_ref[0] = _dot(s.astype(BF), w_ref[0].astype(BF)) + b_ref[0]


def _adaln(cond8, ada_w, ada_b):
    tn = 1536
    return pl.pallas_call(
        _mod_kernel,
        grid=(DEPTH, 6 * D // tn),
        in_specs=[
            pl.BlockSpec((SUBLANES, D), lambda l, n: (0, 0)),
            pl.BlockSpec((1, D, tn), lambda l, n: (l, 0, n)),
            pl.BlockSpec((1, 1, tn), lambda l, n: (l, 0, n)),
        ],
        out_specs=pl.BlockSpec((1, SUBLANES, tn), lambda l, n: (l, 0, n)),
        out_shape=jax.ShapeDtypeStruct((DEPTH, SUBLANES, 6 * D), F32),
        compiler_params=_params(("arbitrary", "arbitrary")),
        name="adaln",
    )(cond8, ada_w, ada_b.reshape(DEPTH, 1, 6 * D))


def _qkv_kernel(*refs, gqa, rope):
    if gqa and rope:
        (x_ref, mod_ref, g_ref, w_ref, bd_ref, qkg_ref, cos_ref, s1_ref, s2_ref,
         q_ref, k_ref, v_ref) = refs
    elif gqa:
        x_ref, mod_ref, g_ref, w_ref, bd_ref, qkg_ref, q_ref, k_ref, v_ref = refs
    else:
        x_ref, mod_ref, g_ref, w_ref, q_ref, k_ref, v_ref = refs
    m = mod_ref[0]
    sh = m[:, 0:D]
    sc = m[:, D:2 * D]
    h = _rms(x_ref[...], g_ref[...]) * (1.0 + sc) + sh
    qkv = _dot(h.astype(BF), w_ref[...])
    if not gqa:
        q_ref[...] = (qkv[:, :D] * 0.125).astype(q_ref.dtype)
        k_ref[...] = qkv[:, D:2 * D].astype(k_ref.dtype)
        v_ref[...] = qkv[:, 2 * D:].astype(v_ref.dtype)
        return
    nqk = D + NKV * HD
    outs = []
    for t in range(nqk // LANES):
        pair = t // 2
        if t % 2 == 0:
            blk = qkv[:, pair * 256:(pair + 1) * 256]
            ss = _dot((blk * blk).astype(BF), bd_ref[...])
            nrm = blk * lax.rsqrt(ss * (1.0 / HD) + EPS)
        xt = nrm[:, (t % 2) * LANES:(t % 2 + 1) * LANES] * qkg_ref[:, t * LANES:(t + 1) * LANES]
        if rope:
            xt = (xt * cos_ref[...]
                  + pltpu.roll(xt, LANES - 16, 1) * s1_ref[...]
                  + pltpu.roll(xt, 16, 1) * s2_ref[...])
        outs.append(xt)
    for t in range(D // LANES):
        q_ref[:, t * LANES:(t + 1) * LANES] = outs[t].astype(q_ref.dtype)
    for t in range(NKV * HD // LANES):
        k_ref[:, t * LANES:(t + 1) * LANES] = outs[D // LANES + t].astype(k_ref.dtype)
    v_ref[...] = qkv[:, nqk:].astype(v_ref.dtype)


def _qkv(x, modl, mod_map, g, w, *, gqa, rope, kv_dtype, tm, extra=()):
    n = x.shape[0]
    nq = w.shape[1]
    kvw = NKV * HD if gqa else D
    in_specs = [
        pl.BlockSpec((tm, D), lambda t: (t, 0)),
        pl.BlockSpec((1, 1, 6 * D), mod_map),
        pl.BlockSpec((1, D), lambda t: (0, 0)),
        pl.BlockSpec((D, nq), lambda t: (0, 0)),
    ]
    args = [x, modl, g.reshape(1, D), w]
    if gqa:
        bd, qkg = extra[0], extra[1]
        in_specs += [pl.BlockSpec((256, 256), lambda t: (0, 0)),
                     pl.BlockSpec((1, D + NKV * HD), lambda t: (0, 0))]
        args += [bd, qkg]
    if rope:
        tps = DEC_SEQ // tm
        for tab in extra[2:]:
            in_specs.append(pl.BlockSpec((tm, LANES), lambda t: (t % tps, 0)))
            args.append(tab)
    return pl.pallas_call(
        functools.partial(_qkv_kernel, gqa=gqa, rope=rope),
        grid=(n // tm,),
        in_specs=in_specs,
        out_specs=[
            pl.BlockSpec((tm, D), lambda t: (t, 0)),
            pl.BlockSpec((tm, kvw), lambda t: (t, 0)),
            pl.BlockSpec((tm, kvw), lambda t: (t, 0)),
        ],
        out_shape=[
            jax.ShapeDtypeStruct((n, D), BF),
            jax.ShapeDtypeStruct((n, kvw), kv_dtype),
            jax.ShapeDtypeStruct((n, kvw), kv_dtype),
        ],
        compiler_params=_params(("arbitrary",)),
        name="qkv_gqa" if gqa else "qkv_na",
    )(*args)


def _softmax_pv(scores, values):
    m = scores[0].max(axis=-1, keepdims=True)
    for s in scores[1:]:
        m = jnp.maximum(m, s.max(axis=-1, keepdims=True))
    es = [jnp.exp(s - m) for s in scores]
    l = es[0].sum(axis=-1, keepdims=True)
    for e in es[1:]:
        l = l + e.sum(axis=-1, keepdims=True)
    pv = _dot(es[0].astype(BF), values[0])
    for e, v in zip(es[1:], values[1:]):
        pv = pv + _dot(e.astype(BF), v)
    return pv / l


def _lane_lo(rows):
    return lax.broadcasted_iota(jnp.int32, (rows, LANES), 1) < HD


def _attn_pairs(q_ref, kv_tiles, o_ref, group):
    tq = q_ref.shape[0]
    lo = _lane_lo(tq)
    for hp in range(NH // 2):
        q2 = q_ref[:, hp * LANES:(hp + 1) * LANES]
        o2 = None
        for half in range(2):
            kvh = (2 * hp + half) // group
            khalf = kvh % 2
            qq = q2
            if half != khalf:
                qq = pltpu.roll(q2.astype(F32), HD, 1).astype(BF)
            qsel = jnp.where(lo if khalf == 0 else jnp.logical_not(lo), qq, jnp.zeros_like(qq))
            kvs = kv_tiles(kvh // 2)
            pv = _softmax_pv([_dot_nt(qsel, k) for k, _ in kvs], [v for _, v in kvs])
            if half != khalf:
                pv = pltpu.roll(pv, HD, 1)
            o2 = pv if half == 0 else jnp.where(lo, o2, pv)
        o_ref[:, hp * LANES:(hp + 1) * LANES] = o2.astype(o_ref.dtype)


def _attn_prompt_kernel(q_ref, k_ref, v_ref, o_ref, *, group):
    def kv_tiles(t):
        sl = slice(t * LANES, (t + 1) * LANES)
        return [(k_ref[:, sl].astype(BF), v_ref[:, sl].astype(BF))]
    _attn_pairs(q_ref, kv_tiles, o_ref, group)


def _attn_prompt(q, k, v, group):
    kvw = k.shape[1]
    return pl.pallas_call(
        functools.partial(_attn_prompt_kernel, group=group),
        grid=(BATCH,),
        in_specs=[
            pl.BlockSpec((SEQ, D), lambda b: (b, 0)),
            pl.BlockSpec((SEQ, kvw), lambda b: (b, 0)),
            pl.BlockSpec((SEQ, kvw), lambda b: (b, 0)),
        ],
        out_specs=pl.BlockSpec((SEQ, D), lambda b: (b, 0)),
        out_shape=jax.ShapeDtypeStruct((NP, D), BF),
        compiler_params=_params(("arbitrary",)),
        name="attn_prompt",
    )(q, k, v)


def _attn_gqa_sample_kernel(q_ref, k_ref, v_ref, kc_ref, vc_ref, o_ref):
    def kv_tiles(t):
        sl = slice(t * LANES, (t + 1) * LANES)
        return [(k_ref[:, sl], v_ref[:, sl]),
                (kc_ref[0, 0, :, sl].astype(BF), vc_ref[0, 0, :, sl].astype(BF))]
    _attn_pairs(q_ref, kv_tiles, o_ref, NH // NKV)


def _attn_gqa_sample(q, k, v, cache_k, cache_v, i):
    tq = 256
    nq = DEC_SEQ // tq
    kvw = NKV * HD
    return pl.pallas_call(
        _attn_gqa_sample_kernel,
        grid=(DEC_BATCH, nq),
        in_specs=[
            pl.BlockSpec((tq, D), lambda b, t: (b * nq + t, 0)),
            pl.BlockSpec((DEC_SEQ, kvw), lambda b, t: (b, 0)),
            pl.BlockSpec((DEC_SEQ, kvw), lambda b, t: (b, 0)),
            pl.BlockSpec((1, 1, PAST, kvw), lambda b, t: (b, i, 0, 0)),
            pl.BlockSpec((1, 1, PAST, kvw), lambda b, t: (b, i, 0, 0)),
        ],
        out_specs=pl.BlockSpec((tq, D), lambda b, t: (b * nq + t, 0)),
        out_shape=jax.ShapeDtypeStruct((NS, D), BF),
        compiler_params=_params(("arbitrary", "arbitrary")),
        name="attn_gqa_sample",
    )(q, k, v, cache_k, cache_v)


NA_QROWS = 8
NA_KROWS = 16


def _attn_na_sample_kernel(q_ref, k_ref, v_ref, kc_ref, vc_ref, p_ref, o_ref, s_scr):
    rb = pl.program_id(2)
    r0 = rb * NA_QROWS
    kstart = jnp.clip(r0 - NA_KH // 2, 0, ROWS - NA_KROWS)
    koff = pl.multiple_of(kstart * GRID_W, 256)
    nkw = NA_KROWS * GRID_W
    kwin = k_ref[pl.ds(koff, nkw), :]
    vwin = v_ref[pl.ds(koff, nkw), :]
    kc = kc_ref[0, 0].astype(BF)
    vc = vc_ref[0, 0].astype(BF)
    tq = q_ref.shape[0]
    lo = _lane_lo(tq)
    lo_w = _lane_lo(GRID_W)
    q2 = q_ref[...]
    o2 = None
    for half in range(2):
        qsel = jnp.where(lo if half == 0 else jnp.logical_not(lo), q2, jnp.zeros_like(q2))
        s_scr[...] = _dot_nt(qsel, kwin)
        for i in range(NA_QROWS):
            qr = r0 + i
            rs = jnp.clip(qr - NA_KH // 2, 0, ROWS - NA_KH)
            for jj in range(NA_KROWS // 2):
                kra = kstart + 2 * jj
                krb = kra + 1
                idx = jnp.clip(kra - qr + NA_KH, 0, 2 * NA_KH - 1)
                va = jnp.logical_and(kra >= rs, kra < rs + NA_KH).astype(jnp.int32)
                vb = jnp.logical_and(krb >= rs, krb < rs + NA_KH).astype(jnp.int32)
                valid = jnp.where(lo_w, va, vb) > 0
                rsl = slice(i * GRID_W, (i + 1) * GRID_W)
                csl = slice(jj * LANES, (jj + 1) * LANES)
                s_scr[rsl, csl] = jnp.where(valid, s_scr[rsl, csl] + p_ref[half, idx], NEG)
        pv = _softmax_pv([s_scr[...], _dot_nt(qsel, kc)], [vwin, vc])
        o2 = pv if half == 0 else jnp.where(lo, o2, pv)
    o_ref[...] = o2.astype(o_ref.dtype)


def _attn_na_sample(q, k, v, cache_k, cache_v, i, ptab):
    tq = NA_QROWS * GRID_W
    nrb = ROWS // NA_QROWS
    nhp = NH // 2
    return pl.pallas_call(
        _attn_na_sample_kernel,
        grid=(DEC_BATCH, nhp, nrb),
        in_specs=[
            pl.BlockSpec((tq, LANES), lambda b, hp, r: (b * nrb + r, hp)),
            pl.BlockSpec((DEC_SEQ, LANES), lambda b, hp, r: (b, hp)),
            pl.BlockSpec((DEC_SEQ, LANES), lambda b, hp, r: (b, hp)),
            pl.BlockSpec((1, 1, PAST, LANES), lambda b, hp, r: (b, i, 0, hp)),
            pl.BlockSpec((1, 1, PAST, LANES), lambda b, hp, r: (b, i, 0, hp)),
            pl.BlockSpec((2, 2 * NA_KH, GRID_W, LANES), lambda b, hp, r: (hp, 0, 0, 0)),
        ],
        out_specs=pl.BlockSpec((tq, LANES), lambda b, hp, r: (b * nrb + r, hp)),
        out_shape=jax.ShapeDtypeStruct((NS, D), BF),
        scratch_shapes=[pltpu.VMEM((tq, NA_KROWS * GRID_W), F32)],
        compiler_params=_params(("arbitrary", "arbitrary", "arbitrary")),
        name="attn_na_sample",
    )(q, k, v, cache_k, cache_v, ptab)


def _wo_kernel(o_ref, x_ref, mod_ref, g_ref, w_ref, out_ref):
    y = _dot(o_ref[...], w_ref[...])
    gate = mod_ref[0][:, 2 * D:3 * D]
    out_ref[...] = x_ref[...] + gate * _rms(y, g_ref[...])


def _wo(o, x, modl, mod_map, g, w, tm):
    n = x.shape[0]
    return pl.pallas_call(
        _wo_kernel,
        grid=(n // tm,),
        in_specs=[
            pl.BlockSpec((tm, D), lambda t: (t, 0)),
            pl.BlockSpec((tm, D), lambda t: (t, 0)),
            pl.BlockSpec((1, 1, 6 * D), mod_map),
            pl.BlockSpec((1, D), lambda t: (0, 0)),
            pl.BlockSpec((D, D), lambda t: (0, 0)),
        ],
        out_specs=pl.BlockSpec((tm, D), lambda t: (t, 0)),
        out_shape=jax.ShapeDtypeStruct((n, D), F32),
        compiler_params=_params(("arbitrary",)),
        name="wo",
    )(o, x, modl, g.reshape(1, D), w)


def _ffn_kernel(*refs, halo, tiles_per_seq):
    if halo:
        (x_ref, xp_ref, xn_ref, mod_ref, gpre_ref, gpost_ref, wup_ref, cw_ref, cb_ref,
         wdn_ref, out_ref, act_scr) = refs
    else:
        (x_ref, mod_ref, gpre_ref, gpost_ref, wup_ref, cw_ref, cb_ref,
         wdn_ref, out_ref, act_scr) = refs
    tm = x_ref.shape[0]
    m = mod_ref[0]
    sh = m[:, 3 * D:4 * D]
    sc = m[:, 4 * D:5 * D]
    gate = m[:, 5 * D:6 * D]

    def pre(xv):
        return _rms(xv, gpre_ref[...]) * (1.0 + sc) + sh

    x = x_ref[...]
    h = pre(x)
    zeros = jnp.zeros((SUBLANES, D), F32)
    if halo:
        t = pl.program_id(0) % tiles_per_seq
        hp = jnp.where(t == 0, zeros, pre(xp_ref[...]))
        hn = jnp.where(t == tiles_per_seq - 1, zeros, pre(xn_ref[...]))
    else:
        hp = zeros
        hn = zeros
    hext = jnp.concatenate([hp, h, hn], axis=0).astype(BF)
    mext = tm + 2 * SUBLANES

    def conv(u, c):
        prev = pltpu.roll(u, 1, 0)[SUBLANES:SUBLANES + tm]
        nxt = pltpu.roll(u, mext - 1, 0)[SUBLANES:SUBLANES + tm]
        cur = u[SUBLANES:SUBLANES + tm]
        return prev * cw_ref[0, c] + cur * cw_ref[1, c] + nxt * cw_ref[2, c] + cb_ref[c]

    for j in range(N_CHUNK):
        a = conv(_dot(hext, wup_ref[j]), j)
        gt = conv(_dot(hext, wup_ref[N_CHUNK + j]), N_CHUNK + j)
        act_scr[:, j * FF_CHUNK:(j + 1) * FF_CHUNK] = (a * jax.nn.sigmoid(a) * gt).astype(BF)
    y = _dot(act_scr[...], wdn_ref[...])
    out_ref[...] = x + gate * _rms(y, gpost_ref[...])


def _ffn(x, modl, mod_map, gpre, gpost, wup, cw, cb, wdn, *, tm, halo):
    n = x.shape[0]
    tps = DEC_SEQ // tm
    r = tm // SUBLANES
    nblk8 = n // SUBLANES
    in_specs = [pl.BlockSpec((tm, D), lambda t: (t, 0))]
    args = [x]
    if halo:
        in_specs += [
            pl.BlockSpec((SUBLANES, D), lambda t: (jnp.maximum(t * r - 1, 0), 0)),
            pl.BlockSpec((SUBLANES, D), lambda t: (jnp.minimum((t + 1) * r, nblk8 - 1), 0)),
        ]
        args += [x, x]
    const = dict(pipeline_mode=pl.Buffered(1))
    in_specs += [
        pl.BlockSpec((1, 1, 6 * D), mod_map),
        pl.BlockSpec((1, D), lambda t: (0, 0)),
        pl.BlockSpec((1, D), lambda t: (0, 0)),
        pl.BlockSpec((2 * N_CHUNK, D, FF_CHUNK), lambda t: (0, 0, 0), **const),
        pl.BlockSpec((3, 2 * N_CHUNK, 1, FF_CHUNK), lambda t: (0, 0, 0, 0)),
        pl.BlockSpec((2 * N_CHUNK, 1, FF_CHUNK), lambda t: (0, 0, 0)),
        pl.BlockSpec((DFF, D), lambda t: (0, 0), **const),
    ]
    args += [modl, gpre.reshape(1, D), gpost.reshape(1, D), wup, cw, cb, wdn]
    return pl.pallas_call(
        functools.partial(_ffn_kernel, halo=halo, tiles_per_seq=tps),
        grid=(n // tm,),
        in_specs=in_specs,
        out_specs=pl.BlockSpec((tm, D), lambda t: (t, 0)),
        out_shape=jax.ShapeDtypeStruct((n, D), F32),
        scratch_shapes=[pltpu.VMEM((tm, DFF), BF)],
        compiler_params=_params(("arbitrary",)),
        name="ffn",
    )(*args)


def _rope_tables():
    t = jnp.arange(DEC_SEQ)
    rows = (t // GRID_W).astype(F32)
    cols = (t % GRID_W).astype(F32)
    quarter = HD // 4
    freqs = ROPE_BASE ** (-jnp.arange(quarter, dtype=F32) / quarter)
    lane = jnp.arange(LANES)
    d = lane % HD
    e = d % (HD // 2)
    is_x2 = (e >= quarter)[None, :]
    pos = jnp.where((d // (HD // 2) == 0)[None, :], rows[:, None], cols[:, None])
    ang = pos * freqs[e % quarter][None, :]
    cos = jnp.cos(ang)
    sin = jnp.sin(ang)
    return cos, jnp.where(is_x2, 0.0, -sin), jnp.where(is_x2, sin, 0.0)


def _na_bias_table(rpb):
    qc = jnp.arange(GRID_W)[:, None]
    kc = jnp.arange(GRID_W)[None, :]
    c_start = jnp.clip(qc - NA_KW // 2, 0, GRID_W - NA_KW)
    ok = (kc >= c_start) & (kc < c_start + NA_KW)
    dc = jnp.clip(kc - qc + (NA_KW - 1), 0, 2 * NA_KW - 2)
    bt = jnp.where(ok[None, None], rpb[:, :, dc], NEG)
    negs = jnp.full((NH, 1, GRID_W, GRID_W), NEG, F32)
    left = jnp.concatenate([negs, bt], axis=1)
    right = jnp.concatenate([bt, negs], axis=1)
    return jnp.concatenate([left, right], axis=-1)


def _block_diag_ones():
    r = jnp.arange(256)
    return (r[:, None] // HD == r[None, :] // HD).astype(BF)


def kernel(x_prompt, x_sample, cache_na_k, cache_na_v, cache_gqa_k, cache_gqa_v, c, c_ctx,
           ada_w, ada_b, norm_mix_pre, norm_mix_post, norm_ffn_pre, norm_ffn_post,
           na_w_qkv, na_w_o, na_rpb, gqa_w_qkv, gqa_w_o, gqa_q_norm, gqa_k_norm,
           ffn_w_up, ffn_conv_w, ffn_conv_b, ffn_w_down):
    n_a = cache_na_k.shape[1]
    n_b = cache_gqa_k.shape[1]
    cond8 = jnp.concatenate([c_ctx[None], c, jnp.zeros((SUBLANES - 1 - DEC_BATCH, D), F32)], axis=0)
    mod = _adaln(cond8, ada_w, ada_b)

    tm_p, tm_s = 512, 512
    tm_fp, tm_fs = 256, 512
    map_p = lambda t: (0, 0, 0)
    map_s = lambda t: (1 + t // (DEC_SEQ // tm_s), 0, 0)
    map_fs = lambda t: (1 + t // (DEC_SEQ // tm_fs), 0, 0)

    cna_k = cache_na_k.reshape(DEC_BATCH, n_a, PAST, D)
    cna_v = cache_na_v.reshape(DEC_BATCH, n_a, PAST, D)
    cg_k = cache_gqa_k.reshape(DEC_BATCH, n_b, PAST, NKV * HD)
    cg_v = cache_gqa_v.reshape(DEC_BATCH, n_b, PAST, NKV * HD)
    rope_tabs = _rope_tables()
    bd = _block_diag_ones()

    xp = x_prompt.reshape(NP, D)
    xs = x_sample.reshape(NS, D)
    na_k, na_v, gq_k, gq_v = [], [], [], []
    for l in range(DEPTH):
        i = l // 2
        modl = mod[l].reshape(SUBLANES, 1, 6 * D)
        if l % 2 == 0:
            w = na_w_qkv[i].astype(BF)
            w_o = na_w_o[i].astype(BF)
            qp, kp, vp = _qkv(xp, modl, map_p, norm_mix_pre[l], w, gqa=False, rope=False,
                              kv_dtype=F32, tm=tm_p)
            qs, ks, vs = _qkv(xs, modl, map_s, norm_mix_pre[l], w, gqa=False, rope=False,
                              kv_dtype=BF, tm=tm_s)
            na_k.append(kp)
            na_v.append(vp)
            op = _attn_prompt(qp, kp, vp, 1)
            os_ = _attn_na_sample(qs, ks, vs, cna_k, cna_v, i, _na_bias_table(na_rpb[i]))
        else:
            w = gqa_w_qkv[i].astype(BF)
            w_o = gqa_w_o[i].astype(BF)
            qkg = jnp.concatenate([jnp.tile(gqa_q_norm[i], NH) * 0.125,
                                   jnp.tile(gqa_k_norm[i], NKV)]).reshape(1, D + NKV * HD)
            qp, kp, vp = _qkv(xp, modl, map_p, norm_mix_pre[l], w, gqa=True, rope=False,
                              kv_dtype=F32, tm=tm_p, extra=(bd, qkg))
            qs, ks, vs = _qkv(xs, modl, map_s, norm_mix_pre[l], w, gqa=True, rope=True,
                              kv_dtype=BF, tm=tm_s, extra=(bd, qkg) + rope_tabs)
            gq_k.append(kp)
            gq_v.append(vp)
            op = _attn_prompt(qp, kp, vp, NH // NKV)
            os_ = _attn_gqa_sample(qs, ks, vs, cg_k, cg_v, i)
        xp = _wo(op, xp, modl, map_p, norm_mix_post[l], w_o, tm_p)
        xs = _wo(os_, xs, modl, map_s, norm_mix_post[l], w_o, tm_s)

        wup = ffn_w_up[l].astype(BF).reshape(D, 2 * N_CHUNK, FF_CHUNK).transpose(1, 0, 2)
        wdn = ffn_w_down[l].astype(BF)
        cw = ffn_conv_w[l].reshape(3, 2 * N_CHUNK, 1, FF_CHUNK)
        cb = ffn_conv_b[l].reshape(2 * N_CHUNK, 1, FF_CHUNK)
        xp = _ffn(xp, modl, map_p, norm_ffn_pre[l], norm_ffn_post[l], wup, cw, cb, wdn,
                  tm=tm_fp, halo=False)
        xs = _ffn(xs, modl, map_fs, norm_ffn_pre[l], norm_ffn_post[l], wup, cw, cb, wdn,
                  tm=tm_fs, halo=True)

    def stack(parts, heads):
        return jnp.stack([p.reshape(BATCH, SEQ, heads, HD) for p in parts], axis=1)

    return (xp.reshape(BATCH, SEQ, D), xs.reshape(DEC_BATCH, DEC_SEQ, D),
            stack(na_k, NH), stack(na_v, NH), stack(gq_k, NKV), stack(gq_v, NKV))
```

```python
import functools

import jax
import jax.numpy as jnp
from jax import lax
from jax.experimental import pallas as pl
from jax.experimental.pallas import tpu as pltpu

D = 1024
HD = 64
NH = 16
NKV = 4
DFF = 2816
DEPTH = 4
GRID_W = 64
ROWS = 32
NA_KH = 8
NA_KW = 16
ROPE_BASE = 10000.0
EPS = 1e-6
BATCH, SEQ = 32, 256
DEC_BATCH, DEC_SEQ = 2, 2048
PAST = 256
NP = BATCH * SEQ
NS = DEC_BATCH * DEC_SEQ

BF = jnp.bfloat16
F32 = jnp.float32
NEG = -1e30
LANES = 128
SUBLANES = 8
FF_CHUNK = 256
N_CHUNK = DFF // FF_CHUNK
VMEM_LIMIT = 50 * 1024 * 1024


def _params(sem):
    return pltpu.CompilerParams(dimension_semantics=sem, vmem_limit_bytes=VMEM_LIMIT)


def _rms(x, g):
    return x * lax.rsqrt(jnp.mean(x * x, axis=-1, keepdims=True) + EPS) * g


def _dot(a, b):
    return jnp.dot(a, b, preferred_element_type=F32)


def _dot_nt(a, b):
    return lax.dot_general(a, b, (((1,), (1,)), ((), ())), preferred_element_type=F32)


def _mod_kernel(cond_ref, w_ref, b_ref, o_ref):
    s = cond_ref[...]
    s = s * jax.nn.sigmoid(s)
    o_ref[0] = _dot(s.astype(BF), w_ref[0].astype(BF)) + b_ref[0]


def _adaln(cond8, ada_w, ada_b):
    tn = 1536
    return pl.pallas_call(
        _mod_kernel,
        grid=(DEPTH, 6 * D // tn),
        in_specs=[
            pl.BlockSpec((SUBLANES, D), lambda l, n: (0, 0)),
            pl.BlockSpec((1, D, tn), lambda l, n: (l, 0, n)),
            pl.BlockSpec((1, 1, tn), lambda l, n: (l, 0, n)),
        ],
        out_specs=pl.BlockSpec((1, SUBLANES, tn), lambda l, n: (l, 0, n)),
        out_shape=jax.ShapeDtypeStruct((DEPTH, SUBLANES, 6 * D), F32),
        compiler_params=_params(("arbitrary", "arbitrary")),
        name="adaln",
    )(cond8, ada_w, ada_b.reshape(DEPTH, 1, 6 * D))


def _prenorm(x_ref, mod_ref, g_ref):
    m = mod_ref[0]
    return (_rms(x_ref[...], g_ref[...]) * (1.0 + m[:, D:2 * D]) + m[:, 0:D]).astype(BF)


def _head_rms_lanes(blk, bd_ref):
    ss = _dot((blk * blk).astype(BF), bd_ref[...])
    return blk * lax.rsqrt(ss * (1.0 / HD) + EPS)


def _qkv_prompt_kernel(*refs, gqa):
    if gqa:
        (x_ref, mod_ref, g_ref, wqv_ref, wkt_ref, bd_ref, qg_ref, kg_ref,
         q_ref, v_ref, kt_ref, vt_ref) = refs
    else:
        x_ref, mod_ref, g_ref, wqv_ref, wkt_ref, q_ref, v_ref, kt_ref, vt_ref = refs
    hb = _prenorm(x_ref, mod_ref, g_ref)
    qv = _dot(hb, wqv_ref[...])
    kt = _dot_nt(wkt_ref[...], hb)
    if gqa:
        for t in range(D // 256):
            sl = slice(t * 256, (t + 1) * 256)
            q_ref[:, sl] = (_head_rms_lanes(qv[:, sl], bd_ref) * qg_ref[:, sl]).astype(q_ref.dtype)
        heads = []
        for h in range(NKV):
            blk = kt[h * HD:(h + 1) * HD, :]
            heads.append(blk * lax.rsqrt(jnp.mean(blk * blk, axis=0, keepdims=True) + EPS))
        kt = jnp.concatenate(heads, axis=0)
    else:
        q_ref[...] = (qv[:, :D] * 0.125).astype(q_ref.dtype)
    v = qv[:, D:]
    v_ref[...] = v.astype(v_ref.dtype)
    for s in range(x_ref.shape[0] // SEQ):
        sl = slice(s * SEQ, (s + 1) * SEQ)
        kts = kt[:, sl]
        if gqa:
            kts = kts * kg_ref[...]
        kt_ref[s, 0] = kts
        vt_ref[s, 0] = v[sl, :].T


def _qkv_prompt(x, modl, g, wqv, wkt, i, n_layers, prev, *, gqa, tm, extra=()):
    kvw = NKV * HD if gqa else D
    in_specs = [
        pl.BlockSpec((tm, D), lambda t: (t, 0)),
        pl.BlockSpec((1, 1, 6 * D), lambda t: (0, 0, 0)),
        pl.BlockSpec((1, D), lambda t: (0, 0)),
        pl.BlockSpec((D, D + kvw), lambda t: (0, 0)),
        pl.BlockSpec((kvw, D), lambda t: (0, 0)),
    ]
    args = [x, modl, g.reshape(1, D), wqv, wkt]
    if gqa:
        in_specs += [pl.BlockSpec((256, 256), lambda t: (0, 0)),
                     pl.BlockSpec((1, D), lambda t: (0, 0)),
                     pl.BlockSpec((kvw, SEQ), lambda t: (0, 0))]
        args += list(extra)
    n_in = len(args)
    aliases = {}
    if prev is not None:
        in_specs += [pl.BlockSpec(memory_space=pl.ANY), pl.BlockSpec(memory_space=pl.ANY)]
        args += list(prev)
        aliases = {n_in: 2, n_in + 1: 3}
    kv_spec = pl.BlockSpec((tm // SEQ, 1, kvw, SEQ), lambda t: (t, i, 0, 0))
    kv_shape = jax.ShapeDtypeStruct((BATCH, n_layers, kvw, SEQ), F32)

    def body(*refs):
        _qkv_prompt_kernel(*(refs[:n_in] + refs[len(args):]), gqa=gqa)

    return pl.pallas_call(
        body,
        grid=(NP // tm,),
        in_specs=in_specs,
        out_specs=[
            pl.BlockSpec((tm, D), lambda t: (t, 0)),
            pl.BlockSpec((tm, kvw), lambda t: (t, 0)),
            kv_spec, kv_spec,
        ],
        out_shape=[
            jax.ShapeDtypeStruct((NP, D), BF),
            jax.ShapeDtypeStruct((NP, kvw), BF),
            kv_shape, kv_shape,
        ],
        input_output_aliases=aliases,
        compiler_params=_params(("arbitrary",)),
        name="qkv_prompt_gqa" if gqa else "qkv_prompt_na",
    )(*args)


def _qkv_sample_kernel(*refs, gqa):
    if gqa:
        (x_ref, mod_ref, g_ref, w_ref, bd_ref, qkg_ref, cos_ref, s1_ref, s2_ref,
         q_ref, k_ref, v_ref) = refs
    else:
        x_ref, mod_ref, g_ref, w_ref, q_ref, k_ref, v_ref = refs
    qkv = _dot(_prenorm(x_ref, mod_ref, g_ref), w_ref[...])
    if not gqa:
        q_ref[...] = (qkv[:, :D] * 0.125).astype(q_ref.dtype)
        k_ref[...] = qkv[:, D:2 * D].astype(k_ref.dtype)
        v_ref[...] = qkv[:, 2 * D:].astype(v_ref.dtype)
        return
    nqk = D + NKV * HD
    for t in range(nqk // LANES):
        if t % 2 == 0:
            nrm = _head_rms_lanes(qkv[:, (t // 2) * 256:(t // 2 + 1) * 256], bd_ref)
        xt = nrm[:, (t % 2) * LANES:(t % 2 + 1) * LANES] * qkg_ref[:, t * LANES:(t + 1) * LANES]
        xt = (xt * cos_ref[...]
              + pltpu.roll(xt, LANES - 16, 1) * s1_ref[...]
              + pltpu.roll(xt, 16, 1) * s2_ref[...])
        if t < D // LANES:
            q_ref[:, t * LANES:(t + 1) * LANES] = xt.astype(q_ref.dtype)
        else:
            tk = t - D // LANES
            k_ref[:, tk * LANES:(tk + 1) * LANES] = xt.astype(k_ref.dtype)
    v_ref[...] = qkv[:, nqk:].astype(v_ref.dtype)


def _qkv_sample(x, modl, mod_map, g, w, *, gqa, tm, extra=()):
    nq = w.shape[1]
    kvw = NKV * HD if gqa else D
    in_specs = [
        pl.BlockSpec((tm, D), lambda t: (t, 0)),
        pl.BlockSpec((1, 1, 6 * D), mod_map),
        pl.BlockSpec((1, D), lambda t: (0, 0)),
        pl.BlockSpec((D, nq), lambda t: (0, 0)),
    ]
    args = [x, modl, g.reshape(1, D), w]
    if gqa:
        in_specs += [pl.BlockSpec((256, 256), lambda t: (0, 0)),
                     pl.BlockSpec((1, D + NKV * HD), lambda t: (0, 0))]
        args += list(extra[:2])
        tps = DEC_SEQ // tm
        for tab in extra[2:]:
            in_specs.append(pl.BlockSpec((tm, LANES), lambda t: (t % tps, 0)))
            args.append(tab)
    return pl.pallas_call(
        functools.partial(_qkv_sample_kernel, gqa=gqa),
        grid=(NS // tm,),
        in_specs=in_specs,
        out_specs=[
            pl.BlockSpec((tm, D), lambda t: (t, 0)),
            pl.BlockSpec((tm, kvw), lambda t: (t, 0)),
            pl.BlockSpec((tm, kvw), lambda t: (t, 0)),
        ],
        out_shape=[
            jax.ShapeDtypeStruct((NS, D), BF),
            jax.ShapeDtypeStruct((NS, kvw), BF),
            jax.ShapeDtypeStruct((NS, kvw), BF),
        ],
        compiler_params=_params(("arbitrary",)),
        name="qkv_sample_gqa" if gqa else "qkv_sample_na",
    )(*args)


def _softmax_pv(scores, values):
    m = scores[0].max(axis=-1, keepdims=True)
    for s in scores[1:]:
        m = jnp.maximum(m, s.max(axis=-1, keepdims=True))
    es = [jnp.exp(s - m) for s in scores]
    l = es[0].sum(axis=-1, keepdims=True)
    for e in es[1:]:
        l = l + e.sum(axis=-1, keepdims=True)
    pv = _dot(es[0].astype(BF), values[0])
    for e, v in zip(es[1:], values[1:]):
        pv = pv + _dot(e.astype(BF), v)
    return pv / l


def _lane_lo(rows):
    return lax.broadcasted_iota(jnp.int32, (rows, LANES), 1) < HD


def _attn_pairs(q_ref, kv_tiles, o_ref, group):
    tq = q_ref.shape[0]
    lo = _lane_lo(tq)
    for hp in range(NH // 2):
        q2 = q_ref[:, hp * LANES:(hp + 1) * LANES]
        o2 = None
        for half in range(2):
            kvh = (2 * hp + half) // group
            khalf = kvh % 2
            qq = q2
            if half != khalf:
                qq = pltpu.roll(q2.astype(F32), HD, 1).astype(BF)
            qsel = jnp.where(lo if khalf == 0 else jnp.logical_not(lo), qq, jnp.zeros_like(qq))
            kvs = kv_tiles(kvh // 2)
            scores = [_dot(qsel, k) if kt else _dot_nt(qsel, k) for k, _, kt in kvs]
            pv = _softmax_pv(scores, [v for _, v, _ in kvs])
            if half != khalf:
                pv = pltpu.roll(pv, HD, 1)
            o2 = pv if half == 0 else jnp.where(lo, o2, pv)
        o_ref[:, hp * LANES:(hp + 1) * LANES] = o2.astype(o_ref.dtype)


def _attn_prompt_kernel(q_ref, kt_ref, v_ref, o_ref, *, group):
    def kv_tiles(t):
        sl = slice(t * LANES, (t + 1) * LANES)
        return [(kt_ref[0, 0, sl, :].astype(BF), v_ref[:, sl], True)]
    _attn_pairs(q_ref, kv_tiles, o_ref, group)


def _attn_prompt(q, kt_all, v, i, group):
    kvw = v.shape[1]
    return pl.pallas_call(
        functools.partial(_attn_prompt_kernel, group=group),
        grid=(BATCH,),
        in_specs=[
            pl.BlockSpec((SEQ, D), lambda b: (b, 0)),
            pl.BlockSpec((1, 1, kvw, SEQ), lambda b: (b, i, 0, 0)),
            pl.BlockSpec((SEQ, kvw), lambda b: (b, 0)),
        ],
        out_specs=pl.BlockSpec((SEQ, D), lambda b: (b, 0)),
        out_shape=jax.ShapeDtypeStruct((NP, D), BF),
        compiler_params=_params(("arbitrary",)),
        name="attn_prompt",
    )(q, kt_all, v)


def _attn_gqa_sample_kernel(q_ref, k_ref, v_ref, kct_ref, vct_ref, o_ref):
    vc = vct_ref[0, 0].T.astype(BF)

    def kv_tiles(t):
        sl = slice(t * LANES, (t + 1) * LANES)
        return [(k_ref[:, sl], v_ref[:, sl], False),
                (kct_ref[0, 0, sl, :].astype(BF), vc[:, sl], True)]
    _attn_pairs(q_ref, kv_tiles, o_ref, NH // NKV)


def _attn_gqa_sample(q, k, v, cache_kt, cache_vt, i):
    tq = 256
    nq = DEC_SEQ // tq
    kvw = NKV * HD
    return pl.pallas_call(
        _attn_gqa_sample_kernel,
        grid=(DEC_BATCH, nq),
        in_specs=[
            pl.BlockSpec((tq, D), lambda b, t: (b * nq + t, 0)),
            pl.BlockSpec((DEC_SEQ, kvw), lambda b, t: (b, 0)),
            pl.BlockSpec((DEC_SEQ, kvw), lambda b, t: (b, 0)),
            pl.BlockSpec((1, 1, kvw, PAST), lambda b, t: (b, i, 0, 0)),
            pl.BlockSpec((1, 1, kvw, PAST), lambda b, t: (b, i, 0, 0)),
        ],
        out_specs=pl.BlockSpec((tq, D), lambda b, t: (b * nq + t, 0)),
        out_shape=jax.ShapeDtypeStruct((NS, D), BF),
        compiler_params=_params(("arbitrary", "arbitrary")),
        name="attn_gqa_sample",
    )(q, k, v, cache_kt, cache_vt)


NA_QROWS = 8
NA_KROWS = 16


def _attn_na_sample_kernel(q_ref, k_ref, v_ref, kct_ref, vct_ref, p_ref, o_ref, s_scr):
    rb = pl.program_id(2)
    r0 = rb * NA_QROWS
    kstart = jnp.clip(r0 - NA_KH // 2, 0, ROWS - NA_KROWS)
    koff = pl.multiple_of(kstart * GRID_W, 256)
    nkw = NA_KROWS * GRID_W
    kwin = k_ref[pl.ds(koff, nkw), :]
    vwin = v_ref[pl.ds(koff, nkw), :]
    kct = kct_ref[0, 0].astype(BF)
    vc = vct_ref[0, 0].T.astype(BF)
    tq = q_ref.shape[0]
    lo = _lane_lo(tq)
    lo_w = _lane_lo(GRID_W)
    q2 = q_ref[...]
    o2 = None
    for half in range(2):
        qsel = jnp.where(lo if half == 0 else jnp.logical_not(lo), q2, jnp.zeros_like(q2))
        s_scr[...] = _dot_nt(qsel, kwin)
        for i in range(NA_QROWS):
            qr = r0 + i
            rs = jnp.clip(qr - NA_KH // 2, 0, ROWS - NA_KH)
            for jj in range(NA_KROWS // 2):
                kra = kstart + 2 * jj
                krb = kra + 1
                idx = jnp.clip(kra - qr + NA_KH, 0, 2 * NA_KH - 1)
                va = jnp.logical_and(kra >= rs, kra < rs + NA_KH).astype(jnp.int32)
                vb = jnp.logical_and(krb >= rs, krb < rs + NA_KH).astype(jnp.int32)
                valid = jnp.where(lo_w, va, vb) > 0
                rsl = slice(i * GRID_W, (i + 1) * GRID_W)
                csl = slice(jj * LANES, (jj + 1) * LANES)
                s_scr[rsl, csl] = jnp.where(valid, s_scr[rsl, csl] + p_ref[0, half, idx], NEG)
        pv = _softmax_pv([s_scr[...], _dot(qsel, kct)], [vwin, vc])
        o2 = pv if half == 0 else jnp.where(lo, o2, pv)
    o_ref[...] = o2.astype(o_ref.dtype)


def _attn_na_sample(q, k, v, cache_kt, cache_vt, i, ptab):
    tq = NA_QROWS * GRID_W
    nrb = ROWS // NA_QROWS
    nhp = NH // 2
    return pl.pallas_call(
        _attn_na_sample_kernel,
        grid=(DEC_BATCH, nhp, nrb),
        in_specs=[
            pl.BlockSpec((tq, LANES), lambda b, hp, r: (b * nrb + r, hp)),
            pl.BlockSpec((DEC_SEQ, LANES), lambda b, hp, r: (b, hp)),
            pl.BlockSpec((DEC_SEQ, LANES), lambda b, hp, r: (b, hp)),
            pl.BlockSpec((1, 1, LANES, PAST), lambda b, hp, r: (b, i, hp, 0)),
            pl.BlockSpec((1, 1, LANES, PAST), lambda b, hp, r: (b, i, hp, 0)),
            pl.BlockSpec((1, 2, 2 * NA_KH, GRID_W, LANES), lambda b, hp, r: (i, hp, 0, 0, 0)),
        ],
        out_specs=pl.BlockSpec((tq, LANES), lambda b, hp, r: (b * nrb + r, hp)),
        out_shape=jax.ShapeDtypeStruct((NS, D), BF),
        scratch_shapes=[pltpu.VMEM((tq, NA_KROWS * GRID_W), F32)],
        compiler_params=_params(("arbitrary", "arbitrary", "arbitrary")),
        name="attn_na_sample",
    )(q, k, v, cache_kt, cache_vt, ptab)


N_DC = 2 * NA_KW - 1
N_DR = 2 * NA_KH - 1
DR_GROUP = 5


def _na_bias_kernel(rpb_ref, o_ref):
    base = (pl.program_id(0) * NH + pl.program_id(1)) * (N_DR * N_DC)
    qc = lax.broadcasted_iota(jnp.int32, (GRID_W, LANES), 0)
    lane = lax.broadcasted_iota(jnp.int32, (GRID_W, LANES), 1)
    kc = lane % GRID_W
    diff = kc - qc + (NA_KW - 1)
    c_start = jnp.clip(qc - NA_KW // 2, 0, GRID_W - NA_KW)
    ok = jnp.logical_and(kc >= c_start, kc < c_start + NA_KW)
    lo = lane < GRID_W
    neg = jnp.full((GRID_W, LANES), NEG, F32)
    prev = neg
    for g0 in range(0, N_DR, DR_GROUP):
        accs = [jnp.zeros((GRID_W, LANES), F32) for _ in range(DR_GROUP)]
        for p in range(N_DC):
            hit = diff == p
            for k in range(DR_GROUP):
                accs[k] = jnp.where(hit, rpb_ref[base + (g0 + k) * N_DC + p], accs[k])
        for k in range(DR_GROUP):
            cur = jnp.where(ok, accs[k], neg)
            o_ref[0, 0, g0 + k] = jnp.where(lo, prev, cur)
            prev = cur
    o_ref[0, 0, N_DR] = jnp.where(lo, prev, neg)


def _na_bias_tables(rpb):
    n_a = rpb.shape[0]
    return pl.pallas_call(
        _na_bias_kernel,
        grid=(n_a, NH),
        in_specs=[pl.BlockSpec(memory_space=pltpu.SMEM)],
        out_specs=pl.BlockSpec((1, 1, N_DR + 1, GRID_W, LANES), lambda a, h: (a, h, 0, 0, 0)),
        out_shape=jax.ShapeDtypeStruct((n_a, NH, N_DR + 1, GRID_W, LANES), F32),
        compiler_params=_params(("arbitrary", "arbitrary")),
        name="na_bias",
    )(rpb.reshape(-1))


def _wo_kernel(o_ref, x_ref, mod_ref, g_ref, w_ref, out_ref):
    y = _dot(o_ref[...], w_ref[...])
    gate = mod_ref[0][:, 2 * D:3 * D]
    out_ref[...] = x_ref[...] + gate * _rms(y, g_ref[...])


def _wo(o, x, modl, mod_map, g, w, tm):
    n = x.shape[0]
    return pl.pallas_call(
        _wo_kernel,
        grid=(n // tm,),
        in_specs=[
            pl.BlockSpec((tm, D), lambda t: (t, 0)),
            pl.BlockSpec((tm, D), lambda t: (t, 0)),
            pl.BlockSpec((1, 1, 6 * D), mod_map),
            pl.BlockSpec((1, D), lambda t: (0, 0)),
            pl.BlockSpec((D, D), lambda t: (0, 0)),
        ],
        out_specs=pl.BlockSpec((tm, D), lambda t: (t, 0)),
        out_shape=jax.ShapeDtypeStruct((n, D), F32),
        compiler_params=_params(("arbitrary",)),
        name="wo",
    )(o, x, modl, g.reshape(1, D), w)


def _ffn_kernel(*refs, halo, tiles_per_seq):
    if halo:
        (x_ref, xp_ref, xn_ref, mod_ref, gpre_ref, gpost_ref, wup_ref, cw_ref, cb_ref,
         wdn_ref, out_ref, act_scr) = refs
    else:
        (x_ref, mod_ref, gpre_ref, gpost_ref, wup_ref, cw_ref, cb_ref,
         wdn_ref, out_ref, act_scr) = refs
    tm = x_ref.shape[0]
    m = mod_ref[0]
    sh = m[:, 3 * D:4 * D]
    sc = m[:, 4 * D:5 * D]
    gate = m[:, 5 * D:6 * D]

    def pre(xv):
        return _rms(xv, gpre_ref[...]) * (1.0 + sc) + sh

    x = x_ref[...]
    h = pre(x)
    zeros = jnp.zeros((SUBLANES, D), F32)
    if halo:
        t = pl.program_id(0) % tiles_per_seq
        hp = jnp.where(t == 0, zeros, pre(xp_ref[...]))
        hn = jnp.where(t == tiles_per_seq - 1, zeros, pre(xn_ref[...]))
    else:
        hp = zeros
        hn = zeros
    hext = jnp.concatenate([hp, h, hn], axis=0).astype(BF)
    mext = tm + 2 * SUBLANES

    def conv(u, c0):
        cols = slice(c0, c0 + FF_CHUNK)
        prev = pltpu.roll(u, 1, 0)[SUBLANES:SUBLANES + tm]
        nxt = pltpu.roll(u, mext - 1, 0)[SUBLANES:SUBLANES + tm]
        cur = u[SUBLANES:SUBLANES + tm]
        return (prev * cw_ref[0:1, cols] + cur * cw_ref[1:2, cols] + nxt * cw_ref[2:3, cols]
                + cb_ref[:, cols])

    for j in range(N_CHUNK):
        ca = j * FF_CHUNK
        cg = DFF + j * FF_CHUNK
        a = conv(_dot(hext, wup_ref[:, ca:ca + FF_CHUNK]), ca)
        gt = conv(_dot(hext, wup_ref[:, cg:cg + FF_CHUNK]), cg)
        act_scr[:, ca:ca + FF_CHUNK] = (a * jax.nn.sigmoid(a) * gt).astype(BF)
    y = _dot(act_scr[...], wdn_ref[...])
    out_ref[...] = x + gate * _rms(y, gpost_ref[...])


def _ffn(x, modl, mod_map, gpre, gpost, wup, cw, cb, wdn, *, tm, halo):
    n = x.shape[0]
    tps = DEC_SEQ // tm
    r = tm // SUBLANES
    nblk8 = n // SUBLANES
    in_specs = [pl.BlockSpec((tm, D), lambda t: (t, 0))]
    args = [x]
    if halo:
        in_specs += [
            pl.BlockSpec((SUBLANES, D), lambda t: (jnp.maximum(t * r - 1, 0), 0)),
            pl.BlockSpec((SUBLANES, D), lambda t: (jnp.minimum((t + 1) * r, nblk8 - 1), 0)),
        ]
        args += [x, x]
    const = dict(pipeline_mode=pl.Buffered(1))
    in_specs += [
        pl.BlockSpec((1, 1, 6 * D), mod_map),
        pl.BlockSpec((1, D), lambda t: (0, 0)),
        pl.BlockSpec((1, D), lambda t: (0, 0)),
        pl.BlockSpec((D, 2 * DFF), lambda t: (0, 0), **const),
        pl.BlockSpec((3, 2 * DFF), lambda t: (0, 0)),
        pl.BlockSpec((1, 2 * DFF), lambda t: (0, 0)),
        pl.BlockSpec((DFF, D), lambda t: (0, 0), **const),
    ]
    args += [modl, gpre.reshape(1, D), gpost.reshape(1, D), wup, cw, cb, wdn]
    return pl.pallas_call(
        functools.partial(_ffn_kernel, halo=halo, tiles_per_seq=tps),
        grid=(n // tm,),
        in_specs=in_specs,
        out_specs=pl.BlockSpec((tm, D), lambda t: (t, 0)),
        out_shape=jax.ShapeDtypeStruct((n, D), F32),
        scratch_shapes=[pltpu.VMEM((tm, DFF), BF)],
        compiler_params=_params(("arbitrary",)),
        name="ffn",
    )(*args)


def _rope_tables():
    t = jnp.arange(DEC_SEQ)
    rows = (t // GRID_W).astype(F32)
    cols = (t % GRID_W).astype(F32)
    quarter = HD // 4
    freqs = ROPE_BASE ** (-jnp.arange(quarter, dtype=F32) / quarter)
    lane = jnp.arange(LANES)
    d = lane % HD
    e = d % (HD // 2)
    is_x2 = (e >= quarter)[None, :]
    pos = jnp.where((d // (HD // 2) == 0)[None, :], rows[:, None], cols[:, None])
    ang = pos * freqs[e % quarter][None, :]
    cos = jnp.cos(ang)
    sin = jnp.sin(ang)
    return cos, jnp.where(is_x2, 0.0, -sin), jnp.where(is_x2, sin, 0.0)


def _block_diag_ones():
    r = jnp.arange(256)
    return (r[:, None] // HD == r[None, :] // HD).astype(BF)


def _cache_t(cache):
    b, n, p, h, d = cache.shape
    return cache.transpose(0, 1, 3, 4, 2).reshape(b, n, h * d, p)


def _cache_out(kt, heads):
    b, n, _, s = kt.shape
    return kt.reshape(b, n, heads, HD, s).transpose(0, 1, 4, 2, 3)


def kernel(x_prompt, x_sample, cache_na_k, cache_na_v, cache_gqa_k, cache_gqa_v, c, c_ctx,
           ada_w, ada_b, norm_mix_pre, norm_mix_post, norm_ffn_pre, norm_ffn_post,
           na_w_qkv, na_w_o, na_rpb, gqa_w_qkv, gqa_w_o, gqa_q_norm, gqa_k_norm,
           ffn_w_up, ffn_conv_w, ffn_conv_b, ffn_w_down):
    n_a = cache_na_k.shape[1]
    n_b = cache_gqa_k.shape[1]
    cond8 = jnp.concatenate([c_ctx[None], c, jnp.zeros((SUBLANES - 1 - DEC_BATCH, D), F32)], axis=0)
    mod = _adaln(cond8, ada_w, ada_b)

    tm_p, tm_s = 512, 512
    tm_fp, tm_fs = 256, 512
    map_p = lambda t: (0, 0, 0)
    map_s = lambda t: (1 + t // (DEC_SEQ // tm_s), 0, 0)
    map_fs = lambda t: (1 + t // (DEC_SEQ // tm_fs), 0, 0)

    cna_kt, cna_vt = _cache_t(cache_na_k), _cache_t(cache_na_v)
    cg_kt, cg_vt = _cache_t(cache_gqa_k), _cache_t(cache_gqa_v)
    rope_tabs = _rope_tables()
    bd = _block_diag_ones()
    ptabs = _na_bias_tables(na_rpb)
    kvg = NKV * HD

    xp = x_prompt.reshape(NP, D)
    xs = x_sample.reshape(NS, D)
    na_kv = None
    gq_kv = None
    for l in range(DEPTH):
        i = l // 2
        modl = mod[l].reshape(SUBLANES, 1, 6 * D)
        if l % 2 == 0:
            w = na_w_qkv[i]
            w_o = na_w_o[i].astype(BF)
            wqv = jnp.concatenate([w[:, :D], w[:, 2 * D:]], axis=1).astype(BF)
            wkt = w[:, D:2 * D].T.astype(BF)
            qp, vp, kt, vt = _qkv_prompt(xp, modl, norm_mix_pre[l], wqv, wkt, i, n_a, na_kv,
                                         gqa=False, tm=tm_p)
            na_kv = (kt, vt)
            qs, ks, vs = _qkv_sample(xs, modl, map_s, norm_mix_pre[l], w.astype(BF),
                                     gqa=False, tm=tm_s)
            op = _attn_prompt(qp, kt, vp, i, 1)
            os_ = _attn_na_sample(qs, ks, vs, cna_kt, cna_vt, i, ptabs)
        else:
            w = gqa_w_qkv[i]
            w_o = gqa_w_o[i].astype(BF)
            wqv = jnp.concatenate([w[:, :D], w[:, D + kvg:]], axis=1).astype(BF)
            wkt = w[:, D:D + kvg].T.astype(BF)
            qg = (jnp.tile(gqa_q_norm[i], NH) * 0.125).reshape(1, D)
            kg = jnp.tile(gqa_k_norm[i], NKV)
            qkg = jnp.concatenate([qg[0], kg]).reshape(1, D + kvg)
            kg2 = jnp.broadcast_to(kg[:, None], (kvg, SEQ))
            qp, vp, kt, vt = _qkv_prompt(xp, modl, norm_mix_pre[l], wqv, wkt, i, n_b, gq_kv,
                                         gqa=True, tm=tm_p, extra=(bd, qg, kg2))
            gq_kv = (kt, vt)
            qs, ks, vs = _qkv_sample(xs, modl, map_s, norm_mix_pre[l], w.astype(BF),
                                     gqa=True, tm=tm_s, extra=(bd, qkg) + rope_tabs)
            op = _attn_prompt(qp, kt, vp, i, NH // NKV)
            os_ = _attn_gqa_sample(qs, ks, vs, cg_kt, cg_vt, i)
        xp = _wo(op, xp, modl, map_p, norm_mix_post[l], w_o, tm_p)
        xs = _wo(os_, xs, modl, map_s, norm_mix_post[l], w_o, tm_s)

        wup = ffn_w_up[l].astype(BF)
        wdn = ffn_w_down[l].astype(BF)
        cw = ffn_conv_w[l]
        cb = ffn_conv_b[l].reshape(1, 2 * DFF)
        xp = _ffn(xp, modl, map_p, norm_ffn_pre[l], norm_ffn_post[l], wup, cw, cb, wdn,
                  tm=tm_fp, halo=False)
        xs = _ffn(xs, modl, map_fs, norm_ffn_pre[l], norm_ffn_post[l], wup, cw, cb, wdn,
                  tm=tm_fs, halo=True)

    return (xp.reshape(BATCH, SEQ, D), xs.reshape(DEC_BATCH, DEC_SEQ, D),
            _cache_out(na_kv[0], NH), _cache_out(na_kv[1], NH),
            _cache_out(gq_kv[0], NKV), _cache_out(gq_kv[1], NKV))
```

```python
import functools

import jax
import jax.numpy as jnp
from jax import lax
from jax.experimental import pallas as pl
from jax.experimental.pallas import tpu as pltpu

D = 1024
HD = 64
NH = 16
NKV = 4
DFF = 2816
DEPTH = 4
GRID_W = 64
ROWS = 32
NA_KH = 8
NA_KW = 16
ROPE_BASE = 10000.0
EPS = 1e-6
BATCH, SEQ = 32, 256
DEC_BATCH, DEC_SEQ = 2, 2048
PAST = 256
NP = BATCH * SEQ
NS = DEC_BATCH * DEC_SEQ

BF = jnp.bfloat16
F32 = jnp.float32
NEG = -1e30
LANES = 128
SUBLANES = 8
FF_CHUNK = 256
N_CHUNK = DFF // FF_CHUNK
VMEM_LIMIT = 50 * 1024 * 1024


def _params(sem):
    return pltpu.CompilerParams(dimension_semantics=sem, vmem_limit_bytes=VMEM_LIMIT)


def _rms(x, g):
    return x * lax.rsqrt(jnp.mean(x * x, axis=-1, keepdims=True) + EPS) * g


def _dot(a, b):
    return jnp.dot(a, b, preferred_element_type=F32)


def _dot_nt(a, b):
    return lax.dot_general(a, b, (((1,), (1,)), ((), ())), preferred_element_type=F32)


def _mod_kernel(cond_ref, w_ref, b_ref, o_ref):
    s = cond_ref[...]
    s = s * jax.nn.sigmoid(s)
    o_ref[0] = _dot(s.astype(BF), w_ref[0].astype(BF)) + b_ref[0]


def _adaln(cond8, ada_w, ada_b):
    tn = 1536
    return pl.pallas_call(
        _mod_kernel,
        grid=(DEPTH, 6 * D // tn),
        in_specs=[
            pl.BlockSpec((SUBLANES, D), lambda l, n: (0, 0)),
            pl.BlockSpec((1, D, tn), lambda l, n: (l, 0, n)),
            pl.BlockSpec((1, 1, tn), lambda l, n: (l, 0, n)),
        ],
        out_specs=pl.BlockSpec((1, SUBLANES, tn), lambda l, n: (l, 0, n)),
        out_shape=jax.ShapeDtypeStruct((DEPTH, SUBLANES, 6 * D), F32),
        compiler_params=_params(("arbitrary", "arbitrary")),
        name="adaln",
    )(cond8, ada_w, ada_b.reshape(DEPTH, 1, 6 * D))


def _prenorm(x_ref, mod_ref, g_ref):
    m = mod_ref[0]
    return (_rms(x_ref[...], g_ref[...]) * (1.0 + m[:, D:2 * D]) + m[:, 0:D]).astype(BF)


def _head_rms_lanes(blk, bd_ref):
    ss = _dot((blk * blk).astype(BF), bd_ref[...])
    return blk * lax.rsqrt(ss * (1.0 / HD) + EPS)


def _qkv_prompt_kernel(*refs, gqa):
    if gqa:
        (x_ref, mod_ref, g_ref, wqv_ref, wkt_ref, bd_ref, qg_ref, kg_ref,
         q_ref, v_ref, kt_ref, vt_ref) = refs
    else:
        x_ref, mod_ref, g_ref, wqv_ref, wkt_ref, q_ref, v_ref, kt_ref, vt_ref = refs
    hb = _prenorm(x_ref, mod_ref, g_ref)
    qv = _dot(hb, wqv_ref[...])
    kt = _dot_nt(wkt_ref[...], hb)
    if gqa:
        for t in range(D // 256):
            sl = slice(t * 256, (t + 1) * 256)
            q_ref[:, sl] = (_head_rms_lanes(qv[:, sl], bd_ref) * qg_ref[:, sl]).astype(q_ref.dtype)
        heads = []
        for h in range(NKV):
            blk = kt[h * HD:(h + 1) * HD, :]
            heads.append(blk * lax.rsqrt(jnp.mean(blk * blk, axis=0, keepdims=True) + EPS))
        kt = jnp.concatenate(heads, axis=0)
    else:
        q_ref[...] = (qv[:, :D] * 0.125).astype(q_ref.dtype)
    v = qv[:, D:]
    v_ref[...] = v.astype(v_ref.dtype)
    for s in range(x_ref.shape[0] // SEQ):
        sl = slice(s * SEQ, (s + 1) * SEQ)
        kts = kt[:, sl]
        if gqa:
            kts = kts * kg_ref[...]
        kt_ref[s, 0] = kts
        vt_ref[s, 0] = v[sl, :].T


def _qkv_prompt(x, modl, g, wqv, wkt, i, n_layers, prev, *, gqa, tm, extra=()):
    kvw = NKV * HD if gqa else D
    in_specs = [
        pl.BlockSpec((tm, D), lambda t: (t, 0)),
        pl.BlockSpec((1, 1, 6 * D), lambda t: (0, 0, 0)),
        pl.BlockSpec((1, D), lambda t: (0, 0)),
        pl.BlockSpec((D, D + kvw), lambda t: (0, 0)),
        pl.BlockSpec((kvw, D), lambda t: (0, 0)),
    ]
    args = [x, modl, g.reshape(1, D), wqv, wkt]
    if gqa:
        in_specs += [pl.BlockSpec((256, 256), lambda t: (0, 0)),
                     pl.BlockSpec((1, D), lambda t: (0, 0)),
                     pl.BlockSpec((kvw, SEQ), lambda t: (0, 0))]
        args += list(extra)
    n_in = len(args)
    aliases = {}
    if prev is not None:
        in_specs += [pl.BlockSpec(memory_space=pl.ANY), pl.BlockSpec(memory_space=pl.ANY)]
        args += list(prev)
        aliases = {n_in: 2, n_in + 1: 3}
    kv_spec = pl.BlockSpec((tm // SEQ, 1, kvw, SEQ), lambda t: (t, i, 0, 0))
    kv_shape = jax.ShapeDtypeStruct((BATCH, n_layers, kvw, SEQ), F32)

    def body(*refs):
        _qkv_prompt_kernel(*(refs[:n_in] + refs[len(args):]), gqa=gqa)

    return pl.pallas_call(
        body,
        grid=(NP // tm,),
        in_specs=in_specs,
        out_specs=[
            pl.BlockSpec((tm, D), lambda t: (t, 0)),
            pl.BlockSpec((tm, kvw), lambda t: (t, 0)),
            kv_spec, kv_spec,
        ],
        out_shape=[
            jax.ShapeDtypeStruct((NP, D), BF),
            jax.ShapeDtypeStruct((NP, kvw), BF),
            kv_shape, kv_shape,
        ],
        input_output_aliases=aliases,
        compiler_params=_params(("arbitrary",)),
        name="qkv_prompt_gqa" if gqa else "qkv_prompt_na",
    )(*args)


def _qkv_sample_na_kernel(x_ref, mod_ref, g_ref, wqv_ref, wkt_ref, q_ref, kt_ref, v_ref):
    hb = _prenorm(x_ref, mod_ref, g_ref)
    qv = _dot(hb, wqv_ref[...])
    q_ref[...] = (qv[:, :D] * 0.125).astype(q_ref.dtype)
    v_ref[...] = qv[:, D:].astype(v_ref.dtype)
    kt_ref[...] = _dot_nt(wkt_ref[...], hb).astype(kt_ref.dtype)


def _qkv_sample_na(x, modl, mod_map, g, wqv, wkt, *, tm):
    return pl.pallas_call(
        _qkv_sample_na_kernel,
        grid=(NS // tm,),
        in_specs=[
            pl.BlockSpec((tm, D), lambda t: (t, 0)),
            pl.BlockSpec((1, 1, 6 * D), mod_map),
            pl.BlockSpec((1, D), lambda t: (0, 0)),
            pl.BlockSpec((D, 2 * D), lambda t: (0, 0)),
            pl.BlockSpec((D, D), lambda t: (0, 0)),
        ],
        out_specs=[
            pl.BlockSpec((tm, D), lambda t: (t, 0)),
            pl.BlockSpec((D, tm), lambda t: (0, t)),
            pl.BlockSpec((tm, D), lambda t: (t, 0)),
        ],
        out_shape=[
            jax.ShapeDtypeStruct((NS, D), BF),
            jax.ShapeDtypeStruct((D, NS), BF),
            jax.ShapeDtypeStruct((NS, D), BF),
        ],
        compiler_params=_params(("arbitrary",)),
        name="qkv_sample_na",
    )(x, modl, g.reshape(1, D), wqv, wkt)


def _qkv_sample_gqa_kernel(x_ref, mod_ref, g_ref, w_ref, bd_ref, qkg_ref, cos_ref, s1_ref, s2_ref,
                           q_ref, kt_ref, v_ref):
    qkv = _dot(_prenorm(x_ref, mod_ref, g_ref), w_ref[...])
    nqk = D + NKV * HD
    for t in range(nqk // LANES):
        if t % 2 == 0:
            nrm = _head_rms_lanes(qkv[:, (t // 2) * 256:(t // 2 + 1) * 256], bd_ref)
        xt = nrm[:, (t % 2) * LANES:(t % 2 + 1) * LANES] * qkg_ref[:, t * LANES:(t + 1) * LANES]
        xt = (xt * cos_ref[...]
              + pltpu.roll(xt, LANES - 16, 1) * s1_ref[...]
              + pltpu.roll(xt, 16, 1) * s2_ref[...])
        if t < D // LANES:
            q_ref[:, t * LANES:(t + 1) * LANES] = xt.astype(q_ref.dtype)
        else:
            tk = t - D // LANES
            kt_ref[tk * LANES:(tk + 1) * LANES, :] = xt.T.astype(kt_ref.dtype)
    v_ref[...] = qkv[:, nqk:].astype(v_ref.dtype)


def _qkv_sample_gqa(x, modl, mod_map, g, w, bd, qkg, rope_tabs, *, tm):
    nq = w.shape[1]
    kvw = NKV * HD
    tps = DEC_SEQ // tm
    in_specs = [
        pl.BlockSpec((tm, D), lambda t: (t, 0)),
        pl.BlockSpec((1, 1, 6 * D), mod_map),
        pl.BlockSpec((1, D), lambda t: (0, 0)),
        pl.BlockSpec((D, nq), lambda t: (0, 0)),
        pl.BlockSpec((256, 256), lambda t: (0, 0)),
        pl.BlockSpec((1, D + kvw), lambda t: (0, 0)),
    ]
    in_specs += [pl.BlockSpec((tm, LANES), lambda t: (t % tps, 0)) for _ in rope_tabs]
    return pl.pallas_call(
        _qkv_sample_gqa_kernel,
        grid=(NS // tm,),
        in_specs=in_specs,
        out_specs=[
            pl.BlockSpec((tm, D), lambda t: (t, 0)),
            pl.BlockSpec((kvw, tm), lambda t: (0, t)),
            pl.BlockSpec((tm, kvw), lambda t: (t, 0)),
        ],
        out_shape=[
            jax.ShapeDtypeStruct((NS, D), BF),
            jax.ShapeDtypeStruct((kvw, NS), BF),
            jax.ShapeDtypeStruct((NS, kvw), BF),
        ],
        compiler_params=_params(("arbitrary",)),
        name="qkv_sample_gqa",
    )(x, modl, g.reshape(1, D), w, bd, qkg, *rope_tabs)


def _lane_lo(rows):
    return lax.broadcasted_iota(jnp.int32, (rows, LANES), 1) < HD


def _pipelined(n, first, second):
    nxt = first(0)
    for i in range(n):
        cur = nxt
        if i + 1 < n:
            nxt = first(i + 1)
        second(i, cur)


def _attn_grouped(q_ref, o_ref, group, stack, key_tiles, value_tiles, *, ones_sum):
    tq = q_ref.shape[0]
    lo = _lane_lo(tq)
    lo_s = _lane_lo(stack * tq)
    pending = {}

    def scores(u):
        g = u * stack // group
        parts = []
        for h in range(u * stack, (u + 1) * stack):
            q2 = q_ref[:, (h // 2) * LANES:(h // 2 + 1) * LANES]
            if h % 2 != g % 2:
                q2 = pltpu.roll(q2.astype(F32), HD, 1).astype(BF)
            parts.append(q2)
        qs = parts[0] if stack == 1 else jnp.concatenate(parts, axis=0)
        mine = lo_s if g % 2 == 0 else jnp.logical_not(lo_s)
        qsel = jnp.where(mine, qs, jnp.zeros_like(qs))
        return [_dot(qsel, kt) for kt in key_tiles(g)]

    def finish(u, ss):
        g = u * stack // group
        m = ss[0].max(axis=-1, keepdims=True)
        for s in ss[1:]:
            m = jnp.maximum(m, s.max(axis=-1, keepdims=True))
        es = [jnp.exp(s - m) for s in ss]
        pv = None
        for e, v in zip(es, value_tiles(g)):
            if ones_sum:
                mine = _lane_lo(v.shape[0])
                if g % 2:
                    mine = jnp.logical_not(mine)
                v = jnp.where(mine, v, jnp.ones_like(v))
            d = _dot(e.astype(BF), v)
            pv = d if pv is None else pv + d
        if ones_sum:
            pv = pv / pltpu.roll(pv, HD, 1)
        else:
            l = es[0].sum(axis=-1, keepdims=True)
            for e in es[1:]:
                l = l + e.sum(axis=-1, keepdims=True)
            pv = pv / l
        for j in range(stack):
            h = u * stack + j
            pj = pv[j * tq:(j + 1) * tq]
            if h % 2 != g % 2:
                pj = pltpu.roll(pj, HD, 1)
            if h % 2 == 0:
                pending[h // 2] = pj
            else:
                sl = slice((h // 2) * LANES, (h // 2 + 1) * LANES)
                o_ref[:, sl] = jnp.where(lo, pending.pop(h // 2), pj).astype(o_ref.dtype)

    _pipelined(NH // stack, scores, finish)


def _attn_prompt_kernel(q_ref, kt_ref, v_ref, o_ref, *, group):
    def tile(g):
        return slice((g // 2) * LANES, (g // 2 + 1) * LANES)
    _attn_grouped(q_ref, o_ref, group, group,
                  lambda g: [kt_ref[0, 0, tile(g), :].astype(BF)],
                  lambda g: [v_ref[:, tile(g)]], ones_sum=False)


def _attn_prompt(q, kt_all, v, i, group):
    kvw = v.shape[1]
    return pl.pallas_call(
        functools.partial(_attn_prompt_kernel, group=group),
        grid=(BATCH,),
        in_specs=[
            pl.BlockSpec((SEQ, D), lambda b: (b, 0)),
            pl.BlockSpec((1, 1, kvw, SEQ), lambda b: (b, i, 0, 0)),
            pl.BlockSpec((SEQ, kvw), lambda b: (b, 0)),
        ],
        out_specs=pl.BlockSpec((SEQ, D), lambda b: (b, 0)),
        out_shape=jax.ShapeDtypeStruct((NP, D), BF),
        compiler_params=_params(("arbitrary",)),
        name="attn_prompt",
    )(q, kt_all, v)


def _attn_gqa_sample_kernel(q_ref, kt_ref, v_ref, kct_ref, vct_ref, o_ref):
    vc = vct_ref[0, 0].T.astype(BF)

    def tile(g):
        return slice((g // 2) * LANES, (g // 2 + 1) * LANES)
    _attn_grouped(q_ref, o_ref, NH // NKV, NH // NKV,
                  lambda g: [kt_ref[tile(g), :], kct_ref[0, 0, tile(g), :].astype(BF)],
                  lambda g: [v_ref[:, tile(g)], vc[:, tile(g)]], ones_sum=True)


GQA_TQ = 256


def _attn_gqa_sample(q, kt, v, cache_kt, cache_vt, i):
    nq = DEC_SEQ // GQA_TQ
    kvw = NKV * HD
    return pl.pallas_call(
        _attn_gqa_sample_kernel,
        grid=(DEC_BATCH, nq),
        in_specs=[
            pl.BlockSpec((GQA_TQ, D), lambda b, t: (b * nq + t, 0)),
            pl.BlockSpec((kvw, DEC_SEQ), lambda b, t: (0, b)),
            pl.BlockSpec((DEC_SEQ, kvw), lambda b, t: (b, 0)),
            pl.BlockSpec((1, 1, kvw, PAST), lambda b, t: (b, i, 0, 0)),
            pl.BlockSpec((1, 1, kvw, PAST), lambda b, t: (b, i, 0, 0)),
        ],
        out_specs=pl.BlockSpec((GQA_TQ, D), lambda b, t: (b * nq + t, 0)),
        out_shape=jax.ShapeDtypeStruct((NS, D), BF),
        compiler_params=_params(("arbitrary", "arbitrary")),
        name="attn_gqa_sample",
    )(q, kt, v, cache_kt, cache_vt)


NA_QROWS = 4
NA_KBLK = NA_QROWS * GRID_W
NA_NKB = 3
NA_PAIRS = NA_QROWS // 2


def _attn_na_sample_kernel(*refs):
    q_ref = refs[0]
    kt_refs = refs[1:1 + NA_NKB]
    v_refs = refs[1 + NA_NKB:1 + 2 * NA_NKB]
    kct_ref, vct_ref, p_ref, o_ref = refs[1 + 2 * NA_NKB:]
    rb = pl.program_id(1)
    kb0 = jnp.clip(rb - 1, 0, ROWS // NA_QROWS - NA_NKB)
    tq = q_ref.shape[0]
    lo = _lane_lo(tq)
    hi = jnp.logical_not(lo)
    lo_w = _lane_lo(GRID_W)
    vc = vct_ref[0, 0].T.astype(BF)
    ones = jnp.ones((tq, LANES), BF)

    tiles = []
    for i in range(NA_QROWS):
        qr = rb * NA_QROWS + i
        rs = jnp.clip(qr - NA_KH // 2, 0, ROWS - NA_KH)
        for j in range(NA_NKB):
            for jj in range(NA_PAIRS):
                kra = (kb0 + j) * NA_QROWS + 2 * jj
                krb = kra + 1
                idx = jnp.clip(kra - qr + NA_KH, 0, 2 * NA_KH - 1)
                va = jnp.logical_and(kra >= rs, kra < rs + NA_KH).astype(jnp.int32)
                vb = jnp.logical_and(krb >= rs, krb < rs + NA_KH).astype(jnp.int32)
                tiles.append((idx, jnp.where(lo_w, va, vb) > 0))

    def scores(h):
        rows = slice((h // 2) * LANES, (h // 2 + 1) * LANES)
        q2 = q_ref[:, rows]
        qsel = jnp.where(lo if h % 2 == 0 else hi, q2, jnp.zeros_like(q2))
        s_lat = [_dot(qsel, r[rows, :]) for r in kt_refs]
        out_rows = []
        for i in range(NA_QROWS):
            rsl = slice(i * GRID_W, (i + 1) * GRID_W)
            row = []
            for j in range(NA_NKB):
                for jj in range(NA_PAIRS):
                    idx, valid = tiles[(i * NA_NKB + j) * NA_PAIRS + jj]
                    t = s_lat[j][rsl, jj * LANES:(jj + 1) * LANES] + p_ref[0, h, idx]
                    row.append(jnp.where(valid, t, NEG))
            out_rows.append(jnp.concatenate(row, axis=1))
        return jnp.concatenate(out_rows, axis=0), _dot(qsel, kct_ref[0, 0, rows, :].astype(BF))

    def weighted_values(h, s, s_ctx):
        rows = slice((h // 2) * LANES, (h // 2 + 1) * LANES)
        mine = lo if h % 2 == 0 else hi
        m = jnp.maximum(s.max(axis=-1, keepdims=True), s_ctx.max(axis=-1, keepdims=True))
        e = jnp.exp(s - m).astype(BF)
        e_ctx = jnp.exp(s_ctx - m).astype(BF)
        pv = _dot(e_ctx, jnp.where(mine, vc[:, rows], ones))
        for j in range(NA_NKB):
            vj = jnp.where(mine, v_refs[j][:, rows], ones)
            pv = pv + _dot(e[:, j * NA_KBLK:(j + 1) * NA_KBLK], vj)
        return pv / pltpu.roll(pv, HD, 1)

    nxt = scores(0)
    o2 = None
    for h in range(NH):
        cur = nxt
        if h + 1 < NH:
            nxt = scores(h + 1)
        pv = weighted_values(h, *cur)
        if h % 2 == 0:
            o2 = pv
        else:
            o_ref[:, (h // 2) * LANES:(h // 2 + 1) * LANES] = jnp.where(lo, o2, pv).astype(o_ref.dtype)


def _attn_na_sample(q, kt, v, cache_kt, cache_vt, i, ptab):
    nrb = ROWS // NA_QROWS

    def kb(b, r, j):
        return b * nrb + jnp.clip(r - 1, 0, nrb - NA_NKB) + j

    in_specs = [pl.BlockSpec((NA_KBLK, D), lambda b, r: (b * nrb + r, 0))]
    in_specs += [pl.BlockSpec((D, NA_KBLK), functools.partial(lambda b, r, j: (0, kb(b, r, j)), j=j))
                 for j in range(NA_NKB)]
    in_specs += [pl.BlockSpec((NA_KBLK, D), functools.partial(lambda b, r, j: (kb(b, r, j), 0), j=j))
                 for j in range(NA_NKB)]
    in_specs += [
        pl.BlockSpec((1, 1, D, PAST), lambda b, r: (b, i, 0, 0)),
        pl.BlockSpec((1, 1, D, PAST), lambda b, r: (b, i, 0, 0)),
        pl.BlockSpec((1, NH, 2 * NA_KH, GRID_W, LANES), lambda b, r: (i, 0, 0, 0, 0),
                     pipeline_mode=pl.Buffered(1)),
    ]
    return pl.pallas_call(
        _attn_na_sample_kernel,
        grid=(DEC_BATCH, nrb),
        in_specs=in_specs,
        out_specs=pl.BlockSpec((NA_KBLK, D), lambda b, r: (b * nrb + r, 0)),
        out_shape=jax.ShapeDtypeStruct((NS, D), BF),
        compiler_params=_params(("arbitrary", "arbitrary")),
        name="attn_na_sample",
    )(q, *([kt] * NA_NKB), *([v] * NA_NKB), cache_kt, cache_vt, ptab)


N_DC = 2 * NA_KW - 1
N_DR = 2 * NA_KH - 1
DR_GROUP = 5


def _na_bias_kernel(rpb_ref, o_ref):
    base = (pl.program_id(0) * NH + pl.program_id(1)) * (N_DR * N_DC)
    qc = lax.broadcasted_iota(jnp.int32, (GRID_W, LANES), 0)
    lane = lax.broadcasted_iota(jnp.int32, (GRID_W, LANES), 1)
    kc = lane % GRID_W
    diff = kc - qc + (NA_KW - 1)
    c_start = jnp.clip(qc - NA_KW // 2, 0, GRID_W - NA_KW)
    ok = jnp.logical_and(kc >= c_start, kc < c_start + NA_KW)
    lo = lane < GRID_W
    neg = jnp.full((GRID_W, LANES), NEG, F32)
    prev = neg
    for g0 in range(0, N_DR, DR_GROUP):
        accs = [jnp.zeros((GRID_W, LANES), F32) for _ in range(DR_GROUP)]
        for p in range(N_DC):
            hit = diff == p
            for k in range(DR_GROUP):
                accs[k] = jnp.where(hit, rpb_ref[base + (g0 + k) * N_DC + p], accs[k])
        for k in range(DR_GROUP):
            cur = jnp.where(ok, accs[k], neg)
            o_ref[0, 0, g0 + k] = jnp.where(lo, prev, cur)
            prev = cur
    o_ref[0, 0, N_DR] = jnp.where(lo, prev, neg)


def _na_bias_tables(rpb):
    n_a = rpb.shape[0]
    return pl.pallas_call(
        _na_bias_kernel,
        grid=(n_a, NH),
        in_specs=[pl.BlockSpec(memory_space=pltpu.SMEM)],
        out_specs=pl.BlockSpec((1, 1, N_DR + 1, GRID_W, LANES), lambda a, h: (a, h, 0, 0, 0)),
        out_shape=jax.ShapeDtypeStruct((n_a, NH, N_DR + 1, GRID_W, LANES), F32),
        compiler_params=_params(("arbitrary", "arbitrary")),
        name="na_bias",
    )(rpb.reshape(-1))


def _wo_kernel(o_ref, x_ref, mod_ref, g_ref, w_ref, out_ref):
    y = _dot(o_ref[...], w_ref[...])
    gate = mod_ref[0][:, 2 * D:3 * D]
    out_ref[...] = x_ref[...] + gate * _rms(y, g_ref[...])


def _wo(o, x, modl, mod_map, g, w, tm):
    n = x.shape[0]
    return pl.pallas_call(
        _wo_kernel,
        grid=(n // tm,),
        in_specs=[
            pl.BlockSpec((tm, D), lambda t: (t, 0)),
            pl.BlockSpec((tm, D), lambda t: (t, 0)),
            pl.BlockSpec((1, 1, 6 * D), mod_map),
            pl.BlockSpec((1, D), lambda t: (0, 0)),
            pl.BlockSpec((D, D), lambda t: (0, 0)),
        ],
        out_specs=pl.BlockSpec((tm, D), lambda t: (t, 0)),
        out_shape=jax.ShapeDtypeStruct((n, D), F32),
        compiler_params=_params(("arbitrary",)),
        name="wo",
    )(o, x, modl, g.reshape(1, D), w)


def _ffn_kernel(*refs, halo, tiles_per_seq):
    if halo:
        (x_ref, xp_ref, xn_ref, mod_ref, gpre_ref, gpost_ref, wup_ref, cw_ref, cb_ref,
         wdn_ref, out_ref, act_scr) = refs
    else:
        (x_ref, mod_ref, gpre_ref, gpost_ref, wup_ref, cw_ref, cb_ref,
         wdn_ref, out_ref, act_scr) = refs
    tm = x_ref.shape[0]
    m = mod_ref[0]
    sh = m[:, 3 * D:4 * D]
    sc = m[:, 4 * D:5 * D]
    gate = m[:, 5 * D:6 * D]

    def pre(xv):
        return _rms(xv, gpre_ref[...]) * (1.0 + sc) + sh

    x = x_ref[...]
    h = pre(x)
    zeros = jnp.zeros((SUBLANES, D), F32)
    if halo:
        t = pl.program_id(0) % tiles_per_seq
        hp = jnp.where(t == 0, zeros, pre(xp_ref[...]))
        hn = jnp.where(t == tiles_per_seq - 1, zeros, pre(xn_ref[...]))
    else:
        hp = zeros
        hn = zeros
    hext = jnp.concatenate([hp, h, hn], axis=0).astype(BF)
    mext = tm + 2 * SUBLANES

    def conv(u, c0):
        cols = slice(c0, c0 + FF_CHUNK)
        prev = pltpu.roll(u, 1, 0)[SUBLANES:SUBLANES + tm]
        nxt = pltpu.roll(u, mext - 1, 0)[SUBLANES:SUBLANES + tm]
        cur = u[SUBLANES:SUBLANES + tm]
        return (prev * cw_ref[0:1, cols] + cur * cw_ref[1:2, cols] + nxt * cw_ref[2:3, cols]
                + cb_ref[:, cols])

    for j in range(N_CHUNK):
        ca = j * FF_CHUNK
        cg = DFF + j * FF_CHUNK
        a = conv(_dot(hext, wup_ref[:, ca:ca + FF_CHUNK]), ca)
        gt = conv(_dot(hext, wup_ref[:, cg:cg + FF_CHUNK]), cg)
        act_scr[:, ca:ca + FF_CHUNK] = (a * jax.nn.sigmoid(a) * gt).astype(BF)
    y = _dot(act_scr[...], wdn_ref[...])
    out_ref[...] = x + gate * _rms(y, gpost_ref[...])


def _ffn(x, modl, mod_map, gpre, gpost, wup, cw, cb, wdn, *, tm, halo):
    n = x.shape[0]
    tps = DEC_SEQ // tm
    r = tm // SUBLANES
    nblk8 = n // SUBLANES
    in_specs = [pl.BlockSpec((tm, D), lambda t: (t, 0))]
    args = [x]
    if halo:
        in_specs += [
            pl.BlockSpec((SUBLANES, D), lambda t: (jnp.maximum(t * r - 1, 0), 0)),
            pl.BlockSpec((SUBLANES, D), lambda t: (jnp.minimum((t + 1) * r, nblk8 - 1), 0)),
        ]
        args += [x, x]
    const = dict(pipeline_mode=pl.Buffered(1))
    in_specs += [
        pl.BlockSpec((1, 1, 6 * D), mod_map),
        pl.BlockSpec((1, D), lambda t: (0, 0)),
        pl.BlockSpec((1, D), lambda t: (0, 0)),
        pl.BlockSpec((D, 2 * DFF), lambda t: (0, 0), **const),
        pl.BlockSpec((3, 2 * DFF), lambda t: (0, 0)),
        pl.BlockSpec((1, 2 * DFF), lambda t: (0, 0)),
        pl.BlockSpec((DFF, D), lambda t: (0, 0), **const),
    ]
    args += [modl, gpre.reshape(1, D), gpost.reshape(1, D), wup, cw, cb, wdn]
    return pl.pallas_call(
        functools.partial(_ffn_kernel, halo=halo, tiles_per_seq=tps),
        grid=(n // tm,),
        in_specs=in_specs,
        out_specs=pl.BlockSpec((tm, D), lambda t: (t, 0)),
        out_shape=jax.ShapeDtypeStruct((n, D), F32),
        scratch_shapes=[pltpu.VMEM((tm, DFF), BF)],
        compiler_params=_params(("arbitrary",)),
        name="ffn",
    )(*args)


def _rope_tables():
    t = jnp.arange(DEC_SEQ)
    rows = (t // GRID_W).astype(F32)
    cols = (t % GRID_W).astype(F32)
    quarter = HD // 4
    freqs = ROPE_BASE ** (-jnp.arange(quarter, dtype=F32) / quarter)
    lane = jnp.arange(LANES)
    d = lane % HD
    e = d % (HD // 2)
    is_x2 = (e >= quarter)[None, :]
    pos = jnp.where((d // (HD // 2) == 0)[None, :], rows[:, None], cols[:, None])
    ang = pos * freqs[e % quarter][None, :]
    cos = jnp.cos(ang)
    sin = jnp.sin(ang)
    return cos, jnp.where(is_x2, 0.0, -sin), jnp.where(is_x2, sin, 0.0)


def _block_diag_ones():
    r = jnp.arange(256)
    return (r[:, None] // HD == r[None, :] // HD).astype(BF)


def _cache_t(cache):
    b, n, p, h, d = cache.shape
    return cache.transpose(0, 1, 3, 4, 2).reshape(b, n, h * d, p)


def _cache_out(kt, heads):
    b, n, _, s = kt.shape
    return kt.reshape(b, n, heads, HD, s).transpose(0, 1, 4, 2, 3)


def kernel(x_prompt, x_sample, cache_na_k, cache_na_v, cache_gqa_k, cache_gqa_v, c, c_ctx,
           ada_w, ada_b, norm_mix_pre, norm_mix_post, norm_ffn_pre, norm_ffn_post,
           na_w_qkv, na_w_o, na_rpb, gqa_w_qkv, gqa_w_o, gqa_q_norm, gqa_k_norm,
           ffn_w_up, ffn_conv_w, ffn_conv_b, ffn_w_down):
    n_a = cache_na_k.shape[1]
    n_b = cache_gqa_k.shape[1]
    cond8 = jnp.concatenate([c_ctx[None], c, jnp.zeros((SUBLANES - 1 - DEC_BATCH, D), F32)], axis=0)
    mod = _adaln(cond8, ada_w, ada_b)

    tm_p, tm_s = 512, 512
    tm_fp, tm_fs = 256, 512
    map_p = lambda t: (0, 0, 0)
    map_s = lambda t: (1 + t // (DEC_SEQ // tm_s), 0, 0)
    map_fs = lambda t: (1 + t // (DEC_SEQ // tm_fs), 0, 0)

    cna_kt, cna_vt = _cache_t(cache_na_k), _cache_t(cache_na_v)
    cg_kt, cg_vt = _cache_t(cache_gqa_k), _cache_t(cache_gqa_v)
    rope_tabs = _rope_tables()
    bd = _block_diag_ones()
    ptabs = _na_bias_tables(na_rpb)
    kvg = NKV * HD

    xp = x_prompt.reshape(NP, D)
    xs = x_sample.reshape(NS, D)
    na_kv = None
    gq_kv = None
    for l in range(DEPTH):
        i = l // 2
        modl = mod[l].reshape(SUBLANES, 1, 6 * D)
        if l % 2 == 0:
            w = na_w_qkv[i]
            w_o = na_w_o[i].astype(BF)
            wqv = jnp.concatenate([w[:, :D], w[:, 2 * D:]], axis=1).astype(BF)
            wkt = w[:, D:2 * D].T.astype(BF)
            qp, vp, kt, vt = _qkv_prompt(xp, modl, norm_mix_pre[l], wqv, wkt, i, n_a, na_kv,
                                         gqa=False, tm=tm_p)
            na_kv = (kt, vt)
            qs, kts, vs = _qkv_sample_na(xs, modl, map_s, norm_mix_pre[l], wqv, wkt, tm=tm_s)
            op = _attn_prompt(qp, kt, vp, i, 1)
            os_ = _attn_na_sample(qs, kts, vs, cna_kt, cna_vt, i, ptabs)
        else:
            w = gqa_w_qkv[i]
            w_o = gqa_w_o[i].astype(BF)
            wqv = jnp.concatenate([w[:, :D], w[:, D + kvg:]], axis=1).astype(BF)
            wkt = w[:, D:D + kvg].T.astype(BF)
            qg = (jnp.tile(gqa_q_norm[i], NH) * 0.125).reshape(1, D)
            kg = jnp.tile(gqa_k_norm[i], NKV)
            qkg = jnp.concatenate([qg[0], kg]).reshape(1, D + kvg)
            kg2 = jnp.broadcast_to(kg[:, None], (kvg, SEQ))
            qp, vp, kt, vt = _qkv_prompt(xp, modl, norm_mix_pre[l], wqv, wkt, i, n_b, gq_kv,
                                         gqa=True, tm=tm_p, extra=(bd, qg, kg2))
            gq_kv = (kt, vt)
            qs, kts, vs = _qkv_sample_gqa(xs, modl, map_s, norm_mix_pre[l], w.astype(BF), bd, qkg,
                                          rope_tabs, tm=tm_s)
            op = _attn_prompt(qp, kt, vp, i, NH // NKV)
            os_ = _attn_gqa_sample(qs, kts, vs, cg_kt, cg_vt, i)
        xp = _wo(op, xp, modl, map_p, norm_mix_post[l], w_o, tm_p)
        xs = _wo(os_, xs, modl, map_s, norm_mix_post[l], w_o, tm_s)

        wup = ffn_w_up[l].astype(BF)
        wdn = ffn_w_down[l].astype(BF)
        cw = ffn_conv_w[l]
        cb = ffn_conv_b[l].reshape(1, 2 * DFF)
        xp = _ffn(xp, modl, map_p, norm_ffn_pre[l], norm_ffn_post[l], wup, cw, cb, wdn,
                  tm=tm_fp, halo=False)
        xs = _ffn(xs, modl, map_fs, norm_ffn_pre[l], norm_ffn_post[l], wup, cw, cb, wdn,
                  tm=tm_fs, halo=True)

    return (xp.reshape(BATCH, SEQ, D), xs.reshape(DEC_BATCH, DEC_SEQ, D),
            _cache_out(na_kv[0], NH), _cache_out(na_kv[1], NH),
            _cache_out(gq_kv[0], NKV), _cache_out(gq_kv[1], NKV))
```

```python
import functools

import jax
import jax.numpy as jnp
from jax import lax
from jax.experimental import pallas as pl
from jax.experimental.pallas import tpu as pltpu

D = 1024
HD = 64
NH = 16
NKV = 4
DFF = 2816
DEPTH = 4
GRID_W = 64
ROWS = 32
NA_KH = 8
NA_KW = 16
ROPE_BASE = 10000.0
EPS = 1e-6
BATCH, SEQ = 32, 256
DEC_BATCH, DEC_SEQ = 2, 2048
PAST = 256
NP = BATCH * SEQ
NS = DEC_BATCH * DEC_SEQ

BF = jnp.bfloat16
F32 = jnp.float32
NEG = -1e30
LANES = 128
SUBLANES = 8
FF_CHUNK = 256
N_CHUNK = DFF // FF_CHUNK
VMEM_LIMIT = 50 * 1024 * 1024


def _params(sem):
    return pltpu.CompilerParams(dimension_semantics=sem, vmem_limit_bytes=VMEM_LIMIT)


def _rms(x, g):
    return x * lax.rsqrt(jnp.mean(x * x, axis=-1, keepdims=True) + EPS) * g


def _dot(a, b):
    return jnp.dot(a, b, preferred_element_type=F32)


def _dot_nt(a, b):
    return lax.dot_general(a, b, (((1,), (1,)), ((), ())), preferred_element_type=F32)


def _mod_kernel(cond_ref, w_ref, b_ref, o_ref):
    s = cond_ref[...]
    s = s * jax.nn.sigmoid(s)
    o_ref[0] = _dot(s.astype(BF), w_ref[0].astype(BF)) + b_ref[0]


def _adaln(cond8, ada_w, ada_b):
    tn = 1536
    return pl.pallas_call(
        _mod_kernel,
        grid=(DEPTH, 6 * D // tn),
        in_specs=[
            pl.BlockSpec((SUBLANES, D), lambda l, n: (0, 0)),
            pl.BlockSpec((1, D, tn), lambda l, n: (l, 0, n)),
            pl.BlockSpec((1, 1, tn), lambda l, n: (l, 0, n)),
        ],
        out_specs=pl.BlockSpec((1, SUBLANES, tn), lambda l, n: (l, 0, n)),
        out_shape=jax.ShapeDtypeStruct((DEPTH, SUBLANES, 6 * D), F32),
        compiler_params=_params(("arbitrary", "arbitrary")),
        name="adaln",
    )(cond8, ada_w, ada_b.reshape(DEPTH, 1, 6 * D))


def _prenorm(x_ref, mod_ref, g_ref):
    m = mod_ref[0]
    return (_rms(x_ref[...], g_ref[...]) * (1.0 + m[:, D:2 * D]) + m[:, 0:D]).astype(BF)


def _head_rms_lanes(blk, bd_ref):
    ss = _dot((blk * blk).astype(BF), bd_ref[...])
    return blk * lax.rsqrt(ss * (1.0 / HD) + EPS)


def _qkv_prompt_kernel(*refs, gqa):
    if gqa:
        (x_ref, mod_ref, g_ref, wqv_ref, wkt_ref, bd_ref, qg_ref, kg_ref,
         q_ref, v_ref, kt_ref, vt_ref) = refs
    else:
        x_ref, mod_ref, g_ref, wqv_ref, wkt_ref, q_ref, v_ref, kt_ref, vt_ref = refs
    hb = _prenorm(x_ref, mod_ref, g_ref)
    qv = _dot(hb, wqv_ref[...])
    kt = _dot_nt(wkt_ref[...], hb)
    if gqa:
        for t in range(D // 256):
            sl = slice(t * 256, (t + 1) * 256)
            q_ref[:, sl] = (_head_rms_lanes(qv[:, sl], bd_ref) * qg_ref[:, sl]).astype(q_ref.dtype)
        heads = []
        for h in range(NKV):
            blk = kt[h * HD:(h + 1) * HD, :]
            heads.append(blk * lax.rsqrt(jnp.mean(blk * blk, axis=0, keepdims=True) + EPS))
        kt = jnp.concatenate(heads, axis=0)
    else:
        q_ref[...] = (qv[:, :D] * 0.125).astype(q_ref.dtype)
    v = qv[:, D:]
    v_ref[...] = v.astype(v_ref.dtype)
    for s in range(x_ref.shape[0] // SEQ):
        sl = slice(s * SEQ, (s + 1) * SEQ)
        kts = kt[:, sl]
        if gqa:
            kts = kts * kg_ref[...]
        kt_ref[s, 0] = kts
        vt_ref[s, 0] = v[sl, :].T


def _qkv_prompt(x, modl, g, wqv, wkt, i, n_layers, prev, *, gqa, tm, extra=()):
    kvw = NKV * HD if gqa else D
    in_specs = [
        pl.BlockSpec((tm, D), lambda t: (t, 0)),
        pl.BlockSpec((1, 1, 6 * D), lambda t: (0, 0, 0)),
        pl.BlockSpec((1, D), lambda t: (0, 0)),
        pl.BlockSpec((D, D + kvw), lambda t: (0, 0)),
        pl.BlockSpec((kvw, D), lambda t: (0, 0)),
    ]
    args = [x, modl, g.reshape(1, D), wqv, wkt]
    if gqa:
        in_specs += [pl.BlockSpec((256, 256), lambda t: (0, 0)),
                     pl.BlockSpec((1, D), lambda t: (0, 0)),
                     pl.BlockSpec((kvw, SEQ), lambda t: (0, 0))]
        args += list(extra)
    n_in = len(args)
    aliases = {}
    if prev is not None:
        in_specs += [pl.BlockSpec(memory_space=pl.ANY), pl.BlockSpec(memory_space=pl.ANY)]
        args += list(prev)
        aliases = {n_in: 2, n_in + 1: 3}
    kv_spec = pl.BlockSpec((tm // SEQ, 1, kvw, SEQ), lambda t: (t, i, 0, 0))
    kv_shape = jax.ShapeDtypeStruct((BATCH, n_layers, kvw, SEQ), F32)

    def body(*refs):
        _qkv_prompt_kernel(*(refs[:n_in] + refs[len(args):]), gqa=gqa)

    return pl.pallas_call(
        body,
        grid=(NP // tm,),
        in_specs=in_specs,
        out_specs=[
            pl.BlockSpec((tm, D), lambda t: (t, 0)),
            pl.BlockSpec((tm, kvw), lambda t: (t, 0)),
            kv_spec, kv_spec,
        ],
        out_shape=[
            jax.ShapeDtypeStruct((NP, D), BF),
            jax.ShapeDtypeStruct((NP, kvw), BF),
            kv_shape, kv_shape,
        ],
        input_output_aliases=aliases,
        compiler_params=_params(("arbitrary",)),
        name="qkv_prompt_gqa" if gqa else "qkv_prompt_na",
    )(*args)


def _qkv_sample_na_kernel(x_ref, mod_ref, g_ref, wqv_ref, wkt_ref, q_ref, kt_ref, v_ref):
    hb = _prenorm(x_ref, mod_ref, g_ref)
    qv = _dot(hb, wqv_ref[...])
    q_ref[...] = (qv[:, :D] * 0.125).astype(q_ref.dtype)
    v_ref[...] = qv[:, D:].astype(v_ref.dtype)
    kt_ref[...] = _dot_nt(wkt_ref[...], hb).astype(kt_ref.dtype)


def _qkv_sample_na(x, modl, mod_map, g, wqv, wkt, *, tm):
    return pl.pallas_call(
        _qkv_sample_na_kernel,
        grid=(NS // tm,),
        in_specs=[
            pl.BlockSpec((tm, D), lambda t: (t, 0)),
            pl.BlockSpec((1, 1, 6 * D), mod_map),
            pl.BlockSpec((1, D), lambda t: (0, 0)),
            pl.BlockSpec((D, 2 * D), lambda t: (0, 0)),
            pl.BlockSpec((D, D), lambda t: (0, 0)),
        ],
        out_specs=[
            pl.BlockSpec((tm, D), lambda t: (t, 0)),
            pl.BlockSpec((D, tm), lambda t: (0, t)),
            pl.BlockSpec((tm, D), lambda t: (t, 0)),
        ],
        out_shape=[
            jax.ShapeDtypeStruct((NS, D), BF),
            jax.ShapeDtypeStruct((D, NS), BF),
            jax.ShapeDtypeStruct((NS, D), BF),
        ],
        compiler_params=_params(("arbitrary",)),
        name="qkv_sample_na",
    )(x, modl, g.reshape(1, D), wqv, wkt)


def _qkv_sample_gqa_kernel(x_ref, mod_ref, g_ref, w_ref, bd_ref, qkg_ref, cos_ref, s1_ref, s2_ref,
                           q_ref, kt_ref, v_ref):
    qkv = _dot(_prenorm(x_ref, mod_ref, g_ref), w_ref[...])
    nqk = D + NKV * HD
    for t in range(nqk // LANES):
        if t % 2 == 0:
            nrm = _head_rms_lanes(qkv[:, (t // 2) * 256:(t // 2 + 1) * 256], bd_ref)
        xt = nrm[:, (t % 2) * LANES:(t % 2 + 1) * LANES] * qkg_ref[:, t * LANES:(t + 1) * LANES]
        xt = (xt * cos_ref[...]
              + pltpu.roll(xt, LANES - 16, 1) * s1_ref[...]
              + pltpu.roll(xt, 16, 1) * s2_ref[...])
        if t < D // LANES:
            q_ref[:, t * LANES:(t + 1) * LANES] = xt.astype(q_ref.dtype)
        else:
            tk = t - D // LANES
            kt_ref[tk * LANES:(tk + 1) * LANES, :] = xt.T.astype(kt_ref.dtype)
    v_ref[...] = qkv[:, nqk:].astype(v_ref.dtype)


def _qkv_sample_gqa(x, modl, mod_map, g, w, bd, qkg, rope_tabs, *, tm):
    nq = w.shape[1]
    kvw = NKV * HD
    tps = DEC_SEQ // tm
    in_specs = [
        pl.BlockSpec((tm, D), lambda t: (t, 0)),
        pl.BlockSpec((1, 1, 6 * D), mod_map),
        pl.BlockSpec((1, D), lambda t: (0, 0)),
        pl.BlockSpec((D, nq), lambda t: (0, 0)),
        pl.BlockSpec((256, 256), lambda t: (0, 0)),
        pl.BlockSpec((1, D + kvw), lambda t: (0, 0)),
    ]
    in_specs += [pl.BlockSpec((tm, LANES), lambda t: (t % tps, 0)) for _ in rope_tabs]
    return pl.pallas_call(
        _qkv_sample_gqa_kernel,
        grid=(NS // tm,),
        in_specs=in_specs,
        out_specs=[
            pl.BlockSpec((tm, D), lambda t: (t, 0)),
            pl.BlockSpec((kvw, tm), lambda t: (0, t)),
            pl.BlockSpec((tm, kvw), lambda t: (t, 0)),
        ],
        out_shape=[
            jax.ShapeDtypeStruct((NS, D), BF),
            jax.ShapeDtypeStruct((kvw, NS), BF),
            jax.ShapeDtypeStruct((NS, kvw), BF),
        ],
        compiler_params=_params(("arbitrary",)),
        name="qkv_sample_gqa",
    )(x, modl, g.reshape(1, D), w, bd, qkg, *rope_tabs)


def _lane_lo(rows):
    return lax.broadcasted_iota(jnp.int32, (rows, LANES), 1) < HD


def _pipelined(n, first, second):
    nxt = first(0)
    for i in range(n):
        cur = nxt
        if i + 1 < n:
            nxt = first(i + 1)
        second(i, cur)


def _attn_grouped(q_ref, o_ref, group, stack, key_tiles, value_tiles, *, ones_sum):
    tq = q_ref.shape[0]
    lo = _lane_lo(tq)
    lo_s = _lane_lo(stack * tq)
    pending = {}

    def scores(u):
        g = u * stack // group
        parts = []
        for h in range(u * stack, (u + 1) * stack):
            q2 = q_ref[:, (h // 2) * LANES:(h // 2 + 1) * LANES]
            if h % 2 != g % 2:
                q2 = pltpu.roll(q2.astype(F32), HD, 1).astype(BF)
            parts.append(q2)
        qs = parts[0] if stack == 1 else jnp.concatenate(parts, axis=0)
        mine = lo_s if g % 2 == 0 else jnp.logical_not(lo_s)
        qsel = jnp.where(mine, qs, jnp.zeros_like(qs))
        return [_dot(qsel, kt) for kt in key_tiles(g)]

    def finish(u, ss):
        g = u * stack // group
        m = ss[0].max(axis=-1, keepdims=True)
        for s in ss[1:]:
            m = jnp.maximum(m, s.max(axis=-1, keepdims=True))
        es = [jnp.exp(s - m) for s in ss]
        pv = None
        for e, v in zip(es, value_tiles(g)):
            if ones_sum:
                mine = _lane_lo(v.shape[0])
                if g % 2:
                    mine = jnp.logical_not(mine)
                v = jnp.where(mine, v, jnp.ones_like(v))
            d = _dot(e.astype(BF), v)
            pv = d if pv is None else pv + d
        if ones_sum:
            pv = pv / pltpu.roll(pv, HD, 1)
        else:
            l = es[0].sum(axis=-1, keepdims=True)
            for e in es[1:]:
                l = l + e.sum(axis=-1, keepdims=True)
            pv = pv / l
        for j in range(stack):
            h = u * stack + j
            pj = pv[j * tq:(j + 1) * tq]
            if h % 2 != g % 2:
                pj = pltpu.roll(pj, HD, 1)
            if h % 2 == 0:
                pending[h // 2] = pj
            else:
                sl = slice((h // 2) * LANES, (h // 2 + 1) * LANES)
                o_ref[:, sl] = jnp.where(lo, pending.pop(h // 2), pj).astype(o_ref.dtype)

    _pipelined(NH // stack, scores, finish)


def _attn_prompt_kernel(q_ref, kt_ref, v_ref, o_ref, *, group):
    def tile(g):
        return slice((g // 2) * LANES, (g // 2 + 1) * LANES)
    _attn_grouped(q_ref, o_ref, group, group,
                  lambda g: [kt_ref[0, 0, tile(g), :].astype(BF)],
                  lambda g: [v_ref[:, tile(g)]], ones_sum=False)


def _attn_prompt(q, kt_all, v, i, group):
    kvw = v.shape[1]
    return pl.pallas_call(
        functools.partial(_attn_prompt_kernel, group=group),
        grid=(BATCH,),
        in_specs=[
            pl.BlockSpec((SEQ, D), lambda b: (b, 0)),
            pl.BlockSpec((1, 1, kvw, SEQ), lambda b: (b, i, 0, 0)),
            pl.BlockSpec((SEQ, kvw), lambda b: (b, 0)),
        ],
        out_specs=pl.BlockSpec((SEQ, D), lambda b: (b, 0)),
        out_shape=jax.ShapeDtypeStruct((NP, D), BF),
        compiler_params=_params(("arbitrary",)),
        name="attn_prompt",
    )(q, kt_all, v)


def _attn_gqa_sample_kernel(q_ref, kt_ref, v_ref, kct_ref, vct_ref, o_ref):
    vc = vct_ref[0, 0].T.astype(BF)

    def tile(g):
        return slice((g // 2) * LANES, (g // 2 + 1) * LANES)
    _attn_grouped(q_ref, o_ref, NH // NKV, NH // NKV,
                  lambda g: [kt_ref[tile(g), :], kct_ref[0, 0, tile(g), :].astype(BF)],
                  lambda g: [v_ref[:, tile(g)], vc[:, tile(g)]], ones_sum=True)


GQA_TQ = 256


def _attn_gqa_sample(q, kt, v, cache_kt, cache_vt, i):
    nq = DEC_SEQ // GQA_TQ
    kvw = NKV * HD
    return pl.pallas_call(
        _attn_gqa_sample_kernel,
        grid=(DEC_BATCH, nq),
        in_specs=[
            pl.BlockSpec((GQA_TQ, D), lambda b, t: (b * nq + t, 0)),
            pl.BlockSpec((kvw, DEC_SEQ), lambda b, t: (0, b)),
            pl.BlockSpec((DEC_SEQ, kvw), lambda b, t: (b, 0)),
            pl.BlockSpec((1, 1, kvw, PAST), lambda b, t: (b, i, 0, 0)),
            pl.BlockSpec((1, 1, kvw, PAST), lambda b, t: (b, i, 0, 0)),
        ],
        out_specs=pl.BlockSpec((GQA_TQ, D), lambda b, t: (b * nq + t, 0)),
        out_shape=jax.ShapeDtypeStruct((NS, D), BF),
        compiler_params=_params(("arbitrary", "arbitrary")),
        name="attn_gqa_sample",
    )(q, kt, v, cache_kt, cache_vt)


NA_QROWS = 4
NA_KBLK = NA_QROWS * GRID_W
NA_NKB = 3
NA_PAIRS = NA_QROWS // 2


def _attn_na_sample_kernel(*refs):
    q_ref = refs[0]
    kt_refs = refs[1:1 + NA_NKB]
    v_refs = refs[1 + NA_NKB:1 + 2 * NA_NKB]
    kct_ref, vct_ref, p_ref, o_ref = refs[1 + 2 * NA_NKB:]
    rb = pl.program_id(1)
    kb0 = jnp.clip(rb - 1, 0, ROWS // NA_QROWS - NA_NKB)
    tq = q_ref.shape[0]
    lo = _lane_lo(tq)
    hi = jnp.logical_not(lo)
    lo_w = _lane_lo(GRID_W)
    vc = vct_ref[0, 0].T.astype(BF)
    ones = jnp.ones((tq, LANES), BF)

    tiles = []
    for i in range(NA_QROWS):
        qr = rb * NA_QROWS + i
        rs = jnp.clip(qr - NA_KH // 2, 0, ROWS - NA_KH)
        for j in range(NA_NKB):
            for jj in range(NA_PAIRS):
                kra = (kb0 + j) * NA_QROWS + 2 * jj
                krb = kra + 1
                idx = jnp.clip(kra - qr + NA_KH, 0, 2 * NA_KH - 1)
                va = jnp.logical_and(kra >= rs, kra < rs + NA_KH).astype(jnp.int32)
                vb = jnp.logical_and(krb >= rs, krb < rs + NA_KH).astype(jnp.int32)
                tiles.append((idx, jnp.where(lo_w, va, vb) > 0))

    def scores(h):
        rows = slice((h // 2) * LANES, (h // 2 + 1) * LANES)
        q2 = q_ref[:, rows]
        qsel = jnp.where(lo if h % 2 == 0 else hi, q2, jnp.zeros_like(q2))
        s_lat = [_dot(qsel, r[rows, :]) for r in kt_refs]
        out_rows = []
        for i in range(NA_QROWS):
            rsl = slice(i * GRID_W, (i + 1) * GRID_W)
            row = []
            for j in range(NA_NKB):
                for jj in range(NA_PAIRS):
                    idx, valid = tiles[(i * NA_NKB + j) * NA_PAIRS + jj]
                    t = s_lat[j][rsl, jj * LANES:(jj + 1) * LANES] + p_ref[0, h, idx]
                    row.append(jnp.where(valid, t, NEG))
            out_rows.append(jnp.concatenate(row, axis=1))
        return jnp.concatenate(out_rows, axis=0), _dot(qsel, kct_ref[0, 0, rows, :].astype(BF))

    def weighted_values(h, s, s_ctx):
        rows = slice((h // 2) * LANES, (h // 2 + 1) * LANES)
        mine = lo if h % 2 == 0 else hi
        m = jnp.maximum(s.max(axis=-1, keepdims=True), s_ctx.max(axis=-1, keepdims=True))
        e = jnp.exp(s - m).astype(BF)
        e_ctx = jnp.exp(s_ctx - m).astype(BF)
        pv = _dot(e_ctx, jnp.where(mine, vc[:, rows], ones))
        for j in range(NA_NKB):
            vj = jnp.where(mine, v_refs[j][:, rows], ones)
            pv = pv + _dot(e[:, j * NA_KBLK:(j + 1) * NA_KBLK], vj)
        return pv / pltpu.roll(pv, HD, 1)

    nxt = scores(0)
    o2 = None
    for h in range(NH):
        cur = nxt
        if h + 1 < NH:
            nxt = scores(h + 1)
        pv = weighted_values(h, *cur)
        if h % 2 == 0:
            o2 = pv
        else:
            o_ref[:, (h // 2) * LANES:(h // 2 + 1) * LANES] = jnp.where(lo, o2, pv).astype(o_ref.dtype)


def _attn_na_sample(q, kt, v, cache_kt, cache_vt, i, ptab):
    nrb = ROWS // NA_QROWS

    def kb(b, r, j):
        return b * nrb + jnp.clip(r - 1, 0, nrb - NA_NKB) + j

    in_specs = [pl.BlockSpec((NA_KBLK, D), lambda b, r: (b * nrb + r, 0))]
    in_specs += [pl.BlockSpec((D, NA_KBLK), functools.partial(lambda b, r, j: (0, kb(b, r, j)), j=j))
                 for j in range(NA_NKB)]
    in_specs += [pl.BlockSpec((NA_KBLK, D), functools.partial(lambda b, r, j: (kb(b, r, j), 0), j=j))
                 for j in range(NA_NKB)]
    in_specs += [
        pl.BlockSpec((1, 1, D, PAST), lambda b, r: (b, i, 0, 0)),
        pl.BlockSpec((1, 1, D, PAST), lambda b, r: (b, i, 0, 0)),
        pl.BlockSpec((1, NH, 2 * NA_KH, GRID_W, LANES), lambda b, r: (i, 0, 0, 0, 0),
                     pipeline_mode=pl.Buffered(1)),
    ]
    return pl.pallas_call(
        _attn_na_sample_kernel,
        grid=(DEC_BATCH, nrb),
        in_specs=in_specs,
        out_specs=pl.BlockSpec((NA_KBLK, D), lambda b, r: (b * nrb + r, 0)),
        out_shape=jax.ShapeDtypeStruct((NS, D), BF),
        compiler_params=_params(("arbitrary", "arbitrary")),
        name="attn_na_sample",
    )(q, *([kt] * NA_NKB), *([v] * NA_NKB), cache_kt, cache_vt, ptab)


N_DC = 2 * NA_KW - 1
N_DR = 2 * NA_KH - 1
DR_GROUP = 5


def _na_bias_kernel(rpb_ref, o_ref):
    base = (pl.program_id(0) * NH + pl.program_id(1)) * (N_DR * N_DC)
    qc = lax.broadcasted_iota(jnp.int32, (GRID_W, LANES), 0)
    lane = lax.broadcasted_iota(jnp.int32, (GRID_W, LANES), 1)
    kc = lane % GRID_W
    diff = kc - qc + (NA_KW - 1)
    c_start = jnp.clip(qc - NA_KW // 2, 0, GRID_W - NA_KW)
    ok = jnp.logical_and(kc >= c_start, kc < c_start + NA_KW)
    lo = lane < GRID_W
    neg = jnp.full((GRID_W, LANES), NEG, F32)
    prev = neg
    for g0 in range(0, N_DR, DR_GROUP):
        accs = [jnp.zeros((GRID_W, LANES), F32) for _ in range(DR_GROUP)]
        for p in range(N_DC):
            hit = diff == p
            for k in range(DR_GROUP):
                accs[k] = jnp.where(hit, rpb_ref[base + (g0 + k) * N_DC + p], accs[k])
        for k in range(DR_GROUP):
            cur = jnp.where(ok, accs[k], neg)
            o_ref[0, 0, g0 + k] = jnp.where(lo, prev, cur)
            prev = cur
    o_ref[0, 0, N_DR] = jnp.where(lo, prev, neg)


def _na_bias_tables(rpb):
    n_a = rpb.shape[0]
    return pl.pallas_call(
        _na_bias_kernel,
        grid=(n_a, NH),
        in_specs=[pl.BlockSpec(memory_space=pltpu.SMEM)],
        out_specs=pl.BlockSpec((1, 1, N_DR + 1, GRID_W, LANES), lambda a, h: (a, h, 0, 0, 0)),
        out_shape=jax.ShapeDtypeStruct((n_a, NH, N_DR + 1, GRID_W, LANES), F32),
        compiler_params=_params(("arbitrary", "arbitrary")),
        name="na_bias",
    )(rpb.reshape(-1))


HALO_O = 16


def _mix_ffn_kernel(*refs, halo, tiles_per_seq):
    if halo:
        (o_ref, op_ref, on_ref, x_ref, xp_ref, xn_ref, mod_ref, gmix_ref, wo_ref,
         gpre_ref, gpost_ref, wup_ref, cw_ref, cb_ref, wdn_ref, out_ref, act_scr) = refs
    else:
        (o_ref, x_ref, mod_ref, gmix_ref, wo_ref,
         gpre_ref, gpost_ref, wup_ref, cw_ref, cb_ref, wdn_ref, out_ref, act_scr) = refs
    tm = x_ref.shape[0]
    m = mod_ref[0]
    gate_mix = m[:, 2 * D:3 * D]
    sh = m[:, 3 * D:4 * D]
    sc = m[:, 4 * D:5 * D]
    gate = m[:, 5 * D:6 * D]
    zeros = jnp.zeros((SUBLANES, D), F32)

    if halo:
        o_ext = jnp.concatenate([op_ref[...].astype(F32)[HALO_O - SUBLANES:], o_ref[...].astype(F32),
                                 on_ref[...].astype(F32)[:SUBLANES]], axis=0).astype(BF)
        x_ext = jnp.concatenate([xp_ref[...], x_ref[...], xn_ref[...]], axis=0)
    else:
        o_ext = o_ref[...]
        x_ext = x_ref[...]
    x_ext = x_ext + gate_mix * _rms(_dot(o_ext, wo_ref[...]), gmix_ref[...])
    h = _rms(x_ext, gpre_ref[...]) * (1.0 + sc) + sh
    if halo:
        t = pl.program_id(0) % tiles_per_seq
        x = x_ext[SUBLANES:SUBLANES + tm]
        hp = jnp.where(t == 0, zeros, h[:SUBLANES])
        hn = jnp.where(t == tiles_per_seq - 1, zeros, h[SUBLANES + tm:])
        hext = jnp.concatenate([hp, h[SUBLANES:SUBLANES + tm], hn], axis=0).astype(BF)
    else:
        x = x_ext
        hext = jnp.concatenate([zeros, h, zeros], axis=0).astype(BF)
    mext = tm + 2 * SUBLANES

    def conv(u, c0):
        cols = slice(c0, c0 + FF_CHUNK)
        prev = pltpu.roll(u, 1, 0)[SUBLANES:SUBLANES + tm]
        nxt = pltpu.roll(u, mext - 1, 0)[SUBLANES:SUBLANES + tm]
        cur = u[SUBLANES:SUBLANES + tm]
        return (prev * cw_ref[0:1, cols] + cur * cw_ref[1:2, cols] + nxt * cw_ref[2:3, cols]
                + cb_ref[:, cols])

    for j in range(N_CHUNK):
        ca = j * FF_CHUNK
        cg = DFF + j * FF_CHUNK
        a = conv(_dot(hext, wup_ref[:, ca:ca + FF_CHUNK]), ca)
        gt = conv(_dot(hext, wup_ref[:, cg:cg + FF_CHUNK]), cg)
        act_scr[:, ca:ca + FF_CHUNK] = (a * jax.nn.sigmoid(a) * gt).astype(BF)
    y = _dot(act_scr[...], wdn_ref[...])
    out_ref[...] = x + gate * _rms(y, gpost_ref[...])


def _mix_ffn(o, x, modl, mod_map, gmix, w_o, gpre, gpost, wup, cw, cb, wdn, *, tm, halo):
    n = x.shape[0]
    tps = DEC_SEQ // tm

    def prev_blk(rows):
        r = tm // rows
        return lambda t: (jnp.maximum(t * r - 1, 0), 0)

    def next_blk(rows):
        r = tm // rows
        return lambda t: (jnp.minimum((t + 1) * r, n // rows - 1), 0)

    in_specs = [pl.BlockSpec((tm, D), lambda t: (t, 0))]
    args = [o]
    if halo:
        in_specs += [pl.BlockSpec((HALO_O, D), prev_blk(HALO_O)), pl.BlockSpec((HALO_O, D), next_blk(HALO_O))]
        args += [o, o]
    in_specs.append(pl.BlockSpec((tm, D), lambda t: (t, 0)))
    args.append(x)
    if halo:
        in_specs += [pl.BlockSpec((SUBLANES, D), prev_blk(SUBLANES)),
                     pl.BlockSpec((SUBLANES, D), next_blk(SUBLANES))]
        args += [x, x]
    const = dict(pipeline_mode=pl.Buffered(1))
    in_specs += [
        pl.BlockSpec((1, 1, 6 * D), mod_map),
        pl.BlockSpec((1, D), lambda t: (0, 0)),
        pl.BlockSpec((D, D), lambda t: (0, 0), **const),
        pl.BlockSpec((1, D), lambda t: (0, 0)),
        pl.BlockSpec((1, D), lambda t: (0, 0)),
        pl.BlockSpec((D, 2 * DFF), lambda t: (0, 0), **const),
        pl.BlockSpec((3, 2 * DFF), lambda t: (0, 0)),
        pl.BlockSpec((1, 2 * DFF), lambda t: (0, 0)),
        pl.BlockSpec((DFF, D), lambda t: (0, 0), **const),
    ]
    args += [modl, gmix.reshape(1, D), w_o, gpre.reshape(1, D), gpost.reshape(1, D), wup, cw, cb, wdn]
    return pl.pallas_call(
        functools.partial(_mix_ffn_kernel, halo=halo, tiles_per_seq=tps),
        grid=(n // tm,),
        in_specs=in_specs,
        out_specs=pl.BlockSpec((tm, D), lambda t: (t, 0)),
        out_shape=jax.ShapeDtypeStruct((n, D), F32),
        scratch_shapes=[pltpu.VMEM((tm, DFF), BF)],
        compiler_params=_params(("arbitrary",)),
        name="mix_ffn",
    )(*args)


def _rope_tables():
    t = jnp.arange(DEC_SEQ)
    rows = (t // GRID_W).astype(F32)
    cols = (t % GRID_W).astype(F32)
    quarter = HD // 4
    freqs = ROPE_BASE ** (-jnp.arange(quarter, dtype=F32) / quarter)
    lane = jnp.arange(LANES)
    d = lane % HD
    e = d % (HD // 2)
    is_x2 = (e >= quarter)[None, :]
    pos = jnp.where((d // (HD // 2) == 0)[None, :], rows[:, None], cols[:, None])
    ang = pos * freqs[e % quarter][None, :]
    cos = jnp.cos(ang)
    sin = jnp.sin(ang)
    return cos, jnp.where(is_x2, 0.0, -sin), jnp.where(is_x2, sin, 0.0)


def _block_diag_ones():
    r = jnp.arange(256)
    return (r[:, None] // HD == r[None, :] // HD).astype(BF)


def _cache_t(cache):
    b, n, p, h, d = cache.shape
    return cache.transpose(0, 1, 3, 4, 2).reshape(b, n, h * d, p)


def _cache_out(kt, heads):
    b, n, _, s = kt.shape
    return kt.reshape(b, n, heads, HD, s).transpose(0, 1, 4, 2, 3)


def kernel(x_prompt, x_sample, cache_na_k, cache_na_v, cache_gqa_k, cache_gqa_v, c, c_ctx,
           ada_w, ada_b, norm_mix_pre, norm_mix_post, norm_ffn_pre, norm_ffn_post,
           na_w_qkv, na_w_o, na_rpb, gqa_w_qkv, gqa_w_o, gqa_q_norm, gqa_k_norm,
           ffn_w_up, ffn_conv_w, ffn_conv_b, ffn_w_down):
    n_a = cache_na_k.shape[1]
    n_b = cache_gqa_k.shape[1]
    cond8 = jnp.concatenate([c_ctx[None], c, jnp.zeros((SUBLANES - 1 - DEC_BATCH, D), F32)], axis=0)
    mod = _adaln(cond8, ada_w, ada_b)

    tm_p, tm_s = 512, 512
    tm_fp, tm_fs = 256, 512
    map_p = lambda t: (0, 0, 0)
    map_s = lambda t: (1 + t // (DEC_SEQ // tm_s), 0, 0)
    map_fs = lambda t: (1 + t // (DEC_SEQ // tm_fs), 0, 0)

    cna_kt, cna_vt = _cache_t(cache_na_k), _cache_t(cache_na_v)
    cg_kt, cg_vt = _cache_t(cache_gqa_k), _cache_t(cache_gqa_v)
    rope_tabs = _rope_tables()
    bd = _block_diag_ones()
    ptabs = _na_bias_tables(na_rpb)
    kvg = NKV * HD

    xp = x_prompt.reshape(NP, D)
    xs = x_sample.reshape(NS, D)
    na_kv = None
    gq_kv = None
    for l in range(DEPTH):
        i = l // 2
        modl = mod[l].reshape(SUBLANES, 1, 6 * D)
        if l % 2 == 0:
            w = na_w_qkv[i]
            w_o = na_w_o[i].astype(BF)
            wqv = jnp.concatenate([w[:, :D], w[:, 2 * D:]], axis=1).astype(BF)
            wkt = w[:, D:2 * D].T.astype(BF)
            qp, vp, kt, vt = _qkv_prompt(xp, modl, norm_mix_pre[l], wqv, wkt, i, n_a, na_kv,
                                         gqa=False, tm=tm_p)
            na_kv = (kt, vt)
            qs, kts, vs = _qkv_sample_na(xs, modl, map_s, norm_mix_pre[l], wqv, wkt, tm=tm_s)
            op = _attn_prompt(qp, kt, vp, i, 1)
            os_ = _attn_na_sample(qs, kts, vs, cna_kt, cna_vt, i, ptabs)
        else:
            w = gqa_w_qkv[i]
            w_o = gqa_w_o[i].astype(BF)
            wqv = jnp.concatenate([w[:, :D], w[:, D + kvg:]], axis=1).astype(BF)
            wkt = w[:, D:D + kvg].T.astype(BF)
            qg = (jnp.tile(gqa_q_norm[i], NH) * 0.125).reshape(1, D)
            kg = jnp.tile(gqa_k_norm[i], NKV)
            qkg = jnp.concatenate([qg[0], kg]).reshape(1, D + kvg)
            kg2 = jnp.broadcast_to(kg[:, None], (kvg, SEQ))
            qp, vp, kt, vt = _qkv_prompt(xp, modl, norm_mix_pre[l], wqv, wkt, i, n_b, gq_kv,
                                         gqa=True, tm=tm_p, extra=(bd, qg, kg2))
            gq_kv = (kt, vt)
            qs, kts, vs = _qkv_sample_gqa(xs, modl, map_s, norm_mix_pre[l], w.astype(BF), bd, qkg,
                                          rope_tabs, tm=tm_s)
            op = _attn_prompt(qp, kt, vp, i, NH // NKV)
            os_ = _attn_gqa_sample(qs, kts, vs, cg_kt, cg_vt, i)
        wup = ffn_w_up[l].astype(BF)
        wdn = ffn_w_down[l].astype(BF)
        cw = ffn_conv_w[l]
        cb = ffn_conv_b[l].reshape(1, 2 * DFF)
        xp = _mix_ffn(op, xp, modl, map_p, norm_mix_post[l], w_o, norm_ffn_pre[l], norm_ffn_post[l],
                      wup, cw, cb, wdn, tm=tm_fp, halo=False)
        xs = _mix_ffn(os_, xs, modl, map_fs, norm_mix_post[l], w_o, norm_ffn_pre[l], norm_ffn_post[l],
                      wup, cw, cb, wdn, tm=tm_fs, halo=True)

    return (xp.reshape(BATCH, SEQ, D), xs.reshape(DEC_BATCH, DEC_SEQ, D),
            _cache_out(na_kv[0], NH), _cache_out(na_kv[1], NH),
            _cache_out(gq_kv[0], NKV), _cache_out(gq_kv[1], NKV))
```

```python
import functools

import jax
import jax.numpy as jnp
from jax import lax
from jax.experimental import pallas as pl
from jax.experimental.pallas import tpu as pltpu

D = 1024
HD = 64
NH = 16
NKV = 4
DFF = 2816
DEPTH = 4
GRID_W = 64
ROWS = 32
NA_KH = 8
NA_KW = 16
ROPE_BASE = 10000.0
EPS = 1e-6
BATCH, SEQ = 32, 256
DEC_BATCH, DEC_SEQ = 2, 2048
PAST = 256
NP = BATCH * SEQ
NS = DEC_BATCH * DEC_SEQ

BF = jnp.bfloat16
F32 = jnp.float32
NEG = -1e30
LANES = 128
SUBLANES = 8
FF_CHUNK = 256
N_CHUNK = DFF // FF_CHUNK
VMEM_LIMIT = 50 * 1024 * 1024


def _params(sem):
    return pltpu.CompilerParams(dimension_semantics=sem, vmem_limit_bytes=VMEM_LIMIT)


def _rms(x, g):
    return x * lax.rsqrt(jnp.mean(x * x, axis=-1, keepdims=True) + EPS) * g


def _dot(a, b):
    return jnp.dot(a, b, preferred_element_type=F32)


def _dot_tn_nt(a, b):
    return lax.dot_general(a, b, (((0,), (1,)), ((), ())), preferred_element_type=F32)


def _qkv_weight_specs(i, gqa):
    if gqa:
        kvw = NKV * HD
        return [pl.BlockSpec((1, D, D), lambda t: (i, 0, 0)),
                pl.BlockSpec((1, D, kvw), lambda t: (i, 0, D // kvw)),
                pl.BlockSpec((1, D, kvw), lambda t: (i, 0, D // kvw + 1))]
    return [pl.BlockSpec((1, D, D), functools.partial(lambda t, c: (i, 0, c), c=c)) for c in range(3)]


def _mod_kernel(cond_ref, w_ref, b_ref, o_ref):
    s = cond_ref[...]
    s = s * jax.nn.sigmoid(s)
    o_ref[0] = _dot(s.astype(BF), w_ref[0].astype(BF)) + b_ref[0]


def _adaln(cond8, ada_w, ada_b):
    tn = 1536
    return pl.pallas_call(
        _mod_kernel,
        grid=(DEPTH, 6 * D // tn),
        in_specs=[
            pl.BlockSpec((SUBLANES, D), lambda l, n: (0, 0)),
            pl.BlockSpec((1, D, tn), lambda l, n: (l, 0, n)),
            pl.BlockSpec((1, 1, tn), lambda l, n: (l, 0, n)),
        ],
        out_specs=pl.BlockSpec((1, SUBLANES, tn), lambda l, n: (l, 0, n)),
        out_shape=jax.ShapeDtypeStruct((DEPTH, SUBLANES, 6 * D), F32),
        compiler_params=_params(("arbitrary", "arbitrary")),
        name="adaln",
    )(cond8, ada_w, ada_b.reshape(DEPTH, 1, 6 * D))


def _prenorm(x_ref, mod_ref, g_ref):
    m = mod_ref[0]
    return (_rms(x_ref[...], g_ref[...]) * (1.0 + m[:, D:2 * D]) + m[:, 0:D]).astype(BF)


def _head_rms_lanes(blk, bd_ref):
    ss = _dot((blk * blk).astype(BF), bd_ref[...])
    return blk * lax.rsqrt(ss * (1.0 / HD) + EPS)


def _qkv_prompt_kernel(*refs, gqa, slot):
    if gqa:
        (x_ref, mod_ref, g_ref, wq_ref, wk_ref, wv_ref, bd_ref, qg_ref, kg_ref,
         q_ref, v_ref, kt_ref, vt_ref) = refs
    else:
        x_ref, mod_ref, g_ref, wq_ref, wk_ref, wv_ref, q_ref, v_ref, kt_ref, vt_ref = refs
    hb = _prenorm(x_ref, mod_ref, g_ref)
    q = _dot(hb, wq_ref[0])
    v = _dot(hb, wv_ref[0])
    kt = _dot_tn_nt(wk_ref[0], hb)
    if gqa:
        for t in range(D // 256):
            sl = slice(t * 256, (t + 1) * 256)
            q_ref[:, sl] = (_head_rms_lanes(q[:, sl], bd_ref) * qg_ref[:, sl]).astype(q_ref.dtype)
        heads = []
        for h in range(NKV):
            blk = kt[h * HD:(h + 1) * HD, :]
            heads.append(blk * lax.rsqrt(jnp.mean(blk * blk, axis=0, keepdims=True) + EPS))
        kt = jnp.concatenate(heads, axis=0)
    else:
        q_ref[...] = (q * 0.125).astype(q_ref.dtype)
    v_ref[...] = v.astype(v_ref.dtype)
    for s in range(x_ref.shape[0] // SEQ):
        sl = slice(s * SEQ, (s + 1) * SEQ)
        kts = kt[:, sl]
        if gqa:
            kts = kts * kg_ref[...]
        kt_ref[s, slot] = kts
        vt_ref[s, slot] = v[sl, :].T
        for other in range(kt_ref.shape[1]):
            if other != slot:
                kt_ref[s, other] = jnp.zeros_like(kts)
                vt_ref[s, other] = jnp.zeros_like(kts)


def _qkv_prompt(x, modl, g, w, i, n_layers, prev, *, gqa, tm, extra=()):
    kvw = NKV * HD if gqa else D
    in_specs = [
        pl.BlockSpec((tm, D), lambda t: (t, 0)),
        pl.BlockSpec((1, 1, 6 * D), lambda t: (0, 0, 0)),
        pl.BlockSpec((1, D), lambda t: (0, 0)),
    ] + _qkv_weight_specs(i, gqa)
    args = [x, modl, g.reshape(1, D), w, w, w]
    if gqa:
        in_specs += [pl.BlockSpec((256, 256), lambda t: (0, 0)),
                     pl.BlockSpec((1, D), lambda t: (0, 0)),
                     pl.BlockSpec((kvw, SEQ), lambda t: (0, 0))]
        args += list(extra)
    n_in = len(args)
    aliases = {}
    if prev is None:
        kv_spec = pl.BlockSpec((tm // SEQ, n_layers, kvw, SEQ), lambda t: (t, 0, 0, 0))
        slot = i
    else:
        in_specs += [pl.BlockSpec(memory_space=pl.ANY), pl.BlockSpec(memory_space=pl.ANY)]
        args += list(prev)
        aliases = {n_in: 2, n_in + 1: 3}
        kv_spec = pl.BlockSpec((tm // SEQ, 1, kvw, SEQ), lambda t: (t, i, 0, 0))
        slot = 0
    kv_shape = jax.ShapeDtypeStruct((BATCH, n_layers, kvw, SEQ), F32)

    def body(*refs):
        _qkv_prompt_kernel(*(refs[:n_in] + refs[len(args):]), gqa=gqa, slot=slot)

    return pl.pallas_call(
        body,
        grid=(NP // tm,),
        in_specs=in_specs,
        out_specs=[
            pl.BlockSpec((tm, D), lambda t: (t, 0)),
            pl.BlockSpec((tm, kvw), lambda t: (t, 0)),
            kv_spec, kv_spec,
        ],
        out_shape=[
            jax.ShapeDtypeStruct((NP, D), BF),
            jax.ShapeDtypeStruct((NP, kvw), BF),
            kv_shape, kv_shape,
        ],
        input_output_aliases=aliases,
        compiler_params=_params(("arbitrary",)),
        name="qkv_prompt_gqa" if gqa else "qkv_prompt_na",
    )(*args)


def _qkv_sample_na_kernel(x_ref, mod_ref, g_ref, wq_ref, wk_ref, wv_ref, q_ref, kt_ref, v_ref):
    hb = _prenorm(x_ref, mod_ref, g_ref)
    q_ref[...] = (_dot(hb, wq_ref[0]) * 0.125).astype(q_ref.dtype)
    v_ref[...] = _dot(hb, wv_ref[0]).astype(v_ref.dtype)
    kt_ref[...] = _dot_tn_nt(wk_ref[0], hb).astype(kt_ref.dtype)


def _qkv_sample_na(x, modl, mod_map, g, w, i, *, tm):
    return pl.pallas_call(
        _qkv_sample_na_kernel,
        grid=(NS // tm,),
        in_specs=[
            pl.BlockSpec((tm, D), lambda t: (t, 0)),
            pl.BlockSpec((1, 1, 6 * D), mod_map),
            pl.BlockSpec((1, D), lambda t: (0, 0)),
        ] + _qkv_weight_specs(i, False),
        out_specs=[
            pl.BlockSpec((tm, D), lambda t: (t, 0)),
            pl.BlockSpec((D, tm), lambda t: (0, t)),
            pl.BlockSpec((tm, D), lambda t: (t, 0)),
        ],
        out_shape=[
            jax.ShapeDtypeStruct((NS, D), BF),
            jax.ShapeDtypeStruct((D, NS), BF),
            jax.ShapeDtypeStruct((NS, D), BF),
        ],
        compiler_params=_params(("arbitrary",)),
        name="qkv_sample_na",
    )(x, modl, g.reshape(1, D), w, w, w)


def _qkv_sample_gqa_kernel(x_ref, mod_ref, g_ref, w_ref, bd_ref, qkg_ref, cos_ref, s1_ref, s2_ref,
                           q_ref, kt_ref, v_ref):
    qkv = _dot(_prenorm(x_ref, mod_ref, g_ref), w_ref[0])
    nqk = D + NKV * HD
    for t in range(nqk // LANES):
        if t % 2 == 0:
            nrm = _head_rms_lanes(qkv[:, (t // 2) * 256:(t // 2 + 1) * 256], bd_ref)
        xt = nrm[:, (t % 2) * LANES:(t % 2 + 1) * LANES] * qkg_ref[:, t * LANES:(t + 1) * LANES]
        xt = (xt * cos_ref[...]
              + pltpu.roll(xt, LANES - 16, 1) * s1_ref[...]
              + pltpu.roll(xt, 16, 1) * s2_ref[...])
        if t < D // LANES:
            q_ref[:, t * LANES:(t + 1) * LANES] = xt.astype(q_ref.dtype)
        else:
            tk = t - D // LANES
            kt_ref[tk * LANES:(tk + 1) * LANES, :] = xt.T.astype(kt_ref.dtype)
    v_ref[...] = qkv[:, nqk:].astype(v_ref.dtype)


def _qkv_sample_gqa(x, modl, mod_map, g, w, i, bd, qkg, rope_tabs, *, tm):
    nq = w.shape[2]
    kvw = NKV * HD
    tps = DEC_SEQ // tm
    in_specs = [
        pl.BlockSpec((tm, D), lambda t: (t, 0)),
        pl.BlockSpec((1, 1, 6 * D), mod_map),
        pl.BlockSpec((1, D), lambda t: (0, 0)),
        pl.BlockSpec((1, D, nq), lambda t: (i, 0, 0)),
        pl.BlockSpec((256, 256), lambda t: (0, 0)),
        pl.BlockSpec((1, D + kvw), lambda t: (0, 0)),
    ]
    in_specs += [pl.BlockSpec((tm, LANES), lambda t: (t % tps, 0)) for _ in rope_tabs]
    return pl.pallas_call(
        _qkv_sample_gqa_kernel,
        grid=(NS // tm,),
        in_specs=in_specs,
        out_specs=[
            pl.BlockSpec((tm, D), lambda t: (t, 0)),
            pl.BlockSpec((kvw, tm), lambda t: (0, t)),
            pl.BlockSpec((tm, kvw), lambda t: (t, 0)),
        ],
        out_shape=[
            jax.ShapeDtypeStruct((NS, D), BF),
            jax.ShapeDtypeStruct((kvw, NS), BF),
            jax.ShapeDtypeStruct((NS, kvw), BF),
        ],
        compiler_params=_params(("arbitrary",)),
        name="qkv_sample_gqa",
    )(x, modl, g.reshape(1, D), w, bd, qkg, *rope_tabs)


def _lane_lo(rows):
    return lax.broadcasted_iota(jnp.int32, (rows, LANES), 1) < HD


def _pipelined(n, first, second):
    nxt = first(0)
    for i in range(n):
        cur = nxt
        if i + 1 < n:
            nxt = first(i + 1)
        second(i, cur)


def _attn_grouped(q_ref, o_ref, group, stack, key_tiles, value_tiles, *, ones_sum):
    tq = q_ref.shape[0]
    lo = _lane_lo(tq)
    lo_s = _lane_lo(stack * tq)
    pending = {}

    def scores(u):
        g = u * stack // group
        parts = []
        for h in range(u * stack, (u + 1) * stack):
            q2 = q_ref[:, (h // 2) * LANES:(h // 2 + 1) * LANES]
            if h % 2 != g % 2:
                q2 = pltpu.roll(q2.astype(F32), HD, 1).astype(BF)
            parts.append(q2)
        qs = parts[0] if stack == 1 else jnp.concatenate(parts, axis=0)
        mine = lo_s if g % 2 == 0 else jnp.logical_not(lo_s)
        qsel = jnp.where(mine, qs, jnp.zeros_like(qs))
        return [_dot(qsel, kt) for kt in key_tiles(g)]

    def finish(u, ss):
        g = u * stack // group
        m = ss[0].max(axis=-1, keepdims=True)
        for s in ss[1:]:
            m = jnp.maximum(m, s.max(axis=-1, keepdims=True))
        es = [jnp.exp(s - m) for s in ss]
        pv = None
        for e, v in zip(es, value_tiles(g)):
            if ones_sum:
                mine = _lane_lo(v.shape[0])
                if g % 2:
                    mine = jnp.logical_not(mine)
                v = jnp.where(mine, v, jnp.ones_like(v))
            d = _dot(e.astype(BF), v)
            pv = d if pv is None else pv + d
        if ones_sum:
            pv = pv / pltpu.roll(pv, HD, 1)
        else:
            l = es[0].sum(axis=-1, keepdims=True)
            for e in es[1:]:
                l = l + e.sum(axis=-1, keepdims=True)
            pv = pv / l
        for j in range(stack):
            h = u * stack + j
            pj = pv[j * tq:(j + 1) * tq]
            if h % 2 != g % 2:
                pj = pltpu.roll(pj, HD, 1)
            if h % 2 == 0:
                pending[h // 2] = pj
            else:
                sl = slice((h // 2) * LANES, (h // 2 + 1) * LANES)
                o_ref[:, sl] = jnp.where(lo, pending.pop(h // 2), pj).astype(o_ref.dtype)

    _pipelined(NH // stack, scores, finish)


def _attn_prompt_kernel(q_ref, kt_ref, v_ref, o_ref, *, group):
    def tile(g):
        return slice((g // 2) * LANES, (g // 2 + 1) * LANES)
    _attn_grouped(q_ref, o_ref, group, group,
                  lambda g: [kt_ref[0, 0, tile(g), :].astype(BF)],
                  lambda g: [v_ref[:, tile(g)]], ones_sum=False)


def _attn_prompt(q, kt_all, v, i, group):
    kvw = v.shape[1]
    return pl.pallas_call(
        functools.partial(_attn_prompt_kernel, group=group),
        grid=(BATCH,),
        in_specs=[
            pl.BlockSpec((SEQ, D), lambda b: (b, 0)),
            pl.BlockSpec((1, 1, kvw, SEQ), lambda b: (b, i, 0, 0)),
            pl.BlockSpec((SEQ, kvw), lambda b: (b, 0)),
        ],
        out_specs=pl.BlockSpec((SEQ, D), lambda b: (b, 0)),
        out_shape=jax.ShapeDtypeStruct((NP, D), BF),
        compiler_params=_params(("arbitrary",)),
        name="attn_prompt",
    )(q, kt_all, v)


def _attn_gqa_sample_kernel(q_ref, kt_ref, v_ref, kct_ref, vct_ref, o_ref):
    vc = vct_ref[0, 0].T.astype(BF)

    def tile(g):
        return slice((g // 2) * LANES, (g // 2 + 1) * LANES)
    _attn_grouped(q_ref, o_ref, NH // NKV, NH // NKV,
                  lambda g: [kt_ref[tile(g), :], kct_ref[0, 0, tile(g), :].astype(BF)],
                  lambda g: [v_ref[:, tile(g)], vc[:, tile(g)]], ones_sum=True)


GQA_TQ = 256


def _attn_gqa_sample(q, kt, v, cache_kt, cache_vt, i):
    nq = DEC_SEQ // GQA_TQ
    kvw = NKV * HD
    return pl.pallas_call(
        _attn_gqa_sample_kernel,
        grid=(DEC_BATCH, nq),
        in_specs=[
            pl.BlockSpec((GQA_TQ, D), lambda b, t: (b * nq + t, 0)),
            pl.BlockSpec((kvw, DEC_SEQ), lambda b, t: (0, b)),
            pl.BlockSpec((DEC_SEQ, kvw), lambda b, t: (b, 0)),
            pl.BlockSpec((1, 1, kvw, PAST), lambda b, t: (b, i, 0, 0)),
            pl.BlockSpec((1, 1, kvw, PAST), lambda b, t: (b, i, 0, 0)),
        ],
        out_specs=pl.BlockSpec((GQA_TQ, D), lambda b, t: (b * nq + t, 0)),
        out_shape=jax.ShapeDtypeStruct((NS, D), BF),
        compiler_params=_params(("arbitrary", "arbitrary")),
        name="attn_gqa_sample",
    )(q, kt, v, cache_kt, cache_vt)


NA_QROWS = 4
NA_KBLK = NA_QROWS * GRID_W
NA_NKB = 3
NA_PAIRS = NA_QROWS // 2


def _attn_na_sample_kernel(*refs):
    q_ref = refs[0]
    kt_refs = refs[1:1 + NA_NKB]
    v_refs = refs[1 + NA_NKB:1 + 2 * NA_NKB]
    kct_ref, vct_ref, p_ref, o_ref = refs[1 + 2 * NA_NKB:]
    rb = pl.program_id(1)
    kb0 = jnp.clip(rb - 1, 0, ROWS // NA_QROWS - NA_NKB)
    tq = q_ref.shape[0]
    lo = _lane_lo(tq)
    hi = jnp.logical_not(lo)
    lo_w = _lane_lo(GRID_W)
    vc = vct_ref[0, 0].T.astype(BF)
    ones = jnp.ones((tq, LANES), BF)

    tiles = []
    for i in range(NA_QROWS):
        qr = rb * NA_QROWS + i
        rs = jnp.clip(qr - NA_KH // 2, 0, ROWS - NA_KH)
        for j in range(NA_NKB):
            for jj in range(NA_PAIRS):
                kra = (kb0 + j) * NA_QROWS + 2 * jj
                krb = kra + 1
                idx = jnp.clip(kra - qr + NA_KH, 0, 2 * NA_KH - 1)
                va = jnp.logical_and(kra >= rs, kra < rs + NA_KH).astype(jnp.int32)
                vb = jnp.logical_and(krb >= rs, krb < rs + NA_KH).astype(jnp.int32)
                tiles.append((idx, jnp.where(lo_w, va, vb) > 0))

    def scores(h):
        rows = slice((h // 2) * LANES, (h // 2 + 1) * LANES)
        q2 = q_ref[:, rows]
        qsel = jnp.where(lo if h % 2 == 0 else hi, q2, jnp.zeros_like(q2))
        s_lat = [_dot(qsel, r[rows, :]) for r in kt_refs]
        out_rows = []
        for i in range(NA_QROWS):
            rsl = slice(i * GRID_W, (i + 1) * GRID_W)
            row = []
            for j in range(NA_NKB):
                for jj in range(NA_PAIRS):
                    idx, valid = tiles[(i * NA_NKB + j) * NA_PAIRS + jj]
                    t = s_lat[j][rsl, jj * LANES:(jj + 1) * LANES] + p_ref[0, h, idx]
                    row.append(jnp.where(valid, t, NEG))
            out_rows.append(jnp.concatenate(row, axis=1))
        return jnp.concatenate(out_rows, axis=0), _dot(qsel, kct_ref[0, 0, rows, :].astype(BF))

    def weighted_values(h, s, s_ctx):
        rows = slice((h // 2) * LANES, (h // 2 + 1) * LANES)
        mine = lo if h % 2 == 0 else hi
        m = jnp.maximum(s.max(axis=-1, keepdims=True), s_ctx.max(axis=-1, keepdims=True))
        e = jnp.exp(s - m).astype(BF)
        e_ctx = jnp.exp(s_ctx - m).astype(BF)
        pv = _dot(e_ctx, jnp.where(mine, vc[:, rows], ones))
        for j in range(NA_NKB):
            vj = jnp.where(mine, v_refs[j][:, rows], ones)
            pv = pv + _dot(e[:, j * NA_KBLK:(j + 1) * NA_KBLK], vj)
        return pv / pltpu.roll(pv, HD, 1)

    nxt = scores(0)
    o2 = None
    for h in range(NH):
        cur = nxt
        if h + 1 < NH:
            nxt = scores(h + 1)
        pv = weighted_values(h, *cur)
        if h % 2 == 0:
            o2 = pv
        else:
            o_ref[:, (h // 2) * LANES:(h // 2 + 1) * LANES] = jnp.where(lo, o2, pv).astype(o_ref.dtype)


def _attn_na_sample(q, kt, v, cache_kt, cache_vt, i, ptab):
    nrb = ROWS // NA_QROWS

    def kb(b, r, j):
        return b * nrb + jnp.clip(r - 1, 0, nrb - NA_NKB) + j

    in_specs = [pl.BlockSpec((NA_KBLK, D), lambda b, r: (b * nrb + r, 0))]
    in_specs += [pl.BlockSpec((D, NA_KBLK), functools.partial(lambda b, r, j: (0, kb(b, r, j)), j=j))
                 for j in range(NA_NKB)]
    in_specs += [pl.BlockSpec((NA_KBLK, D), functools.partial(lambda b, r, j: (kb(b, r, j), 0), j=j))
                 for j in range(NA_NKB)]
    in_specs += [
        pl.BlockSpec((1, 1, D, PAST), lambda b, r: (b, i, 0, 0)),
        pl.BlockSpec((1, 1, D, PAST), lambda b, r: (b, i, 0, 0)),
        pl.BlockSpec((1, NH, 2 * NA_KH, GRID_W, LANES), lambda b, r: (i, 0, 0, 0, 0),
                     pipeline_mode=pl.Buffered(1)),
    ]
    return pl.pallas_call(
        _attn_na_sample_kernel,
        grid=(DEC_BATCH, nrb),
        in_specs=in_specs,
        out_specs=pl.BlockSpec((NA_KBLK, D), lambda b, r: (b * nrb + r, 0)),
        out_shape=jax.ShapeDtypeStruct((NS, D), BF),
        compiler_params=_params(("arbitrary", "arbitrary")),
        name="attn_na_sample",
    )(q, *([kt] * NA_NKB), *([v] * NA_NKB), cache_kt, cache_vt, ptab)


N_DC = 2 * NA_KW - 1
N_DR = 2 * NA_KH - 1
DR_GROUP = 5


def _na_bias_kernel(rpb_ref, o_ref):
    base = (pl.program_id(0) * NH + pl.program_id(1)) * (N_DR * N_DC)
    qc = lax.broadcasted_iota(jnp.int32, (GRID_W, LANES), 0)
    lane = lax.broadcasted_iota(jnp.int32, (GRID_W, LANES), 1)
    kc = lane % GRID_W
    diff = kc - qc + (NA_KW - 1)
    c_start = jnp.clip(qc - NA_KW // 2, 0, GRID_W - NA_KW)
    ok = jnp.logical_and(kc >= c_start, kc < c_start + NA_KW)
    lo = lane < GRID_W
    neg = jnp.full((GRID_W, LANES), NEG, F32)
    prev = neg
    for g0 in range(0, N_DR, DR_GROUP):
        accs = [jnp.zeros((GRID_W, LANES), F32) for _ in range(DR_GROUP)]
        for p in range(N_DC):
            hit = diff == p
            for k in range(DR_GROUP):
                accs[k] = jnp.where(hit, rpb_ref[base + (g0 + k) * N_DC + p], accs[k])
        for k in range(DR_GROUP):
            cur = jnp.where(ok, accs[k], neg)
            o_ref[0, 0, g0 + k] = jnp.where(lo, prev, cur)
            prev = cur
    o_ref[0, 0, N_DR] = jnp.where(lo, prev, neg)


def _na_bias_tables(rpb):
    n_a = rpb.shape[0]
    return pl.pallas_call(
        _na_bias_kernel,
        grid=(n_a, NH),
        in_specs=[pl.BlockSpec(memory_space=pltpu.SMEM)],
        out_specs=pl.BlockSpec((1, 1, N_DR + 1, GRID_W, LANES), lambda a, h: (a, h, 0, 0, 0)),
        out_shape=jax.ShapeDtypeStruct((n_a, NH, N_DR + 1, GRID_W, LANES), F32),
        compiler_params=_params(("arbitrary", "arbitrary")),
        name="na_bias",
    )(rpb.reshape(-1))


HALO_O = 16


def _mix_ffn_kernel(*refs, halo, tiles_per_seq):
    if halo:
        (o_ref, op_ref, on_ref, x_ref, xp_ref, xn_ref, mod_ref, gmix_ref, wo_ref,
         gpre_ref, gpost_ref, wup_ref, cw_ref, cb_ref, wdn_ref, out_ref, act_scr) = refs
    else:
        (o_ref, x_ref, mod_ref, gmix_ref, wo_ref,
         gpre_ref, gpost_ref, wup_ref, cw_ref, cb_ref, wdn_ref, out_ref, act_scr) = refs
    tm = x_ref.shape[0]
    m = mod_ref[0]
    gate_mix = m[:, 2 * D:3 * D]
    sh = m[:, 3 * D:4 * D]
    sc = m[:, 4 * D:5 * D]
    gate = m[:, 5 * D:6 * D]
    zeros = jnp.zeros((SUBLANES, D), F32)

    if halo:
        o_ext = jnp.concatenate([op_ref[...].astype(F32)[HALO_O - SUBLANES:], o_ref[...].astype(F32),
                                 on_ref[...].astype(F32)[:SUBLANES]], axis=0).astype(BF)
        x_ext = jnp.concatenate([xp_ref[...], x_ref[...], xn_ref[...]], axis=0)
    else:
        o_ext = o_ref[...]
        x_ext = x_ref[...]
    x_ext = x_ext + gate_mix * _rms(_dot(o_ext, wo_ref[0]), gmix_ref[...])
    h = _rms(x_ext, gpre_ref[...]) * (1.0 + sc) + sh
    if halo:
        t = pl.program_id(0) % tiles_per_seq
        x = x_ext[SUBLANES:SUBLANES + tm]
        hp = jnp.where(t == 0, zeros, h[:SUBLANES])
        hn = jnp.where(t == tiles_per_seq - 1, zeros, h[SUBLANES + tm:])
        hext = jnp.concatenate([hp, h[SUBLANES:SUBLANES + tm], hn], axis=0).astype(BF)
        starts = [SUBLANES]
        span = tm
    else:
        x = x_ext
        parts = [zeros]
        for s in range(tm // SEQ):
            parts += [h[s * SEQ:(s + 1) * SEQ], zeros]
        hext = jnp.concatenate(parts, axis=0).astype(BF)
        starts = [SUBLANES + s * (SEQ + SUBLANES) for s in range(tm // SEQ)]
        span = SEQ
    mext = hext.shape[0]

    def token_rows(v):
        parts = [v[r0:r0 + span] for r0 in starts]
        return parts[0] if len(parts) == 1 else jnp.concatenate(parts, axis=0)

    def conv(u, c0):
        cols = slice(c0, c0 + FF_CHUNK)
        prev = token_rows(pltpu.roll(u, 1, 0))
        nxt = token_rows(pltpu.roll(u, mext - 1, 0))
        return (prev * cw_ref[0, 0:1, cols] + token_rows(u) * cw_ref[0, 1:2, cols]
                + nxt * cw_ref[0, 2:3, cols] + cb_ref[0, :, cols])

    for j in range(N_CHUNK):
        ca = j * FF_CHUNK
        cg = DFF + j * FF_CHUNK
        a = conv(_dot(hext, wup_ref[0, :, ca:ca + FF_CHUNK]), ca)
        gt = conv(_dot(hext, wup_ref[0, :, cg:cg + FF_CHUNK]), cg)
        act_scr[:, ca:ca + FF_CHUNK] = (a * jax.nn.sigmoid(a) * gt).astype(BF)
    y = _dot(act_scr[...], wdn_ref[0])
    out_ref[...] = x + gate * _rms(y, gpost_ref[...])


def _mix_ffn(o, x, modl, mod_map, gmix, w_o, i, gpre, gpost, wup, cw, cb, wdn, l, *, tm, halo):
    n = x.shape[0]
    tps = DEC_SEQ // tm

    def prev_blk(rows):
        r = tm // rows
        return lambda t: (jnp.maximum(t * r - 1, 0), 0)

    def next_blk(rows):
        r = tm // rows
        return lambda t: (jnp.minimum((t + 1) * r, n // rows - 1), 0)

    in_specs = [pl.BlockSpec((tm, D), lambda t: (t, 0))]
    args = [o]
    if halo:
        in_specs += [pl.BlockSpec((HALO_O, D), prev_blk(HALO_O)), pl.BlockSpec((HALO_O, D), next_blk(HALO_O))]
        args += [o, o]
    in_specs.append(pl.BlockSpec((tm, D), lambda t: (t, 0)))
    args.append(x)
    if halo:
        in_specs += [pl.BlockSpec((SUBLANES, D), prev_blk(SUBLANES)),
                     pl.BlockSpec((SUBLANES, D), next_blk(SUBLANES))]
        args += [x, x]
    const = dict(pipeline_mode=pl.Buffered(1))
    in_specs += [
        pl.BlockSpec((1, 1, 6 * D), mod_map),
        pl.BlockSpec((1, D), lambda t: (0, 0)),
        pl.BlockSpec((1, D, D), lambda t: (i, 0, 0), **const),
        pl.BlockSpec((1, D), lambda t: (0, 0)),
        pl.BlockSpec((1, D), lambda t: (0, 0)),
        pl.BlockSpec((1, D, 2 * DFF), lambda t: (l, 0, 0), **const),
        pl.BlockSpec((1, 3, 2 * DFF), lambda t: (l, 0, 0)),
        pl.BlockSpec((1, 1, 2 * DFF), lambda t: (l, 0, 0)),
        pl.BlockSpec((1, DFF, D), lambda t: (l, 0, 0), **const),
    ]
    args += [modl, gmix.reshape(1, D), w_o, gpre.reshape(1, D), gpost.reshape(1, D), wup, cw, cb, wdn]
    return pl.pallas_call(
        functools.partial(_mix_ffn_kernel, halo=halo, tiles_per_seq=tps),
        grid=(n // tm,),
        in_specs=in_specs,
        out_specs=pl.BlockSpec((tm, D), lambda t: (t, 0)),
        out_shape=jax.ShapeDtypeStruct((n, D), F32),
        scratch_shapes=[pltpu.VMEM((tm, DFF), BF)],
        compiler_params=_params(("arbitrary",)),
        name="mix_ffn",
    )(*args)


def _rope_tables():
    t = jnp.arange(DEC_SEQ)
    rows = (t // GRID_W).astype(F32)
    cols = (t % GRID_W).astype(F32)
    quarter = HD // 4
    freqs = ROPE_BASE ** (-jnp.arange(quarter, dtype=F32) / quarter)
    lane = jnp.arange(LANES)
    d = lane % HD
    e = d % (HD // 2)
    is_x2 = (e >= quarter)[None, :]
    pos = jnp.where((d // (HD // 2) == 0)[None, :], rows[:, None], cols[:, None])
    ang = pos * freqs[e % quarter][None, :]
    cos = jnp.cos(ang)
    sin = jnp.sin(ang)
    return cos, jnp.where(is_x2, 0.0, -sin), jnp.where(is_x2, sin, 0.0)


def _block_diag_ones():
    r = jnp.arange(256)
    return (r[:, None] // HD == r[None, :] // HD).astype(BF)


def _cache_t(cache):
    b, n, p, h, d = cache.shape
    return cache.transpose(0, 1, 3, 4, 2).reshape(b, n, h * d, p)


def _cache_out(kt, heads):
    b, n, _, s = kt.shape
    return kt.reshape(b, n, heads, HD, s).transpose(0, 1, 4, 2, 3)


def kernel(x_prompt, x_sample, cache_na_k, cache_na_v, cache_gqa_k, cache_gqa_v, c, c_ctx,
           ada_w, ada_b, norm_mix_pre, norm_mix_post, norm_ffn_pre, norm_ffn_post,
           na_w_qkv, na_w_o, na_rpb, gqa_w_qkv, gqa_w_o, gqa_q_norm, gqa_k_norm,
           ffn_w_up, ffn_conv_w, ffn_conv_b, ffn_w_down):
    n_a = cache_na_k.shape[1]
    n_b = cache_gqa_k.shape[1]
    cond8 = jnp.concatenate([c_ctx[None], c, jnp.zeros((SUBLANES - 1 - DEC_BATCH, D), F32)], axis=0)
    mod = _adaln(cond8, ada_w, ada_b)

    tm_p, tm_s = 512, 512
    tm_fp, tm_fs = 512, 512
    map_p = lambda t: (0, 0, 0)
    map_s = lambda t: (1 + t // (DEC_SEQ // tm_s), 0, 0)
    map_fs = lambda t: (1 + t // (DEC_SEQ // tm_fs), 0, 0)

    cna_kt, cna_vt = _cache_t(cache_na_k), _cache_t(cache_na_v)
    cg_kt, cg_vt = _cache_t(cache_gqa_k), _cache_t(cache_gqa_v)
    rope_tabs = _rope_tables()
    bd = _block_diag_ones()
    ptabs = _na_bias_tables(na_rpb)
    kvg = NKV * HD

    w_na, w_gq = na_w_qkv.astype(BF), gqa_w_qkv.astype(BF)
    wo_na, wo_gq = na_w_o.astype(BF), gqa_w_o.astype(BF)
    wup, wdn = ffn_w_up.astype(BF), ffn_w_down.astype(BF)
    cb = ffn_conv_b.reshape(DEPTH, 1, 2 * DFF)

    xp = x_prompt.reshape(NP, D)
    xs = x_sample.reshape(NS, D)
    na_kv = None
    gq_kv = None
    for l in range(DEPTH):
        i = l // 2
        modl = mod[l].reshape(SUBLANES, 1, 6 * D)
        if l % 2 == 0:
            w_o = wo_na
            qp, vp, kt, vt = _qkv_prompt(xp, modl, norm_mix_pre[l], w_na, i, n_a, na_kv,
                                         gqa=False, tm=tm_p)
            na_kv = (kt, vt)
            qs, kts, vs = _qkv_sample_na(xs, modl, map_s, norm_mix_pre[l], w_na, i, tm=tm_s)
            op = _attn_prompt(qp, kt, vp, i, 1)
            os_ = _attn_na_sample(qs, kts, vs, cna_kt, cna_vt, i, ptabs)
        else:
            w_o = wo_gq
            qg = (jnp.tile(gqa_q_norm[i], NH) * 0.125).reshape(1, D)
            kg = jnp.tile(gqa_k_norm[i], NKV)
            qkg = jnp.concatenate([qg[0], kg]).reshape(1, D + kvg)
            kg2 = jnp.broadcast_to(kg[:, None], (kvg, SEQ))
            qp, vp, kt, vt = _qkv_prompt(xp, modl, norm_mix_pre[l], w_gq, i, n_b, gq_kv,
                                         gqa=True, tm=tm_p, extra=(bd, qg, kg2))
            gq_kv = (kt, vt)
            qs, kts, vs = _qkv_sample_gqa(xs, modl, map_s, norm_mix_pre[l], w_gq, i, bd, qkg,
                                          rope_tabs, tm=tm_s)
            op = _attn_prompt(qp, kt, vp, i, NH // NKV)
            os_ = _attn_gqa_sample(qs, kts, vs, cg_kt, cg_vt, i)
        xp = _mix_ffn(op, xp, modl, map_p, norm_mix_post[l], w_o, i, norm_ffn_pre[l], norm_ffn_post[l],
                      wup, ffn_conv_w, cb, wdn, l, tm=tm_fp, halo=False)
        xs = _mix_ffn(os_, xs, modl, map_fs, norm_mix_post[l], w_o, i, norm_ffn_pre[l], norm_ffn_post[l],
                      wup, ffn_conv_w, cb, wdn, l, tm=tm_fs, halo=True)

    return (xp.reshape(BATCH, SEQ, D), xs.reshape(DEC_BATCH, DEC_SEQ, D),
            _cache_out(na_kv[0], NH), _cache_out(na_kv[1], NH),
            _cache_out(gq_kv[0], NKV), _cache_out(gq_kv[1], NKV))
```

```python
import functools

import jax
import jax.numpy as jnp
from jax import lax
from jax.experimental import pallas as pl
from jax.experimental.pallas import tpu as pltpu

D = 1024
HD = 64
NH = 16
NKV = 4
DFF = 2816
DEPTH = 4
GRID_W = 64
ROWS = 32
NA_KH = 8
NA_KW = 16
ROPE_BASE = 10000.0
EPS = 1e-6
BATCH, SEQ = 32, 256
DEC_BATCH, DEC_SEQ = 2, 2048
PAST = 256
NP = BATCH * SEQ
NS = DEC_BATCH * DEC_SEQ

BF = jnp.bfloat16
F32 = jnp.float32
NEG = -1e30
LANES = 128
SUBLANES = 8
FF_CHUNK = 256
N_CHUNK = DFF // FF_CHUNK
VMEM_LIMIT = 50 * 1024 * 1024


def _params(sem):
    return pltpu.CompilerParams(dimension_semantics=sem, vmem_limit_bytes=VMEM_LIMIT)


def _rms(x, g):
    return x * lax.rsqrt(jnp.mean(x * x, axis=-1, keepdims=True) + EPS) * g


def _dot(a, b):
    return jnp.dot(a, b, preferred_element_type=F32)


def _dot_tn_nt(a, b):
    return lax.dot_general(a, b, (((0,), (1,)), ((), ())), preferred_element_type=F32)


def _qkv_weight_specs(i, gqa):
    if gqa:
        kvw = NKV * HD
        return [pl.BlockSpec((1, D, D), lambda t: (i, 0, 0)),
                pl.BlockSpec((1, D, kvw), lambda t: (i, 0, D // kvw)),
                pl.BlockSpec((1, D, kvw), lambda t: (i, 0, D // kvw + 1))]
    return [pl.BlockSpec((1, D, D), functools.partial(lambda t, c: (i, 0, c), c=c)) for c in range(3)]


def _mod_kernel(cond_ref, w_ref, b_ref, o_ref):
    s = cond_ref[...]
    s = s * jax.nn.sigmoid(s)
    o_ref[0] = _dot(s.astype(BF), w_ref[0].astype(BF)) + b_ref[0]


def _adaln(cond8, ada_w, ada_b):
    tn = 1536
    return pl.pallas_call(
        _mod_kernel,
        grid=(DEPTH, 6 * D // tn),
        in_specs=[
            pl.BlockSpec((SUBLANES, D), lambda l, n: (0, 0)),
            pl.BlockSpec((1, D, tn), lambda l, n: (l, 0, n)),
            pl.BlockSpec((1, 1, tn), lambda l, n: (l, 0, n)),
        ],
        out_specs=pl.BlockSpec((1, SUBLANES, tn), lambda l, n: (l, 0, n)),
        out_shape=jax.ShapeDtypeStruct((DEPTH, SUBLANES, 6 * D), F32),
        compiler_params=_params(("arbitrary", "arbitrary")),
        name="adaln",
    )(cond8, ada_w, ada_b.reshape(DEPTH, 1, 6 * D))


def _prenorm(x_ref, mod_ref, g_ref):
    m = mod_ref[0]
    return (_rms(x_ref[...], g_ref[...]) * (1.0 + m[:, D:2 * D]) + m[:, 0:D]).astype(BF)


def _head_rms_lanes(blk, bd_ref):
    ss = _dot((blk * blk).astype(BF), bd_ref[...])
    return blk * lax.rsqrt(ss * (1.0 / HD) + EPS)


def _qkv_prompt_kernel(*refs, gqa, slot):
    if gqa:
        (x_ref, mod_ref, g_ref, wq_ref, wk_ref, wv_ref, bd_ref, qg_ref, kg_ref,
         q_ref, v_ref, kt_ref, vt_ref) = refs
    else:
        x_ref, mod_ref, g_ref, wq_ref, wk_ref, wv_ref, q_ref, v_ref, kt_ref, vt_ref = refs
    hb = _prenorm(x_ref, mod_ref, g_ref)
    q = _dot(hb, wq_ref[0])
    v = _dot(hb, wv_ref[0])
    kt = _dot_tn_nt(wk_ref[0], hb)
    if gqa:
        for t in range(D // 256):
            sl = slice(t * 256, (t + 1) * 256)
            q_ref[:, sl] = (_head_rms_lanes(q[:, sl], bd_ref) * qg_ref[:, sl]).astype(q_ref.dtype)
        heads = []
        for h in range(NKV):
            blk = kt[h * HD:(h + 1) * HD, :]
            heads.append(blk * lax.rsqrt(jnp.mean(blk * blk, axis=0, keepdims=True) + EPS))
        kt = jnp.concatenate(heads, axis=0)
    else:
        q_ref[...] = (q * 0.125).astype(q_ref.dtype)
    v_ref[...] = v.astype(v_ref.dtype)
    for s in range(x_ref.shape[0] // SEQ):
        sl = slice(s * SEQ, (s + 1) * SEQ)
        kts = kt[:, sl]
        if gqa:
            kts = kts * kg_ref[...]
        kt_ref[s, slot] = kts
        vt_ref[s, slot] = v[sl, :].T
        for other in range(kt_ref.shape[1]):
            if other != slot:
                kt_ref[s, other] = jnp.zeros_like(kts)
                vt_ref[s, other] = jnp.zeros_like(kts)


def _qkv_prompt(x, modl, g, w, i, n_layers, prev, *, gqa, tm, extra=()):
    kvw = NKV * HD if gqa else D
    in_specs = [
        pl.BlockSpec((tm, D), lambda t: (t, 0)),
        pl.BlockSpec((1, 1, 6 * D), lambda t: (0, 0, 0)),
        pl.BlockSpec((1, D), lambda t: (0, 0)),
    ] + _qkv_weight_specs(i, gqa)
    args = [x, modl, g.reshape(1, D), w, w, w]
    if gqa:
        in_specs += [pl.BlockSpec((256, 256), lambda t: (0, 0)),
                     pl.BlockSpec((1, D), lambda t: (0, 0)),
                     pl.BlockSpec((kvw, SEQ), lambda t: (0, 0))]
        args += list(extra)
    n_in = len(args)
    aliases = {}
    if prev is None:
        kv_spec = pl.BlockSpec((tm // SEQ, n_layers, kvw, SEQ), lambda t: (t, 0, 0, 0))
        slot = i
    else:
        in_specs += [pl.BlockSpec(memory_space=pl.ANY), pl.BlockSpec(memory_space=pl.ANY)]
        args += list(prev)
        aliases = {n_in: 2, n_in + 1: 3}
        kv_spec = pl.BlockSpec((tm // SEQ, 1, kvw, SEQ), lambda t: (t, i, 0, 0))
        slot = 0
    kv_shape = jax.ShapeDtypeStruct((BATCH, n_layers, kvw, SEQ), F32)

    def body(*refs):
        _qkv_prompt_kernel(*(refs[:n_in] + refs[len(args):]), gqa=gqa, slot=slot)

    return pl.pallas_call(
        body,
        grid=(NP // tm,),
        in_specs=in_specs,
        out_specs=[
            pl.BlockSpec((tm, D), lambda t: (t, 0)),
            pl.BlockSpec((tm, kvw), lambda t: (t, 0)),
            kv_spec, kv_spec,
        ],
        out_shape=[
            jax.ShapeDtypeStruct((NP, D), BF),
            jax.ShapeDtypeStruct((NP, kvw), BF),
            kv_shape, kv_shape,
        ],
        input_output_aliases=aliases,
        compiler_params=_params(("arbitrary",)),
        name="qkv_prompt_gqa" if gqa else "qkv_prompt_na",
    )(*args)


def _qkv_sample_na_kernel(x_ref, mod_ref, g_ref, wq_ref, wk_ref, wv_ref, q_ref, kt_ref, v_ref):
    hb = _prenorm(x_ref, mod_ref, g_ref)
    q_ref[...] = (_dot(hb, wq_ref[0]) * 0.125).astype(q_ref.dtype)
    v_ref[...] = _dot(hb, wv_ref[0]).astype(v_ref.dtype)
    kt_ref[...] = _dot_tn_nt(wk_ref[0], hb).astype(kt_ref.dtype)


def _qkv_sample_na(x, modl, mod_map, g, w, i, *, tm):
    return pl.pallas_call(
        _qkv_sample_na_kernel,
        grid=(NS // tm,),
        in_specs=[
            pl.BlockSpec((tm, D), lambda t: (t, 0)),
            pl.BlockSpec((1, 1, 6 * D), mod_map),
            pl.BlockSpec((1, D), lambda t: (0, 0)),
        ] + _qkv_weight_specs(i, False),
        out_specs=[
            pl.BlockSpec((tm, D), lambda t: (t, 0)),
            pl.BlockSpec((D, tm), lambda t: (0, t)),
            pl.BlockSpec((tm, D), lambda t: (t, 0)),
        ],
        out_shape=[
            jax.ShapeDtypeStruct((NS, D), BF),
            jax.ShapeDtypeStruct((D, NS), BF),
            jax.ShapeDtypeStruct((NS, D), BF),
        ],
        compiler_params=_params(("arbitrary",)),
        name="qkv_sample_na",
    )(x, modl, g.reshape(1, D), w, w, w)


def _qkv_sample_gqa_kernel(x_ref, mod_ref, g_ref, w_ref, bd_ref, qkg_ref, cos_ref, s1_ref, s2_ref,
                           q_ref, kt_ref, v_ref):
    qkv = _dot(_prenorm(x_ref, mod_ref, g_ref), w_ref[0])
    nqk = D + NKV * HD
    for t in range(nqk // LANES):
        if t % 2 == 0:
            nrm = _head_rms_lanes(qkv[:, (t // 2) * 256:(t // 2 + 1) * 256], bd_ref)
        xt = nrm[:, (t % 2) * LANES:(t % 2 + 1) * LANES] * qkg_ref[:, t * LANES:(t + 1) * LANES]
        xt = (xt * cos_ref[...]
              + pltpu.roll(xt, LANES - 16, 1) * s1_ref[...]
              + pltpu.roll(xt, 16, 1) * s2_ref[...])
        if t < D // LANES:
            q_ref[:, t * LANES:(t + 1) * LANES] = xt.astype(q_ref.dtype)
        else:
            tk = t - D // LANES
            kt_ref[tk * LANES:(tk + 1) * LANES, :] = xt.T.astype(kt_ref.dtype)
    v_ref[...] = qkv[:, nqk:].astype(v_ref.dtype)


def _qkv_sample_gqa(x, modl, mod_map, g, w, i, bd, qkg, rope_tabs, *, tm):
    nq = w.shape[2]
    kvw = NKV * HD
    tps = DEC_SEQ // tm
    in_specs = [
        pl.BlockSpec((tm, D), lambda t: (t, 0)),
        pl.BlockSpec((1, 1, 6 * D), mod_map),
        pl.BlockSpec((1, D), lambda t: (0, 0)),
        pl.BlockSpec((1, D, nq), lambda t: (i, 0, 0)),
        pl.BlockSpec((256, 256), lambda t: (0, 0)),
        pl.BlockSpec((1, D + kvw), lambda t: (0, 0)),
    ]
    in_specs += [pl.BlockSpec((tm, LANES), lambda t: (t % tps, 0)) for _ in rope_tabs]
    return pl.pallas_call(
        _qkv_sample_gqa_kernel,
        grid=(NS // tm,),
        in_specs=in_specs,
        out_specs=[
            pl.BlockSpec((tm, D), lambda t: (t, 0)),
            pl.BlockSpec((kvw, tm), lambda t: (0, t)),
            pl.BlockSpec((tm, kvw), lambda t: (t, 0)),
        ],
        out_shape=[
            jax.ShapeDtypeStruct((NS, D), BF),
            jax.ShapeDtypeStruct((kvw, NS), BF),
            jax.ShapeDtypeStruct((NS, kvw), BF),
        ],
        compiler_params=_params(("arbitrary",)),
        name="qkv_sample_gqa",
    )(x, modl, g.reshape(1, D), w, bd, qkg, *rope_tabs)


def _lane_lo(rows):
    return lax.broadcasted_iota(jnp.int32, (rows, LANES), 1) < HD


def _pipelined(n, first, second):
    nxt = first(0)
    for i in range(n):
        cur = nxt
        if i + 1 < n:
            nxt = first(i + 1)
        second(i, cur)


def _attn_grouped(problems, group, stack, *, ones_sum):
    tq = problems[0][0].shape[0]
    lo = _lane_lo(tq)
    lo_s = _lane_lo(stack * tq)
    pending = {}
    per = NH // stack

    def scores(n):
        q_ref, _, key_tiles, _ = problems[n // per]
        u = n % per
        g = u * stack // group
        parts = []
        for h in range(u * stack, (u + 1) * stack):
            q2 = q_ref[:, (h // 2) * LANES:(h // 2 + 1) * LANES]
            if h % 2 != g % 2:
                q2 = pltpu.roll(q2.astype(F32), HD, 1).astype(BF)
            parts.append(q2)
        qs = parts[0] if stack == 1 else jnp.concatenate(parts, axis=0)
        mine = lo_s if g % 2 == 0 else jnp.logical_not(lo_s)
        qsel = jnp.where(mine, qs, jnp.zeros_like(qs))
        return [_dot(qsel, kt) for kt in key_tiles(g)]

    def finish(n, ss):
        _, o_ref, _, value_tiles = problems[n // per]
        u = n % per
        g = u * stack // group
        m = ss[0].max(axis=-1, keepdims=True)
        for s in ss[1:]:
            m = jnp.maximum(m, s.max(axis=-1, keepdims=True))
        es = [jnp.exp(s - m) for s in ss]
        pv = None
        for e, v in zip(es, value_tiles(g)):
            if ones_sum:
                mine = _lane_lo(v.shape[0])
                if g % 2:
                    mine = jnp.logical_not(mine)
                v = jnp.where(mine, v, jnp.ones_like(v))
            d = _dot(e.astype(BF), v)
            pv = d if pv is None else pv + d
        if ones_sum:
            pv = pv / pltpu.roll(pv, HD, 1)
        else:
            l = es[0].sum(axis=-1, keepdims=True)
            for e in es[1:]:
                l = l + e.sum(axis=-1, keepdims=True)
            pv = pv / l
        for j in range(stack):
            h = u * stack + j
            pj = pv[j * tq:(j + 1) * tq]
            if h % 2 != g % 2:
                pj = pltpu.roll(pj, HD, 1)
            if h % 2 == 0:
                pending[h // 2] = pj
            else:
                sl = slice((h // 2) * LANES, (h // 2 + 1) * LANES)
                o_ref[:, sl] = jnp.where(lo, pending.pop(h // 2), pj).astype(o_ref.dtype)

    _pipelined(per * len(problems), scores, finish)


ATTN_P_SEQS = 4


def _attn_prompt_kernel(q_ref, kt_ref, v_ref, o_ref, *, group):
    def tile(g):
        return slice((g // 2) * LANES, (g // 2 + 1) * LANES)

    def problem(s):
        rows = pl.ds(s * SEQ, SEQ)
        return (q_ref.at[rows, :], o_ref.at[rows, :],
                lambda g: [kt_ref[s, 0, tile(g), :].astype(BF)],
                lambda g: [v_ref[pl.ds(s * SEQ, SEQ), tile(g)]])
    _attn_grouped([problem(s) for s in range(ATTN_P_SEQS)], group, group, ones_sum=False)


def _attn_prompt(q, kt_all, v, i, group):
    kvw = v.shape[1]
    rows = ATTN_P_SEQS * SEQ
    return pl.pallas_call(
        functools.partial(_attn_prompt_kernel, group=group),
        grid=(BATCH // ATTN_P_SEQS,),
        in_specs=[
            pl.BlockSpec((rows, D), lambda b: (b, 0)),
            pl.BlockSpec((ATTN_P_SEQS, 1, kvw, SEQ), lambda b: (b, i, 0, 0)),
            pl.BlockSpec((rows, kvw), lambda b: (b, 0)),
        ],
        out_specs=pl.BlockSpec((rows, D), lambda b: (b, 0)),
        out_shape=jax.ShapeDtypeStruct((NP, D), BF),
        compiler_params=_params(("arbitrary",)),
        name="attn_prompt",
    )(q, kt_all, v)


def _attn_gqa_sample_kernel(q_ref, kt_ref, v_ref, kct_ref, vct_ref, o_ref):
    vc = vct_ref[0, 0].T.astype(BF)

    def tile(g):
        return slice((g // 2) * LANES, (g // 2 + 1) * LANES)
    _attn_grouped([(q_ref, o_ref,
                    lambda g: [kt_ref[tile(g), :], kct_ref[0, 0, tile(g), :].astype(BF)],
                    lambda g: [v_ref[:, tile(g)], vc[:, tile(g)]])],
                  NH // NKV, NH // NKV, ones_sum=True)


GQA_TQ = 256


def _attn_gqa_sample(q, kt, v, cache_kt, cache_vt, i):
    nq = DEC_SEQ // GQA_TQ
    kvw = NKV * HD
    return pl.pallas_call(
        _attn_gqa_sample_kernel,
        grid=(DEC_BATCH, nq),
        in_specs=[
            pl.BlockSpec((GQA_TQ, D), lambda b, t: (b * nq + t, 0)),
            pl.BlockSpec((kvw, DEC_SEQ), lambda b, t: (0, b)),
            pl.BlockSpec((DEC_SEQ, kvw), lambda b, t: (b, 0)),
            pl.BlockSpec((1, 1, kvw, PAST), lambda b, t: (b, i, 0, 0)),
            pl.BlockSpec((1, 1, kvw, PAST), lambda b, t: (b, i, 0, 0)),
        ],
        out_specs=pl.BlockSpec((GQA_TQ, D), lambda b, t: (b * nq + t, 0)),
        out_shape=jax.ShapeDtypeStruct((NS, D), BF),
        compiler_params=_params(("arbitrary", "arbitrary")),
        name="attn_gqa_sample",
    )(q, kt, v, cache_kt, cache_vt)


NA_QROWS = 4
NA_KBLK = NA_QROWS * GRID_W
NA_NKB = 3
NA_PAIRS = NA_QROWS // 2


def _attn_na_sample_kernel(*refs):
    q_ref = refs[0]
    kt_refs = refs[1:1 + NA_NKB]
    v_refs = refs[1 + NA_NKB:1 + 2 * NA_NKB]
    kct_ref, vct_ref, p_ref, o_ref = refs[1 + 2 * NA_NKB:]
    rb = pl.program_id(1)
    kb0 = jnp.clip(rb - 1, 0, ROWS // NA_QROWS - NA_NKB)
    tq = q_ref.shape[0]
    lo = _lane_lo(tq)
    hi = jnp.logical_not(lo)
    lo_w = _lane_lo(GRID_W)
    vc = vct_ref[0, 0].T.astype(BF)
    ones = jnp.ones((tq, LANES), BF)

    tiles = []
    for i in range(NA_QROWS):
        qr = rb * NA_QROWS + i
        rs = jnp.clip(qr - NA_KH // 2, 0, ROWS - NA_KH)
        for j in range(NA_NKB):
            for jj in range(NA_PAIRS):
                kra = (kb0 + j) * NA_QROWS + 2 * jj
                krb = kra + 1
                idx = jnp.clip(kra - qr + NA_KH, 0, 2 * NA_KH - 1)
                va = jnp.logical_and(kra >= rs, kra < rs + NA_KH).astype(jnp.int32)
                vb = jnp.logical_and(krb >= rs, krb < rs + NA_KH).astype(jnp.int32)
                tiles.append((idx, jnp.where(lo_w, va, vb) > 0))

    def scores(h):
        rows = slice((h // 2) * LANES, (h // 2 + 1) * LANES)
        q2 = q_ref[:, rows]
        qsel = jnp.where(lo if h % 2 == 0 else hi, q2, jnp.zeros_like(q2))
        s_lat = [_dot(qsel, r[rows, :]) for r in kt_refs]
        out_rows = []
        for i in range(NA_QROWS):
            rsl = slice(i * GRID_W, (i + 1) * GRID_W)
            row = []
            for j in range(NA_NKB):
                for jj in range(NA_PAIRS):
                    idx, valid = tiles[(i * NA_NKB + j) * NA_PAIRS + jj]
                    t = s_lat[j][rsl, jj * LANES:(jj + 1) * LANES] + p_ref[0, h, idx]
                    row.append(jnp.where(valid, t, NEG))
            out_rows.append(jnp.concatenate(row, axis=1))
        return jnp.concatenate(out_rows, axis=0), _dot(qsel, kct_ref[0, 0, rows, :].astype(BF))

    def weighted_values(h, s, s_ctx):
        rows = slice((h // 2) * LANES, (h // 2 + 1) * LANES)
        mine = lo if h % 2 == 0 else hi
        m = jnp.maximum(s.max(axis=-1, keepdims=True), s_ctx.max(axis=-1, keepdims=True))
        e = jnp.exp(s - m).astype(BF)
        e_ctx = jnp.exp(s_ctx - m).astype(BF)
        pv = _dot(e_ctx, jnp.where(mine, vc[:, rows], ones))
        for j in range(NA_NKB):
            vj = jnp.where(mine, v_refs[j][:, rows], ones)
            pv = pv + _dot(e[:, j * NA_KBLK:(j + 1) * NA_KBLK], vj)
        return pv / pltpu.roll(pv, HD, 1)

    nxt = scores(0)
    o2 = None
    for h in range(NH):
        cur = nxt
        if h + 1 < NH:
            nxt = scores(h + 1)
        pv = weighted_values(h, *cur)
        if h % 2 == 0:
            o2 = pv
        else:
            o_ref[:, (h // 2) * LANES:(h // 2 + 1) * LANES] = jnp.where(lo, o2, pv).astype(o_ref.dtype)


def _attn_na_sample(q, kt, v, cache_kt, cache_vt, i, ptab):
    nrb = ROWS // NA_QROWS

    def kb(b, r, j):
        return b * nrb + jnp.clip(r - 1, 0, nrb - NA_NKB) + j

    in_specs = [pl.BlockSpec((NA_KBLK, D), lambda b, r: (b * nrb + r, 0))]
    in_specs += [pl.BlockSpec((D, NA_KBLK), functools.partial(lambda b, r, j: (0, kb(b, r, j)), j=j))
                 for j in range(NA_NKB)]
    in_specs += [pl.BlockSpec((NA_KBLK, D), functools.partial(lambda b, r, j: (kb(b, r, j), 0), j=j))
                 for j in range(NA_NKB)]
    in_specs += [
        pl.BlockSpec((1, 1, D, PAST), lambda b, r: (b, i, 0, 0)),
        pl.BlockSpec((1, 1, D, PAST), lambda b, r: (b, i, 0, 0)),
        pl.BlockSpec((1, NH, 2 * NA_KH, GRID_W, LANES), lambda b, r: (i, 0, 0, 0, 0),
                     pipeline_mode=pl.Buffered(1)),
    ]
    return pl.pallas_call(
        _attn_na_sample_kernel,
        grid=(DEC_BATCH, nrb),
        in_specs=in_specs,
        out_specs=pl.BlockSpec((NA_KBLK, D), lambda b, r: (b * nrb + r, 0)),
        out_shape=jax.ShapeDtypeStruct((NS, D), BF),
        compiler_params=_params(("arbitrary", "arbitrary")),
        name="attn_na_sample",
    )(q, *([kt] * NA_NKB), *([v] * NA_NKB), cache_kt, cache_vt, ptab)


N_DC = 2 * NA_KW - 1
N_DR = 2 * NA_KH - 1
DR_GROUP = 5


def _na_bias_kernel(rpb_ref, o_ref):
    base = (pl.program_id(0) * NH + pl.program_id(1)) * (N_DR * N_DC)
    qc = lax.broadcasted_iota(jnp.int32, (GRID_W, LANES), 0)
    lane = lax.broadcasted_iota(jnp.int32, (GRID_W, LANES), 1)
    kc = lane % GRID_W
    diff = kc - qc + (NA_KW - 1)
    c_start = jnp.clip(qc - NA_KW // 2, 0, GRID_W - NA_KW)
    ok = jnp.logical_and(kc >= c_start, kc < c_start + NA_KW)
    lo = lane < GRID_W
    neg = jnp.full((GRID_W, LANES), NEG, F32)
    prev = neg
    for g0 in range(0, N_DR, DR_GROUP):
        accs = [jnp.zeros((GRID_W, LANES), F32) for _ in range(DR_GROUP)]
        for p in range(N_DC):
            hit = diff == p
            for k in range(DR_GROUP):
                accs[k] = jnp.where(hit, rpb_ref[base + (g0 + k) * N_DC + p], accs[k])
        for k in range(DR_GROUP):
            cur = jnp.where(ok, accs[k], neg)
            o_ref[0, 0, g0 + k] = jnp.where(lo, prev, cur)
            prev = cur
    o_ref[0, 0, N_DR] = jnp.where(lo, prev, neg)


def _na_bias_tables(rpb):
    n_a = rpb.shape[0]
    return pl.pallas_call(
        _na_bias_kernel,
        grid=(n_a, NH),
        in_specs=[pl.BlockSpec(memory_space=pltpu.SMEM)],
        out_specs=pl.BlockSpec((1, 1, N_DR + 1, GRID_W, LANES), lambda a, h: (a, h, 0, 0, 0)),
        out_shape=jax.ShapeDtypeStruct((n_a, NH, N_DR + 1, GRID_W, LANES), F32),
        compiler_params=_params(("arbitrary", "arbitrary")),
        name="na_bias",
    )(rpb.reshape(-1))


HALO_O = 16


def _mix_ffn_kernel(*refs, halo, tiles_per_seq):
    if halo:
        (o_ref, op_ref, on_ref, x_ref, xp_ref, xn_ref, mod_ref, gmix_ref, wo_ref,
         gpre_ref, gpost_ref, wup_ref, cw_ref, cb_ref, wdn_ref, out_ref, act_scr) = refs
    else:
        (o_ref, x_ref, mod_ref, gmix_ref, wo_ref,
         gpre_ref, gpost_ref, wup_ref, cw_ref, cb_ref, wdn_ref, out_ref, act_scr) = refs
    tm = x_ref.shape[0]
    m = mod_ref[0]
    gate_mix = m[:, 2 * D:3 * D]
    sh = m[:, 3 * D:4 * D]
    sc = m[:, 4 * D:5 * D]
    gate = m[:, 5 * D:6 * D]
    zeros = jnp.zeros((SUBLANES, D), F32)

    if halo:
        o_ext = jnp.concatenate([op_ref[...].astype(F32)[HALO_O - SUBLANES:], o_ref[...].astype(F32),
                                 on_ref[...].astype(F32)[:SUBLANES]], axis=0).astype(BF)
        x_ext = jnp.concatenate([xp_ref[...], x_ref[...], xn_ref[...]], axis=0)
    else:
        o_ext = o_ref[...]
        x_ext = x_ref[...]
    x_ext = x_ext + gate_mix * _rms(_dot(o_ext, wo_ref[0]), gmix_ref[...])
    h = _rms(x_ext, gpre_ref[...]) * (1.0 + sc) + sh
    if halo:
        t = pl.program_id(0) % tiles_per_seq
        x = x_ext[SUBLANES:SUBLANES + tm]
        hp = jnp.where(t == 0, zeros, h[:SUBLANES])
        hn = jnp.where(t == tiles_per_seq - 1, zeros, h[SUBLANES + tm:])
        hext = jnp.concatenate([hp, h[SUBLANES:SUBLANES + tm], hn], axis=0).astype(BF)
        starts = [SUBLANES]
        span = tm
    else:
        x = x_ext
        parts = [zeros]
        for s in range(tm // SEQ):
            parts += [h[s * SEQ:(s + 1) * SEQ], zeros]
        hext = jnp.concatenate(parts, axis=0).astype(BF)
        starts = [SUBLANES + s * (SEQ + SUBLANES) for s in range(tm // SEQ)]
        span = SEQ
    mext = hext.shape[0]

    def token_rows(v):
        parts = [v[r0:r0 + span] for r0 in starts]
        return parts[0] if len(parts) == 1 else jnp.concatenate(parts, axis=0)

    def conv(u, c0):
        cols = slice(c0, c0 + FF_CHUNK)
        prev = token_rows(pltpu.roll(u, 1, 0))
        nxt = token_rows(pltpu.roll(u, mext - 1, 0))
        return (prev * cw_ref[0, 0:1, cols] + token_rows(u) * cw_ref[0, 1:2, cols]
                + nxt * cw_ref[0, 2:3, cols] + cb_ref[0, :, cols])

    for j in range(N_CHUNK):
        ca = j * FF_CHUNK
        cg = DFF + j * FF_CHUNK
        a = conv(_dot(hext, wup_ref[0, :, ca:ca + FF_CHUNK]), ca)
        gt = conv(_dot(hext, wup_ref[0, :, cg:cg + FF_CHUNK]), cg)
        act_scr[:, ca:ca + FF_CHUNK] = (a * jax.nn.sigmoid(a) * gt).astype(BF)
    y = _dot(act_scr[...], wdn_ref[0])
    out_ref[...] = x + gate * _rms(y, gpost_ref[...])


def _mix_ffn(o, x, modl, mod_map, gmix, w_o, i, gpre, gpost, wup, cw, cb, wdn, l, *, tm, halo):
    n = x.shape[0]
    tps = DEC_SEQ // tm

    def prev_blk(rows):
        r = tm // rows
        return lambda t: (jnp.maximum(t * r - 1, 0), 0)

    def next_blk(rows):
        r = tm // rows
        return lambda t: (jnp.minimum((t + 1) * r, n // rows - 1), 0)

    in_specs = [pl.BlockSpec((tm, D), lambda t: (t, 0))]
    args = [o]
    if halo:
        in_specs += [pl.BlockSpec((HALO_O, D), prev_blk(HALO_O)), pl.BlockSpec((HALO_O, D), next_blk(HALO_O))]
        args += [o, o]
    in_specs.append(pl.BlockSpec((tm, D), lambda t: (t, 0)))
    args.append(x)
    if halo:
        in_specs += [pl.BlockSpec((SUBLANES, D), prev_blk(SUBLANES)),
                     pl.BlockSpec((SUBLANES, D), next_blk(SUBLANES))]
        args += [x, x]
    const = dict(pipeline_mode=pl.Buffered(1))
    in_specs += [
        pl.BlockSpec((1, 1, 6 * D), mod_map),
        pl.BlockSpec((1, D), lambda t: (0, 0)),
        pl.BlockSpec((1, D, D), lambda t: (i, 0, 0), **const),
        pl.BlockSpec((1, D), lambda t: (0, 0)),
        pl.BlockSpec((1, D), lambda t: (0, 0)),
        pl.BlockSpec((1, D, 2 * DFF), lambda t: (l, 0, 0), **const),
        pl.BlockSpec((1, 3, 2 * DFF), lambda t: (l, 0, 0)),
        pl.BlockSpec((1, 1, 2 * DFF), lambda t: (l, 0, 0)),
        pl.BlockSpec((1, DFF, D), lambda t: (l, 0, 0), **const),
    ]
    args += [modl, gmix.reshape(1, D), w_o, gpre.reshape(1, D), gpost.reshape(1, D), wup, cw, cb, wdn]
    return pl.pallas_call(
        functools.partial(_mix_ffn_kernel, halo=halo, tiles_per_seq=tps),
        grid=(n // tm,),
        in_specs=in_specs,
        out_specs=pl.BlockSpec((tm, D), lambda t: (t, 0)),
        out_shape=jax.ShapeDtypeStruct((n, D), F32),
        scratch_shapes=[pltpu.VMEM((tm, DFF), BF)],
        compiler_params=_params(("arbitrary",)),
        name="mix_ffn",
    )(*args)


def _rope_tables():
    t = jnp.arange(DEC_SEQ)
    rows = (t // GRID_W).astype(F32)
    cols = (t % GRID_W).astype(F32)
    quarter = HD // 4
    freqs = ROPE_BASE ** (-jnp.arange(quarter, dtype=F32) / quarter)
    lane = jnp.arange(LANES)
    d = lane % HD
    e = d % (HD // 2)
    is_x2 = (e >= quarter)[None, :]
    pos = jnp.where((d // (HD // 2) == 0)[None, :], rows[:, None], cols[:, None])
    ang = pos * freqs[e % quarter][None, :]
    cos = jnp.cos(ang)
    sin = jnp.sin(ang)
    return cos, jnp.where(is_x2, 0.0, -sin), jnp.where(is_x2, sin, 0.0)


def _block_diag_ones():
    r = jnp.arange(256)
    return (r[:, None] // HD == r[None, :] // HD).astype(BF)


def _cache_t(cache):
    b, n, p, h, d = cache.shape
    return cache.transpose(0, 1, 3, 4, 2).reshape(b, n, h * d, p)


def _cache_out(kt, heads):
    b, n, _, s = kt.shape
    return kt.reshape(b, n, heads, HD, s).transpose(0, 1, 4, 2, 3)


def kernel(x_prompt, x_sample, cache_na_k, cache_na_v, cache_gqa_k, cache_gqa_v, c, c_ctx,
           ada_w, ada_b, norm_mix_pre, norm_mix_post, norm_ffn_pre, norm_ffn_post,
           na_w_qkv, na_w_o, na_rpb, gqa_w_qkv, gqa_w_o, gqa_q_norm, gqa_k_norm,
           ffn_w_up, ffn_conv_w, ffn_conv_b, ffn_w_down):
    n_a = cache_na_k.shape[1]
    n_b = cache_gqa_k.shape[1]
    cond8 = jnp.concatenate([c_ctx[None], c, jnp.zeros((SUBLANES - 1 - DEC_BATCH, D), F32)], axis=0)
    mod = _adaln(cond8, ada_w, ada_b)

    tm_p, tm_s = 512, 512
    tm_fp, tm_fs = 512, 512
    map_p = lambda t: (0, 0, 0)
    map_s = lambda t: (1 + t // (DEC_SEQ // tm_s), 0, 0)
    map_fs = lambda t: (1 + t // (DEC_SEQ // tm_fs), 0, 0)

    cna_kt, cna_vt = _cache_t(cache_na_k), _cache_t(cache_na_v)
    cg_kt, cg_vt = _cache_t(cache_gqa_k), _cache_t(cache_gqa_v)
    rope_tabs = _rope_tables()
    bd = _block_diag_ones()
    ptabs = _na_bias_tables(na_rpb)
    kvg = NKV * HD

    w_na, w_gq = na_w_qkv.astype(BF), gqa_w_qkv.astype(BF)
    wo_na, wo_gq = na_w_o.astype(BF), gqa_w_o.astype(BF)
    wup, wdn = ffn_w_up.astype(BF), ffn_w_down.astype(BF)
    cb = ffn_conv_b.reshape(DEPTH, 1, 2 * DFF)

    xp = x_prompt.reshape(NP, D)
    xs = x_sample.reshape(NS, D)
    na_kv = None
    gq_kv = None
    for l in range(DEPTH):
        i = l // 2
        modl = mod[l].reshape(SUBLANES, 1, 6 * D)
        if l % 2 == 0:
            w_o = wo_na
            qp, vp, kt, vt = _qkv_prompt(xp, modl, norm_mix_pre[l], w_na, i, n_a, na_kv,
                                         gqa=False, tm=tm_p)
            na_kv = (kt, vt)
            qs, kts, vs = _qkv_sample_na(xs, modl, map_s, norm_mix_pre[l], w_na, i, tm=tm_s)
            op = _attn_prompt(qp, kt, vp, i, 1)
            os_ = _attn_na_sample(qs, kts, vs, cna_kt, cna_vt, i, ptabs)
        else:
            w_o = wo_gq
            qg = (jnp.tile(gqa_q_norm[i], NH) * 0.125).reshape(1, D)
            kg = jnp.tile(gqa_k_norm[i], NKV)
            qkg = jnp.concatenate([qg[0], kg]).reshape(1, D + kvg)
            kg2 = jnp.broadcast_to(kg[:, None], (kvg, SEQ))
            qp, vp, kt, vt = _qkv_prompt(xp, modl, norm_mix_pre[l], w_gq, i, n_b, gq_kv,
                                         gqa=True, tm=tm_p, extra=(bd, qg, kg2))
            gq_kv = (kt, vt)
            qs, kts, vs = _qkv_sample_gqa(xs, modl, map_s, norm_mix_pre[l], w_gq, i, bd, qkg,
                                          rope_tabs, tm=tm_s)
            op = _attn_prompt(qp, kt, vp, i, NH // NKV)
            os_ = _attn_gqa_sample(qs, kts, vs, cg_kt, cg_vt, i)
        xp = _mix_ffn(op, xp, modl, map_p, norm_mix_post[l], w_o, i, norm_ffn_pre[l], norm_ffn_post[l],
                      wup, ffn_conv_w, cb, wdn, l, tm=tm_fp, halo=False)
        xs = _mix_ffn(os_, xs, modl, map_fs, norm_mix_post[l], w_o, i, norm_ffn_pre[l], norm_ffn_post[l],
                      wup, ffn_conv_w, cb, wdn, l, tm=tm_fs, halo=True)

    return (xp.reshape(BATCH, SEQ, D), xs.reshape(DEC_BATCH, DEC_SEQ, D),
            _cache_out(na_kv[0], NH), _cache_out(na_kv[1], NH),
            _cache_out(gq_kv[0], NKV), _cache_out(gq_kv[1], NKV))
```

```python
import functools

import jax
import jax.numpy as jnp
from jax import lax
from jax.experimental import pallas as pl
from jax.experimental.pallas import tpu as pltpu

D = 1024
HD = 64
NH = 16
NKV = 4
DFF = 2816
DEPTH = 4
GRID_W = 64
ROWS = 32
NA_KH = 8
NA_KW = 16
ROPE_BASE = 10000.0
EPS = 1e-6
BATCH, SEQ = 32, 256
DEC_BATCH, DEC_SEQ = 2, 2048
PAST = 256
NP = BATCH * SEQ
NS = DEC_BATCH * DEC_SEQ

BF = jnp.bfloat16
F32 = jnp.float32
NEG = -1e30
LANES = 128
SUBLANES = 8
FF_CHUNK = 256
N_CHUNK = DFF // FF_CHUNK
VMEM_LIMIT = 50 * 1024 * 1024


def _params(sem):
    return pltpu.CompilerParams(dimension_semantics=sem, vmem_limit_bytes=VMEM_LIMIT)


def _rms(x, g):
    return x * lax.rsqrt(jnp.mean(x * x, axis=-1, keepdims=True) + EPS) * g


def _dot(a, b):
    return jnp.dot(a, b, preferred_element_type=F32)


def _dot_tn_nt(a, b):
    return lax.dot_general(a, b, (((0,), (1,)), ((), ())), preferred_element_type=F32)


def _qkv_weight_specs(i, gqa):
    if gqa:
        kvw = NKV * HD
        return [pl.BlockSpec((1, D, D), lambda t: (i, 0, 0)),
                pl.BlockSpec((1, D, kvw), lambda t: (i, 0, D // kvw)),
                pl.BlockSpec((1, D, kvw), lambda t: (i, 0, D // kvw + 1))]
    return [pl.BlockSpec((1, D, D), functools.partial(lambda t, c: (i, 0, c), c=c)) for c in range(3)]


def _mod_kernel(cond_ref, w_ref, b_ref, o_ref):
    s = cond_ref[...]
    s = s * jax.nn.sigmoid(s)
    o_ref[0] = _dot(s.astype(BF), w_ref[0].astype(BF)) + b_ref[0]


def _adaln(cond8, ada_w, ada_b):
    tn = 1536
    return pl.pallas_call(
        _mod_kernel,
        grid=(DEPTH, 6 * D // tn),
        in_specs=[
            pl.BlockSpec((SUBLANES, D), lambda l, n: (0, 0)),
            pl.BlockSpec((1, D, tn), lambda l, n: (l, 0, n)),
            pl.BlockSpec((1, 1, tn), lambda l, n: (l, 0, n)),
        ],
        out_specs=pl.BlockSpec((1, SUBLANES, tn), lambda l, n: (l, 0, n)),
        out_shape=jax.ShapeDtypeStruct((DEPTH, SUBLANES, 6 * D), F32),
        compiler_params=_params(("arbitrary", "arbitrary")),
        name="adaln",
    )(cond8, ada_w, ada_b.reshape(DEPTH, 1, 6 * D))


def _prenorm(x_ref, mod_ref, g_ref):
    m = mod_ref[0]
    return (_rms(x_ref[...], g_ref[...] * (1.0 + m[:, D:2 * D])) + m[:, 0:D]).astype(BF)


def _head_rms_lanes(blk, bd_ref):
    ss = _dot((blk * blk).astype(BF), bd_ref[...])
    return blk * lax.rsqrt(ss * (1.0 / HD) + EPS)


def _qkv_prompt_kernel(*refs, gqa, slot):
    if gqa:
        (x_ref, mod_ref, g_ref, wq_ref, wk_ref, wv_ref, bd_ref, qg_ref, kg_ref,
         q_ref, v_ref, kt_ref, vt_ref) = refs
    else:
        x_ref, mod_ref, g_ref, wq_ref, wk_ref, wv_ref, q_ref, v_ref, kt_ref, vt_ref = refs
    hb = _prenorm(x_ref, mod_ref, g_ref)
    q = _dot(hb, wq_ref[0])
    v = _dot(hb, wv_ref[0])
    kt = _dot_tn_nt(wk_ref[0], hb)
    if gqa:
        for t in range(D // 256):
            sl = slice(t * 256, (t + 1) * 256)
            q_ref[:, sl] = (_head_rms_lanes(q[:, sl], bd_ref) * qg_ref[:, sl]).astype(q_ref.dtype)
        heads = []
        for h in range(NKV):
            blk = kt[h * HD:(h + 1) * HD, :]
            heads.append(blk * lax.rsqrt(jnp.mean(blk * blk, axis=0, keepdims=True) + EPS))
        kt = jnp.concatenate(heads, axis=0)
    else:
        q_ref[...] = (q * 0.125).astype(q_ref.dtype)
    v_ref[...] = v.astype(v_ref.dtype)
    for s in range(x_ref.shape[0] // SEQ):
        sl = slice(s * SEQ, (s + 1) * SEQ)
        kts = kt[:, sl]
        if gqa:
            kts = kts * kg_ref[...]
        kt_ref[s, slot] = kts
        vt_ref[s, slot] = v[sl, :].T
        for other in range(kt_ref.shape[1]):
            if other != slot:
                kt_ref[s, other] = jnp.zeros_like(kts)
                vt_ref[s, other] = jnp.zeros_like(kts)


def _qkv_prompt(x, modl, g, w, i, n_layers, prev, *, gqa, tm, extra=()):
    kvw = NKV * HD if gqa else D
    in_specs = [
        pl.BlockSpec((tm, D), lambda t: (t, 0)),
        pl.BlockSpec((1, 1, 6 * D), lambda t: (0, 0, 0)),
        pl.BlockSpec((1, D), lambda t: (0, 0)),
    ] + _qkv_weight_specs(i, gqa)
    args = [x, modl, g.reshape(1, D), w, w, w]
    if gqa:
        in_specs += [pl.BlockSpec((256, 256), lambda t: (0, 0)),
                     pl.BlockSpec((1, D), lambda t: (0, 0)),
                     pl.BlockSpec((kvw, SEQ), lambda t: (0, 0))]
        args += list(extra)
    n_in = len(args)
    aliases = {}
    if prev is None:
        kv_spec = pl.BlockSpec((tm // SEQ, n_layers, kvw, SEQ), lambda t: (t, 0, 0, 0))
        slot = i
    else:
        in_specs += [pl.BlockSpec(memory_space=pl.ANY), pl.BlockSpec(memory_space=pl.ANY)]
        args += list(prev)
        aliases = {n_in: 2, n_in + 1: 3}
        kv_spec = pl.BlockSpec((tm // SEQ, 1, kvw, SEQ), lambda t: (t, i, 0, 0))
        slot = 0
    kv_shape = jax.ShapeDtypeStruct((BATCH, n_layers, kvw, SEQ), F32)

    def body(*refs):
        _qkv_prompt_kernel(*(refs[:n_in] + refs[len(args):]), gqa=gqa, slot=slot)

    return pl.pallas_call(
        body,
        grid=(NP // tm,),
        in_specs=in_specs,
        out_specs=[
            pl.BlockSpec((tm, D), lambda t: (t, 0)),
            pl.BlockSpec((tm, kvw), lambda t: (t, 0)),
            kv_spec, kv_spec,
        ],
        out_shape=[
            jax.ShapeDtypeStruct((NP, D), BF),
            jax.ShapeDtypeStruct((NP, kvw), BF),
            kv_shape, kv_shape,
        ],
        input_output_aliases=aliases,
        compiler_params=_params(("arbitrary",)),
        name="qkv_prompt_gqa" if gqa else "qkv_prompt_na",
    )(*args)


def _qkv_sample_na_kernel(x_ref, mod_ref, g_ref, wq_ref, wk_ref, wv_ref, q_ref, kt_ref, v_ref):
    hb = _prenorm(x_ref, mod_ref, g_ref)
    q_ref[...] = (_dot(hb, wq_ref[0]) * 0.125).astype(q_ref.dtype)
    v_ref[...] = _dot(hb, wv_ref[0]).astype(v_ref.dtype)
    kt_ref[...] = _dot_tn_nt(wk_ref[0], hb).astype(kt_ref.dtype)


def _qkv_sample_na(x, modl, mod_map, g, w, i, *, tm):
    return pl.pallas_call(
        _qkv_sample_na_kernel,
        grid=(NS // tm,),
        in_specs=[
            pl.BlockSpec((tm, D), lambda t: (t, 0)),
            pl.BlockSpec((1, 1, 6 * D), mod_map),
            pl.BlockSpec((1, D), lambda t: (0, 0)),
        ] + _qkv_weight_specs(i, False),
        out_specs=[
            pl.BlockSpec((tm, D), lambda t: (t, 0)),
            pl.BlockSpec((D, tm), lambda t: (0, t)),
            pl.BlockSpec((tm, D), lambda t: (t, 0)),
        ],
        out_shape=[
            jax.ShapeDtypeStruct((NS, D), BF),
            jax.ShapeDtypeStruct((D, NS), BF),
            jax.ShapeDtypeStruct((NS, D), BF),
        ],
        compiler_params=_params(("arbitrary",)),
        name="qkv_sample_na",
    )(x, modl, g.reshape(1, D), w, w, w)


def _qkv_sample_gqa_kernel(x_ref, mod_ref, g_ref, w_ref, bd_ref, qkg_ref, cos_ref, s1_ref, s2_ref,
                           q_ref, kt_ref, v_ref):
    qkv = _dot(_prenorm(x_ref, mod_ref, g_ref), w_ref[0])
    nqk = D + NKV * HD
    for t in range(nqk // LANES):
        if t % 2 == 0:
            nrm = _head_rms_lanes(qkv[:, (t // 2) * 256:(t // 2 + 1) * 256], bd_ref)
        xt = nrm[:, (t % 2) * LANES:(t % 2 + 1) * LANES] * qkg_ref[:, t * LANES:(t + 1) * LANES]
        xt = (xt * cos_ref[...]
              + pltpu.roll(xt, LANES - 16, 1) * s1_ref[...]
              + pltpu.roll(xt, 16, 1) * s2_ref[...])
        if t < D // LANES:
            q_ref[:, t * LANES:(t + 1) * LANES] = xt.astype(q_ref.dtype)
        else:
            tk = t - D // LANES
            kt_ref[tk * LANES:(tk + 1) * LANES, :] = xt.T.astype(kt_ref.dtype)
    v_ref[...] = qkv[:, nqk:].astype(v_ref.dtype)


def _qkv_sample_gqa(x, modl, mod_map, g, w, i, bd, qkg, rope_tabs, *, tm):
    nq = w.shape[2]
    kvw = NKV * HD
    tps = DEC_SEQ // tm
    in_specs = [
        pl.BlockSpec((tm, D), lambda t: (t, 0)),
        pl.BlockSpec((1, 1, 6 * D), mod_map),
        pl.BlockSpec((1, D), lambda t: (0, 0)),
        pl.BlockSpec((1, D, nq), lambda t: (i, 0, 0)),
        pl.BlockSpec((256, 256), lambda t: (0, 0)),
        pl.BlockSpec((1, D + kvw), lambda t: (0, 0)),
    ]
    in_specs += [pl.BlockSpec((tm, LANES), lambda t: (t % tps, 0)) for _ in rope_tabs]
    return pl.pallas_call(
        _qkv_sample_gqa_kernel,
        grid=(NS // tm,),
        in_specs=in_specs,
        out_specs=[
            pl.BlockSpec((tm, D), lambda t: (t, 0)),
            pl.BlockSpec((kvw, tm), lambda t: (0, t)),
            pl.BlockSpec((tm, kvw), lambda t: (t, 0)),
        ],
        out_shape=[
            jax.ShapeDtypeStruct((NS, D), BF),
            jax.ShapeDtypeStruct((kvw, NS), BF),
            jax.ShapeDtypeStruct((NS, kvw), BF),
        ],
        compiler_params=_params(("arbitrary",)),
        name="qkv_sample_gqa",
    )(x, modl, g.reshape(1, D), w, bd, qkg, *rope_tabs)


def _lane_lo(rows):
    return lax.broadcasted_iota(jnp.int32, (rows, LANES), 1) < HD


def _pipelined(n, first, second):
    nxt = first(0)
    for i in range(n):
        cur = nxt
        if i + 1 < n:
            nxt = first(i + 1)
        second(i, cur)


def _attn_grouped(problems, group, stack, *, ones_sum):
    tq = problems[0][0].shape[0]
    lo = _lane_lo(tq)
    lo_s = _lane_lo(stack * tq)
    pending = {}
    per = NH // stack

    def scores(n):
        q_ref, _, key_tiles, _ = problems[n // per]
        u = n % per
        g = u * stack // group
        parts = []
        for h in range(u * stack, (u + 1) * stack):
            q2 = q_ref[:, (h // 2) * LANES:(h // 2 + 1) * LANES]
            if h % 2 != g % 2:
                q2 = pltpu.roll(q2.astype(F32), HD, 1).astype(BF)
            parts.append(q2)
        qs = parts[0] if stack == 1 else jnp.concatenate(parts, axis=0)
        mine = lo_s if g % 2 == 0 else jnp.logical_not(lo_s)
        qsel = jnp.where(mine, qs, jnp.zeros_like(qs))
        return [_dot(qsel, kt) for kt in key_tiles(g)]

    def finish(n, ss):
        _, o_ref, _, value_tiles = problems[n // per]
        u = n % per
        g = u * stack // group
        m = ss[0].max(axis=-1, keepdims=True)
        for s in ss[1:]:
            m = jnp.maximum(m, s.max(axis=-1, keepdims=True))
        es = [jnp.exp(s - m) for s in ss]
        pv = None
        for e, v in zip(es, value_tiles(g)):
            if ones_sum:
                mine = _lane_lo(v.shape[0])
                if g % 2:
                    mine = jnp.logical_not(mine)
                v = jnp.where(mine, v, jnp.ones_like(v))
            d = _dot(e.astype(BF), v)
            pv = d if pv is None else pv + d
        if ones_sum:
            pv = pv / pltpu.roll(pv, HD, 1)
        else:
            l = es[0].sum(axis=-1, keepdims=True)
            for e in es[1:]:
                l = l + e.sum(axis=-1, keepdims=True)
            pv = pv / l
        for j in range(stack):
            h = u * stack + j
            pj = pv[j * tq:(j + 1) * tq]
            if h % 2 != g % 2:
                pj = pltpu.roll(pj, HD, 1)
            if h % 2 == 0:
                pending[h // 2] = pj
            else:
                sl = slice((h // 2) * LANES, (h // 2 + 1) * LANES)
                o_ref[:, sl] = jnp.where(lo, pending.pop(h // 2), pj).astype(o_ref.dtype)

    _pipelined(per * len(problems), scores, finish)


ATTN_P_SEQS = 4


def _attn_prompt_kernel(q_ref, kt_ref, v_ref, o_ref, *, group):
    def tile(g):
        return slice((g // 2) * LANES, (g // 2 + 1) * LANES)

    def problem(s):
        rows = pl.ds(s * SEQ, SEQ)
        return (q_ref.at[rows, :], o_ref.at[rows, :],
                lambda g: [kt_ref[s, 0, tile(g), :].astype(BF)],
                lambda g: [v_ref[pl.ds(s * SEQ, SEQ), tile(g)]])
    _attn_grouped([problem(s) for s in range(ATTN_P_SEQS)], group, group, ones_sum=False)


def _attn_prompt(q, kt_all, v, i, group):
    kvw = v.shape[1]
    rows = ATTN_P_SEQS * SEQ
    return pl.pallas_call(
        functools.partial(_attn_prompt_kernel, group=group),
        grid=(BATCH // ATTN_P_SEQS,),
        in_specs=[
            pl.BlockSpec((rows, D), lambda b: (b, 0)),
            pl.BlockSpec((ATTN_P_SEQS, 1, kvw, SEQ), lambda b: (b, i, 0, 0)),
            pl.BlockSpec((rows, kvw), lambda b: (b, 0)),
        ],
        out_specs=pl.BlockSpec((rows, D), lambda b: (b, 0)),
        out_shape=jax.ShapeDtypeStruct((NP, D), BF),
        compiler_params=_params(("arbitrary",)),
        name="attn_prompt",
    )(q, kt_all, v)


def _attn_gqa_sample_kernel(q_ref, kt_ref, v_ref, kct_ref, vct_ref, o_ref):
    vc = vct_ref[0, 0].T.astype(BF)

    def tile(g):
        return slice((g // 2) * LANES, (g // 2 + 1) * LANES)
    _attn_grouped([(q_ref, o_ref,
                    lambda g: [kt_ref[tile(g), :], kct_ref[0, 0, tile(g), :].astype(BF)],
                    lambda g: [v_ref[:, tile(g)], vc[:, tile(g)]])],
                  NH // NKV, NH // NKV, ones_sum=True)


GQA_TQ = 256


def _attn_gqa_sample(q, kt, v, cache_kt, cache_vt, i):
    nq = DEC_SEQ // GQA_TQ
    kvw = NKV * HD
    return pl.pallas_call(
        _attn_gqa_sample_kernel,
        grid=(DEC_BATCH, nq),
        in_specs=[
            pl.BlockSpec((GQA_TQ, D), lambda b, t: (b * nq + t, 0)),
            pl.BlockSpec((kvw, DEC_SEQ), lambda b, t: (0, b)),
            pl.BlockSpec((DEC_SEQ, kvw), lambda b, t: (b, 0)),
            pl.BlockSpec((1, 1, kvw, PAST), lambda b, t: (b, i, 0, 0)),
            pl.BlockSpec((1, 1, kvw, PAST), lambda b, t: (b, i, 0, 0)),
        ],
        out_specs=pl.BlockSpec((GQA_TQ, D), lambda b, t: (b * nq + t, 0)),
        out_shape=jax.ShapeDtypeStruct((NS, D), BF),
        compiler_params=_params(("arbitrary", "arbitrary")),
        name="attn_gqa_sample",
    )(q, kt, v, cache_kt, cache_vt)


NA_QROWS = 4
NA_KBLK = NA_QROWS * GRID_W
NA_NKB = 3
NA_PAIRS = NA_QROWS // 2


def _attn_na_sample_kernel(*refs):
    q_ref = refs[0]
    kt_refs = refs[1:1 + NA_NKB]
    v_refs = refs[1 + NA_NKB:1 + 2 * NA_NKB]
    kct_ref, vct_ref, p_ref, o_ref = refs[1 + 2 * NA_NKB:]
    rb = pl.program_id(1)
    kb0 = jnp.clip(rb - 1, 0, ROWS // NA_QROWS - NA_NKB)
    tq = q_ref.shape[0]
    lo = _lane_lo(tq)
    hi = jnp.logical_not(lo)
    lo_w = _lane_lo(GRID_W)
    vc = vct_ref[0, 0].T.astype(BF)
    ones = jnp.ones((tq, LANES), BF)

    tiles = []
    for i in range(NA_QROWS):
        qr = rb * NA_QROWS + i
        rs = jnp.clip(qr - NA_KH // 2, 0, ROWS - NA_KH)
        for j in range(NA_NKB):
            for jj in range(NA_PAIRS):
                kra = (kb0 + j) * NA_QROWS + 2 * jj
                krb = kra + 1
                idx = jnp.clip(kra - qr + NA_KH, 0, 2 * NA_KH - 1)
                va = jnp.logical_and(kra >= rs, kra < rs + NA_KH).astype(jnp.int32)
                vb = jnp.logical_and(krb >= rs, krb < rs + NA_KH).astype(jnp.int32)
                tiles.append((idx, jnp.where(lo_w, va, vb) > 0))

    def scores(h):
        rows = slice((h // 2) * LANES, (h // 2 + 1) * LANES)
        q2 = q_ref[:, rows]
        qsel = jnp.where(lo if h % 2 == 0 else hi, q2, jnp.zeros_like(q2))
        s_lat = [_dot(qsel, r[rows, :]) for r in kt_refs]
        out_rows = []
        for i in range(NA_QROWS):
            rsl = slice(i * GRID_W, (i + 1) * GRID_W)
            row = []
            for j in range(NA_NKB):
                for jj in range(NA_PAIRS):
                    idx, valid = tiles[(i * NA_NKB + j) * NA_PAIRS + jj]
                    t = s_lat[j][rsl, jj * LANES:(jj + 1) * LANES] + p_ref[0, h, idx]
                    row.append(jnp.where(valid, t, NEG))
            out_rows.append(jnp.concatenate(row, axis=1))
        return jnp.concatenate(out_rows, axis=0), _dot(qsel, kct_ref[0, 0, rows, :].astype(BF))

    def weighted_values(h, s, s_ctx):
        rows = slice((h // 2) * LANES, (h // 2 + 1) * LANES)
        mine = lo if h % 2 == 0 else hi
        m = jnp.maximum(s.max(axis=-1, keepdims=True), s_ctx.max(axis=-1, keepdims=True))
        e = jnp.exp(s - m).astype(BF)
        e_ctx = jnp.exp(s_ctx - m).astype(BF)
        pv = _dot(e_ctx, jnp.where(mine, vc[:, rows], ones))
        for j in range(NA_NKB):
            vj = jnp.where(mine, v_refs[j][:, rows], ones)
            pv = pv + _dot(e[:, j * NA_KBLK:(j + 1) * NA_KBLK], vj)
        return pv / pltpu.roll(pv, HD, 1)

    nxt = scores(0)
    o2 = None
    for h in range(NH):
        cur = nxt
        if h + 1 < NH:
            nxt = scores(h + 1)
        pv = weighted_values(h, *cur)
        if h % 2 == 0:
            o2 = pv
        else:
            o_ref[:, (h // 2) * LANES:(h // 2 + 1) * LANES] = jnp.where(lo, o2, pv).astype(o_ref.dtype)


def _attn_na_sample(q, kt, v, cache_kt, cache_vt, i, ptab):
    nrb = ROWS // NA_QROWS

    def kb(b, r, j):
        return b * nrb + jnp.clip(r - 1, 0, nrb - NA_NKB) + j

    in_specs = [pl.BlockSpec((NA_KBLK, D), lambda b, r: (b * nrb + r, 0))]
    in_specs += [pl.BlockSpec((D, NA_KBLK), functools.partial(lambda b, r, j: (0, kb(b, r, j)), j=j))
                 for j in range(NA_NKB)]
    in_specs += [pl.BlockSpec((NA_KBLK, D), functools.partial(lambda b, r, j: (kb(b, r, j), 0), j=j))
                 for j in range(NA_NKB)]
    in_specs += [
        pl.BlockSpec((1, 1, D, PAST), lambda b, r: (b, i, 0, 0)),
        pl.BlockSpec((1, 1, D, PAST), lambda b, r: (b, i, 0, 0)),
        pl.BlockSpec((1, NH, 2 * NA_KH, GRID_W, LANES), lambda b, r: (i, 0, 0, 0, 0),
                     pipeline_mode=pl.Buffered(1)),
    ]
    return pl.pallas_call(
        _attn_na_sample_kernel,
        grid=(DEC_BATCH, nrb),
        in_specs=in_specs,
        out_specs=pl.BlockSpec((NA_KBLK, D), lambda b, r: (b * nrb + r, 0)),
        out_shape=jax.ShapeDtypeStruct((NS, D), BF),
        compiler_params=_params(("arbitrary", "arbitrary")),
        name="attn_na_sample",
    )(q, *([kt] * NA_NKB), *([v] * NA_NKB), cache_kt, cache_vt, ptab)


N_DC = 2 * NA_KW - 1
N_DR = 2 * NA_KH - 1
N_ENT = N_DR + 1
DC_PAD = 32
N_SPLIT = 3
TILE_ELEMS = GRID_W * LANES


def _na_bias_kernel(lhs_ref, sel_ref, ok_ref, o_ref):
    t = _dot(lhs_ref[0], sel_ref[...])
    ok = ok_ref[...] > 0.0
    for h in range(NH):
        rows = slice(h * N_ENT, (h + 1) * N_ENT)
        o_ref[0, rows, :] = jnp.where(ok, t[rows, :], NEG)


def _na_bias_tables(rpb):
    n_a = rpb.shape[0]
    qc = jnp.arange(GRID_W)[:, None]
    lane = jnp.arange(LANES)[None, :]
    kc = lane % GRID_W
    half = lane // GRID_W
    c_start = jnp.clip(qc - NA_KW // 2, 0, GRID_W - NA_KW)
    ok = (kc >= c_start) & (kc < c_start + NA_KW)
    slot = (half * DC_PAD + kc - qc + (NA_KW - 1)).reshape(-1)
    sel = (jnp.arange(2 * DC_PAD)[:, None] == slot[None, :]) & ok.reshape(1, -1)
    sel = jnp.tile(sel.astype(BF), (N_SPLIT, 1))
    d = jnp.arange(N_ENT)[:, None, None]
    ok_d = ok[None] & jnp.where(half[None] == 0, d >= 1, d <= N_DR - 1)
    ok_d = ok_d.reshape(N_ENT, TILE_ELEMS).astype(F32)
    pad = jnp.zeros((n_a, NH, 1, N_DC), F32)
    left = jnp.concatenate([pad, rpb], axis=2)
    right = jnp.concatenate([rpb, pad], axis=2)
    padc = lambda a: jnp.pad(a, ((0, 0), (0, 0), (0, 0), (0, DC_PAD - N_DC)))
    both = jnp.concatenate([padc(left), padc(right)], axis=-1).reshape(n_a, NH * N_ENT, 2 * DC_PAD)
    pieces = []
    rest = both
    for _ in range(N_SPLIT):
        piece = rest.astype(BF)
        pieces.append(piece)
        rest = rest - piece.astype(F32)
    lhs = jnp.concatenate(pieces, axis=-1)
    k = N_SPLIT * 2 * DC_PAD
    out = pl.pallas_call(
        _na_bias_kernel,
        grid=(n_a,),
        in_specs=[
            pl.BlockSpec((1, NH * N_ENT, k), lambda a: (a, 0, 0)),
            pl.BlockSpec((k, TILE_ELEMS), lambda a: (0, 0)),
            pl.BlockSpec((N_ENT, TILE_ELEMS), lambda a: (0, 0)),
        ],
        out_specs=pl.BlockSpec((1, NH * N_ENT, TILE_ELEMS), lambda a: (a, 0, 0)),
        out_shape=jax.ShapeDtypeStruct((n_a, NH * N_ENT, TILE_ELEMS), F32),
        compiler_params=_params(("arbitrary",)),
        name="na_bias",
    )(lhs, sel, ok_d)
    return out.reshape(n_a, NH, N_ENT, GRID_W, LANES)


HALO_O = 16


def _mix_ffn_kernel(*refs, halo, tiles_per_seq):
    if halo:
        (o_ref, op_ref, on_ref, x_ref, xp_ref, xn_ref, mod_ref, gmix_ref, wo_ref,
         gpre_ref, gpost_ref, wup_ref, cw_ref, cb_ref, wdn_ref, out_ref, act_scr) = refs
    else:
        (o_ref, x_ref, mod_ref, gmix_ref, wo_ref,
         gpre_ref, gpost_ref, wup_ref, cw_ref, cb_ref, wdn_ref, out_ref, act_scr) = refs
    tm = x_ref.shape[0]
    m = mod_ref[0]
    gate_mix = m[:, 2 * D:3 * D]
    sh = m[:, 3 * D:4 * D]
    sc = m[:, 4 * D:5 * D]
    gate = m[:, 5 * D:6 * D]
    zeros = jnp.zeros((SUBLANES, D), F32)

    if halo:
        o_ext = jnp.concatenate([op_ref[...].astype(F32)[HALO_O - SUBLANES:], o_ref[...].astype(F32),
                                 on_ref[...].astype(F32)[:SUBLANES]], axis=0).astype(BF)
        x_ext = jnp.concatenate([xp_ref[...], x_ref[...], xn_ref[...]], axis=0)
    else:
        o_ext = o_ref[...]
        x_ext = x_ref[...]
    x_ext = x_ext + _rms(_dot(o_ext, wo_ref[0]), gate_mix * gmix_ref[...])
    h = _rms(x_ext, gpre_ref[...] * (1.0 + sc)) + sh
    if halo:
        t = pl.program_id(0) % tiles_per_seq
        x = x_ext[SUBLANES:SUBLANES + tm]
        hp = jnp.where(t == 0, zeros, h[:SUBLANES])
        hn = jnp.where(t == tiles_per_seq - 1, zeros, h[SUBLANES + tm:])
        hext = jnp.concatenate([hp, h[SUBLANES:SUBLANES + tm], hn], axis=0).astype(BF)
        starts = [SUBLANES]
        span = tm
    else:
        x = x_ext
        parts = [zeros]
        for s in range(tm // SEQ):
            parts += [h[s * SEQ:(s + 1) * SEQ], zeros]
        hext = jnp.concatenate(parts, axis=0).astype(BF)
        starts = [SUBLANES + s * (SEQ + SUBLANES) for s in range(tm // SEQ)]
        span = SEQ
    mext = hext.shape[0]

    def token_rows(v):
        parts = [v[r0:r0 + span] for r0 in starts]
        return parts[0] if len(parts) == 1 else jnp.concatenate(parts, axis=0)

    def conv(u, c0):
        cols = slice(c0, c0 + FF_CHUNK)
        prev = token_rows(pltpu.roll(u, 1, 0))
        nxt = token_rows(pltpu.roll(u, mext - 1, 0))
        return (prev * cw_ref[0, 0:1, cols] + token_rows(u) * cw_ref[0, 1:2, cols]
                + nxt * cw_ref[0, 2:3, cols] + cb_ref[0, :, cols])

    for j in range(N_CHUNK):
        ca = j * FF_CHUNK
        cg = DFF + j * FF_CHUNK
        a = conv(_dot(hext, wup_ref[0, :, ca:ca + FF_CHUNK]), ca)
        gt = conv(_dot(hext, wup_ref[0, :, cg:cg + FF_CHUNK]), cg)
        ha = 0.5 * a
        act_scr[:, ca:ca + FF_CHUNK] = ((ha + ha * jnp.tanh(ha)) * gt).astype(BF)
    y = _dot(act_scr[...], wdn_ref[0])
    out_ref[...] = x + _rms(y, gate * gpost_ref[...])


def _mix_ffn(o, x, modl, mod_map, gmix, w_o, i, gpre, gpost, wup, cw, cb, wdn, l, *, tm, halo):
    n = x.shape[0]
    tps = DEC_SEQ // tm

    def prev_blk(rows):
        r = tm // rows
        return lambda t: (jnp.maximum(t * r - 1, 0), 0)

    def next_blk(rows):
        r = tm // rows
        return lambda t: (jnp.minimum((t + 1) * r, n // rows - 1), 0)

    in_specs = [pl.BlockSpec((tm, D), lambda t: (t, 0))]
    args = [o]
    if halo:
        in_specs += [pl.BlockSpec((HALO_O, D), prev_blk(HALO_O)), pl.BlockSpec((HALO_O, D), next_blk(HALO_O))]
        args += [o, o]
    in_specs.append(pl.BlockSpec((tm, D), lambda t: (t, 0)))
    args.append(x)
    if halo:
        in_specs += [pl.BlockSpec((SUBLANES, D), prev_blk(SUBLANES)),
                     pl.BlockSpec((SUBLANES, D), next_blk(SUBLANES))]
        args += [x, x]
    const = dict(pipeline_mode=pl.Buffered(1))
    in_specs += [
        pl.BlockSpec((1, 1, 6 * D), mod_map),
        pl.BlockSpec((1, D), lambda t: (0, 0)),
        pl.BlockSpec((1, D, D), lambda t: (i, 0, 0), **const),
        pl.BlockSpec((1, D), lambda t: (0, 0)),
        pl.BlockSpec((1, D), lambda t: (0, 0)),
        pl.BlockSpec((1, D, 2 * DFF), lambda t: (l, 0, 0), **const),
        pl.BlockSpec((1, 3, 2 * DFF), lambda t: (l, 0, 0)),
        pl.BlockSpec((1, 1, 2 * DFF), lambda t: (l, 0, 0)),
        pl.BlockSpec((1, DFF, D), lambda t: (l, 0, 0), **const),
    ]
    args += [modl, gmix.reshape(1, D), w_o, gpre.reshape(1, D), gpost.reshape(1, D), wup, cw, cb, wdn]
    return pl.pallas_call(
        functools.partial(_mix_ffn_kernel, halo=halo, tiles_per_seq=tps),
        grid=(n // tm,),
        in_specs=in_specs,
        out_specs=pl.BlockSpec((tm, D), lambda t: (t, 0)),
        out_shape=jax.ShapeDtypeStruct((n, D), F32),
        scratch_shapes=[pltpu.VMEM((tm, DFF), BF)],
        compiler_params=_params(("arbitrary",)),
        name="mix_ffn",
    )(*args)


def _rope_tables():
    t = jnp.arange(DEC_SEQ)
    rows = (t // GRID_W).astype(F32)
    cols = (t % GRID_W).astype(F32)
    quarter = HD // 4
    freqs = ROPE_BASE ** (-jnp.arange(quarter, dtype=F32) / quarter)
    lane = jnp.arange(LANES)
    d = lane % HD
    e = d % (HD // 2)
    is_x2 = (e >= quarter)[None, :]
    pos = jnp.where((d // (HD // 2) == 0)[None, :], rows[:, None], cols[:, None])
    ang = pos * freqs[e % quarter][None, :]
    cos = jnp.cos(ang)
    sin = jnp.sin(ang)
    return cos, jnp.where(is_x2, 0.0, -sin), jnp.where(is_x2, sin, 0.0)


def _block_diag_ones():
    r = jnp.arange(256)
    return (r[:, None] // HD == r[None, :] // HD).astype(BF)


def _cache_t(cache):
    b, n, p, h, d = cache.shape
    return cache.transpose(0, 1, 3, 4, 2).reshape(b, n, h * d, p)


def _cache_out(kt, heads):
    b, n, _, s = kt.shape
    return kt.reshape(b, n, heads, HD, s).transpose(0, 1, 4, 2, 3)


def kernel(x_prompt, x_sample, cache_na_k, cache_na_v, cache_gqa_k, cache_gqa_v, c, c_ctx,
           ada_w, ada_b, norm_mix_pre, norm_mix_post, norm_ffn_pre, norm_ffn_post,
           na_w_qkv, na_w_o, na_rpb, gqa_w_qkv, gqa_w_o, gqa_q_norm, gqa_k_norm,
           ffn_w_up, ffn_conv_w, ffn_conv_b, ffn_w_down):
    n_a = cache_na_k.shape[1]
    n_b = cache_gqa_k.shape[1]
    cond8 = jnp.concatenate([c_ctx[None], c, jnp.zeros((SUBLANES - 1 - DEC_BATCH, D), F32)], axis=0)
    mod = _adaln(cond8, ada_w, ada_b)

    tm_p, tm_s = 512, 512
    tm_fp, tm_fs = 512, 512
    map_p = lambda t: (0, 0, 0)
    map_s = lambda t: (1 + t // (DEC_SEQ // tm_s), 0, 0)
    map_fs = lambda t: (1 + t // (DEC_SEQ // tm_fs), 0, 0)

    cna_kt, cna_vt = _cache_t(cache_na_k), _cache_t(cache_na_v)
    cg_kt, cg_vt = _cache_t(cache_gqa_k), _cache_t(cache_gqa_v)
    rope_tabs = _rope_tables()
    bd = _block_diag_ones()
    ptabs = _na_bias_tables(na_rpb)
    kvg = NKV * HD

    w_na, w_gq = na_w_qkv.astype(BF), gqa_w_qkv.astype(BF)
    wo_na, wo_gq = na_w_o.astype(BF), gqa_w_o.astype(BF)
    wup, wdn = ffn_w_up.astype(BF), ffn_w_down.astype(BF)
    cb = ffn_conv_b.reshape(DEPTH, 1, 2 * DFF)

    xp = x_prompt.reshape(NP, D)
    xs = x_sample.reshape(NS, D)
    na_kv = None
    gq_kv = None
    for l in range(DEPTH):
        i = l // 2
        modl = mod[l].reshape(SUBLANES, 1, 6 * D)
        if l % 2 == 0:
            w_o = wo_na
            qp, vp, kt, vt = _qkv_prompt(xp, modl, norm_mix_pre[l], w_na, i, n_a, na_kv,
                                         gqa=False, tm=tm_p)
            na_kv = (kt, vt)
            qs, kts, vs = _qkv_sample_na(xs, modl, map_s, norm_mix_pre[l], w_na, i, tm=tm_s)
            op = _attn_prompt(qp, kt, vp, i, 1)
            os_ = _attn_na_sample(qs, kts, vs, cna_kt, cna_vt, i, ptabs)
        else:
            w_o = wo_gq
            qg = (jnp.tile(gqa_q_norm[i], NH) * 0.125).reshape(1, D)
            kg = jnp.tile(gqa_k_norm[i], NKV)
            qkg = jnp.concatenate([qg[0], kg]).reshape(1, D + kvg)
            kg2 = jnp.broadcast_to(kg[:, None], (kvg, SEQ))
            qp, vp, kt, vt = _qkv_prompt(xp, modl, norm_mix_pre[l], w_gq, i, n_b, gq_kv,
                                         gqa=True, tm=tm_p, extra=(bd, qg, kg2))
            gq_kv = (kt, vt)
            qs, kts, vs = _qkv_sample_gqa(xs, modl, map_s, norm_mix_pre[l], w_gq, i, bd, qkg,
                                          rope_tabs, tm=tm_s)
            op = _attn_prompt(qp, kt, vp, i, NH // NKV)
            os_ = _attn_gqa_sample(qs, kts, vs, cg_kt, cg_vt, i)
        xp = _mix_ffn(op, xp, modl, map_p, norm_mix_post[l], w_o, i, norm_ffn_pre[l], norm_ffn_post[l],
                      wup, ffn_conv_w, cb, wdn, l, tm=tm_fp, halo=False)
        xs = _mix_ffn(os_, xs, modl, map_fs, norm_mix_post[l], w_o, i, norm_ffn_pre[l], norm_ffn_post[l],
                      wup, ffn_conv_w, cb, wdn, l, tm=tm_fs, halo=True)

    return (xp.reshape(BATCH, SEQ, D), xs.reshape(DEC_BATCH, DEC_SEQ, D),
            _cache_out(na_kv[0], NH), _cache_out(na_kv[1], NH),
            _cache_out(gq_kv[0], NKV), _cache_out(gq_kv[1], NKV))
```

```python
import functools

import jax
import jax.numpy as jnp
from jax import lax
from jax.experimental import pallas as pl
from jax.experimental.pallas import tpu as pltpu

D = 1024
HD = 64
NH = 16
NKV = 4
DFF = 2816
DEPTH = 4
GRID_W = 64
ROWS = 32
NA_KH = 8
NA_KW = 16
ROPE_BASE = 10000.0
EPS = 1e-6
BATCH, SEQ = 32, 256
DEC_BATCH, DEC_SEQ = 2, 2048
PAST = 256
NP = BATCH * SEQ
NS = DEC_BATCH * DEC_SEQ

BF = jnp.bfloat16
F32 = jnp.float32
NEG = -1e30
LANES = 128
SUBLANES = 8
FF_CHUNK = 256
N_CHUNK = DFF // FF_CHUNK
VMEM_LIMIT = 50 * 1024 * 1024


def _params(sem):
    return pltpu.CompilerParams(dimension_semantics=sem, vmem_limit_bytes=VMEM_LIMIT)


def _rms(x, g):
    return x * lax.rsqrt(jnp.mean(x * x, axis=-1, keepdims=True) + EPS) * g


def _dot(a, b):
    return jnp.dot(a, b, preferred_element_type=F32)


def _dot_tn_nt(a, b):
    return lax.dot_general(a, b, (((0,), (1,)), ((), ())), preferred_element_type=F32)


def _qkv_weight_specs(i, gqa):
    if gqa:
        kvw = NKV * HD
        return [pl.BlockSpec((1, D, D), lambda t: (i, 0, 0)),
                pl.BlockSpec((1, D, kvw), lambda t: (i, 0, D // kvw)),
                pl.BlockSpec((1, D, kvw), lambda t: (i, 0, D // kvw + 1))]
    return [pl.BlockSpec((1, D, D), functools.partial(lambda t, c: (i, 0, c), c=c)) for c in range(3)]


def _mod_kernel(cond_ref, w_ref, b_ref, o_ref):
    s = cond_ref[...]
    s = s * jax.nn.sigmoid(s)
    o_ref[0] = _dot(s.astype(BF), w_ref[0].astype(BF)) + b_ref[0]


def _adaln(cond8, ada_w, ada_b):
    tn = 1536
    return pl.pallas_call(
        _mod_kernel,
        grid=(DEPTH, 6 * D // tn),
        in_specs=[
            pl.BlockSpec((SUBLANES, D), lambda l, n: (0, 0)),
            pl.BlockSpec((1, D, tn), lambda l, n: (l, 0, n)),
            pl.BlockSpec((1, 1, tn), lambda l, n: (l, 0, n)),
        ],
        out_specs=pl.BlockSpec((1, SUBLANES, tn), lambda l, n: (l, 0, n)),
        out_shape=jax.ShapeDtypeStruct((DEPTH, SUBLANES, 6 * D), F32),
        compiler_params=_params(("arbitrary", "arbitrary")),
        name="adaln",
    )(cond8, ada_w, ada_b.reshape(DEPTH, 1, 6 * D))


def _prenorm(x_ref, mod_ref, g_ref):
    m = mod_ref[0]
    return (_rms(x_ref[...], g_ref[...] * (1.0 + m[:, D:2 * D])) + m[:, 0:D]).astype(BF)


def _head_rms_lanes(blk, bd_ref):
    ss = _dot((blk * blk).astype(BF), bd_ref[...])
    return blk * lax.rsqrt(ss * (1.0 / HD) + EPS)


def _qkv_prompt_kernel(*refs, gqa, slot):
    if gqa:
        (x_ref, mod_ref, g_ref, wq_ref, wk_ref, wv_ref, bd_ref, qg_ref, kg_ref,
         q_ref, v_ref, kt_ref, vt_ref) = refs
    else:
        x_ref, mod_ref, g_ref, wq_ref, wk_ref, wv_ref, q_ref, v_ref, kt_ref, vt_ref = refs
    hb = _prenorm(x_ref, mod_ref, g_ref)
    q = _dot(hb, wq_ref[0])
    v = _dot(hb, wv_ref[0])
    kt = _dot_tn_nt(wk_ref[0], hb)
    if gqa:
        for t in range(D // 256):
            sl = slice(t * 256, (t + 1) * 256)
            q_ref[:, sl] = (_head_rms_lanes(q[:, sl], bd_ref) * qg_ref[:, sl]).astype(q_ref.dtype)
        heads = []
        for h in range(NKV):
            blk = kt[h * HD:(h + 1) * HD, :]
            heads.append(blk * lax.rsqrt(jnp.mean(blk * blk, axis=0, keepdims=True) + EPS))
        kt = jnp.concatenate(heads, axis=0)
    else:
        q_ref[...] = (q * 0.125).astype(q_ref.dtype)
    v_ref[...] = v.astype(v_ref.dtype)
    for s in range(x_ref.shape[0] // SEQ):
        sl = slice(s * SEQ, (s + 1) * SEQ)
        kts = kt[:, sl]
        if gqa:
            kts = kts * kg_ref[...]
        kt_ref[s, slot] = kts
        vt_ref[s, slot] = v[sl, :].T
        for other in range(kt_ref.shape[1]):
            if other != slot:
                kt_ref[s, other] = jnp.zeros_like(kts)
                vt_ref[s, other] = jnp.zeros_like(kts)


def _qkv_prompt(x, modl, g, w, i, n_layers, prev, *, gqa, tm, extra=()):
    kvw = NKV * HD if gqa else D
    in_specs = [
        pl.BlockSpec((tm, D), lambda t: (t, 0)),
        pl.BlockSpec((1, 1, 6 * D), lambda t: (0, 0, 0)),
        pl.BlockSpec((1, D), lambda t: (0, 0)),
    ] + _qkv_weight_specs(i, gqa)
    args = [x, modl, g.reshape(1, D), w, w, w]
    if gqa:
        in_specs += [pl.BlockSpec((256, 256), lambda t: (0, 0)),
                     pl.BlockSpec((1, D), lambda t: (0, 0)),
                     pl.BlockSpec((kvw, SEQ), lambda t: (0, 0))]
        args += list(extra)
    n_in = len(args)
    aliases = {}
    if prev is None:
        kv_spec = pl.BlockSpec((tm // SEQ, n_layers, kvw, SEQ), lambda t: (t, 0, 0, 0))
        slot = i
    else:
        in_specs += [pl.BlockSpec(memory_space=pl.ANY), pl.BlockSpec(memory_space=pl.ANY)]
        args += list(prev)
        aliases = {n_in: 2, n_in + 1: 3}
        kv_spec = pl.BlockSpec((tm // SEQ, 1, kvw, SEQ), lambda t: (t, i, 0, 0))
        slot = 0
    kv_shape = jax.ShapeDtypeStruct((BATCH, n_layers, kvw, SEQ), F32)

    def body(*refs):
        _qkv_prompt_kernel(*(refs[:n_in] + refs[len(args):]), gqa=gqa, slot=slot)

    return pl.pallas_call(
        body,
        grid=(NP // tm,),
        in_specs=in_specs,
        out_specs=[
            pl.BlockSpec((tm, D), lambda t: (t, 0)),
            pl.BlockSpec((tm, kvw), lambda t: (t, 0)),
            kv_spec, kv_spec,
        ],
        out_shape=[
            jax.ShapeDtypeStruct((NP, D), BF),
            jax.ShapeDtypeStruct((NP, kvw), BF),
            kv_shape, kv_shape,
        ],
        input_output_aliases=aliases,
        compiler_params=_params(("arbitrary",)),
        name="qkv_prompt_gqa" if gqa else "qkv_prompt_na",
    )(*args)


def _qkv_sample_na_kernel(x_ref, mod_ref, g_ref, wq_ref, wk_ref, wv_ref, q_ref, kt_ref, v_ref):
    hb = _prenorm(x_ref, mod_ref, g_ref)
    q_ref[...] = (_dot(hb, wq_ref[0]) * 0.125).astype(q_ref.dtype)
    v_ref[...] = _dot(hb, wv_ref[0]).astype(v_ref.dtype)
    kt_ref[...] = _dot_tn_nt(wk_ref[0], hb).astype(kt_ref.dtype)


def _qkv_sample_na(x, modl, mod_map, g, w, i, *, tm):
    return pl.pallas_call(
        _qkv_sample_na_kernel,
        grid=(NS // tm,),
        in_specs=[
            pl.BlockSpec((tm, D), lambda t: (t, 0)),
            pl.BlockSpec((1, 1, 6 * D), mod_map),
            pl.BlockSpec((1, D), lambda t: (0, 0)),
        ] + _qkv_weight_specs(i, False),
        out_specs=[
            pl.BlockSpec((tm, D), lambda t: (t, 0)),
            pl.BlockSpec((D, tm), lambda t: (0, t)),
            pl.BlockSpec((tm, D), lambda t: (t, 0)),
        ],
        out_shape=[
            jax.ShapeDtypeStruct((NS, D), BF),
            jax.ShapeDtypeStruct((D, NS), BF),
            jax.ShapeDtypeStruct((NS, D), BF),
        ],
        compiler_params=_params(("arbitrary",)),
        name="qkv_sample_na",
    )(x, modl, g.reshape(1, D), w, w, w)


def _qkv_sample_gqa_kernel(x_ref, mod_ref, g_ref, w_ref, bd_ref, qkg_ref, cos_ref, s1_ref, s2_ref,
                           q_ref, kt_ref, v_ref):
    qkv = _dot(_prenorm(x_ref, mod_ref, g_ref), w_ref[0])
    nqk = D + NKV * HD
    for t in range(nqk // LANES):
        if t % 2 == 0:
            nrm = _head_rms_lanes(qkv[:, (t // 2) * 256:(t // 2 + 1) * 256], bd_ref)
        xt = nrm[:, (t % 2) * LANES:(t % 2 + 1) * LANES] * qkg_ref[:, t * LANES:(t + 1) * LANES]
        xt = (xt * cos_ref[...]
              + pltpu.roll(xt, LANES - 16, 1) * s1_ref[...]
              + pltpu.roll(xt, 16, 1) * s2_ref[...])
        if t < D // LANES:
            q_ref[:, t * LANES:(t + 1) * LANES] = xt.astype(q_ref.dtype)
        else:
            tk = t - D // LANES
            kt_ref[tk * LANES:(tk + 1) * LANES, :] = xt.T.astype(kt_ref.dtype)
    v_ref[...] = qkv[:, nqk:].astype(v_ref.dtype)


def _qkv_sample_gqa(x, modl, mod_map, g, w, i, bd, qkg, rope_tabs, *, tm):
    nq = w.shape[2]
    kvw = NKV * HD
    tps = DEC_SEQ // tm
    in_specs = [
        pl.BlockSpec((tm, D), lambda t: (t, 0)),
        pl.BlockSpec((1, 1, 6 * D), mod_map),
        pl.BlockSpec((1, D), lambda t: (0, 0)),
        pl.BlockSpec((1, D, nq), lambda t: (i, 0, 0)),
        pl.BlockSpec((256, 256), lambda t: (0, 0)),
        pl.BlockSpec((1, D + kvw), lambda t: (0, 0)),
    ]
    in_specs += [pl.BlockSpec((tm, LANES), lambda t: (t % tps, 0)) for _ in rope_tabs]
    return pl.pallas_call(
        _qkv_sample_gqa_kernel,
        grid=(NS // tm,),
        in_specs=in_specs,
        out_specs=[
            pl.BlockSpec((tm, D), lambda t: (t, 0)),
            pl.BlockSpec((kvw, tm), lambda t: (0, t)),
            pl.BlockSpec((tm, kvw), lambda t: (t, 0)),
        ],
        out_shape=[
            jax.ShapeDtypeStruct((NS, D), BF),
            jax.ShapeDtypeStruct((kvw, NS), BF),
            jax.ShapeDtypeStruct((NS, kvw), BF),
        ],
        compiler_params=_params(("arbitrary",)),
        name="qkv_sample_gqa",
    )(x, modl, g.reshape(1, D), w, bd, qkg, *rope_tabs)


def _lane_lo(rows):
    return lax.broadcasted_iota(jnp.int32, (rows, LANES), 1) < HD


def _pipelined(n, first, second):
    nxt = first(0)
    for i in range(n):
        cur = nxt
        if i + 1 < n:
            nxt = first(i + 1)
        second(i, cur)


def _attn_grouped(problems, group, stack, *, ones_sum):
    tq = problems[0][0].shape[0]
    lo = _lane_lo(tq)
    lo_s = _lane_lo(stack * tq)
    pending = {}
    per = NH // stack

    def scores(n):
        q_ref, _, key_tiles, _ = problems[n // per]
        u = n % per
        g = u * stack // group
        parts = []
        for h in range(u * stack, (u + 1) * stack):
            q2 = q_ref[:, (h // 2) * LANES:(h // 2 + 1) * LANES]
            if h % 2 != g % 2:
                q2 = pltpu.roll(q2.astype(F32), HD, 1).astype(BF)
            parts.append(q2)
        qs = parts[0] if stack == 1 else jnp.concatenate(parts, axis=0)
        mine = lo_s if g % 2 == 0 else jnp.logical_not(lo_s)
        qsel = jnp.where(mine, qs, jnp.zeros_like(qs))
        return [_dot(qsel, kt) for kt in key_tiles(g)]

    def finish(n, ss):
        _, o_ref, _, value_tiles = problems[n // per]
        u = n % per
        g = u * stack // group
        m = ss[0].max(axis=-1, keepdims=True)
        for s in ss[1:]:
            m = jnp.maximum(m, s.max(axis=-1, keepdims=True))
        es = [jnp.exp(s - m) for s in ss]
        pv = None
        for e, v in zip(es, value_tiles(g)):
            if ones_sum:
                mine = _lane_lo(v.shape[0])
                if g % 2:
                    mine = jnp.logical_not(mine)
                v = jnp.where(mine, v, jnp.ones_like(v))
            d = _dot(e.astype(BF), v)
            pv = d if pv is None else pv + d
        if ones_sum:
            pv = pv / pltpu.roll(pv, HD, 1)
        else:
            l = es[0].sum(axis=-1, keepdims=True)
            for e in es[1:]:
                l = l + e.sum(axis=-1, keepdims=True)
            pv = pv / l
        for j in range(stack):
            h = u * stack + j
            pj = pv[j * tq:(j + 1) * tq]
            if h % 2 != g % 2:
                pj = pltpu.roll(pj, HD, 1)
            if h % 2 == 0:
                pending[h // 2] = pj
            else:
                sl = slice((h // 2) * LANES, (h // 2 + 1) * LANES)
                o_ref[:, sl] = jnp.where(lo, pending.pop(h // 2), pj).astype(o_ref.dtype)

    _pipelined(per * len(problems), scores, finish)


ATTN_P_SEQS = 4


def _attn_prompt_kernel(q_ref, kt_ref, v_ref, *rest, group):
    n_w = (len(rest) - 1) // 2
    w_refs, o_ref, wb_refs = rest[:n_w], rest[n_w], rest[n_w + 1:]
    for w_ref, wb_ref in zip(w_refs, wb_refs):
        wb_ref[...] = w_ref[0].astype(BF)

    def tile(g):
        return slice((g // 2) * LANES, (g // 2 + 1) * LANES)

    def problem(s):
        rows = pl.ds(s * SEQ, SEQ)
        return (q_ref.at[rows, :], o_ref.at[rows, :],
                lambda g: [kt_ref[s, 0, tile(g), :].astype(BF)],
                lambda g: [v_ref[pl.ds(s * SEQ, SEQ), tile(g)]])
    _attn_grouped([problem(s) for s in range(ATTN_P_SEQS)], group, group, ones_sum=False)


def _attn_prompt(q, kt_all, v, i, group, weights):
    kvw = v.shape[1]
    rows = ATTN_P_SEQS * SEQ
    steps = BATCH // ATTN_P_SEQS
    w_in, w_out, w_shape = [], [], []
    for w, layer in weights:
        _, r, c = w.shape
        w_in.append(pl.BlockSpec((1, r // steps, c), functools.partial(lambda b, layer: (layer, b, 0),
                                                                        layer=layer)))
        w_out.append(pl.BlockSpec((r // steps, c), lambda b: (b, 0)))
        w_shape.append(jax.ShapeDtypeStruct((r, c), BF))
    return pl.pallas_call(
        functools.partial(_attn_prompt_kernel, group=group),
        grid=(steps,),
        in_specs=[
            pl.BlockSpec((rows, D), lambda b: (b, 0)),
            pl.BlockSpec((ATTN_P_SEQS, 1, kvw, SEQ), lambda b: (b, i, 0, 0)),
            pl.BlockSpec((rows, kvw), lambda b: (b, 0)),
        ] + w_in,
        out_specs=[pl.BlockSpec((rows, D), lambda b: (b, 0))] + w_out,
        out_shape=[jax.ShapeDtypeStruct((NP, D), BF)] + w_shape,
        compiler_params=_params(("arbitrary",)),
        name="attn_prompt",
    )(q, kt_all, v, *[w for w, _ in weights])


def _attn_gqa_sample_kernel(q_ref, kt_ref, v_ref, kct_ref, vct_ref, o_ref):
    vc = vct_ref[0, 0].T.astype(BF)

    def tile(g):
        return slice((g // 2) * LANES, (g // 2 + 1) * LANES)
    _attn_grouped([(q_ref, o_ref,
                    lambda g: [kt_ref[tile(g), :], kct_ref[0, 0, tile(g), :].astype(BF)],
                    lambda g: [v_ref[:, tile(g)], vc[:, tile(g)]])],
                  NH // NKV, NH // NKV, ones_sum=True)


GQA_TQ = 256


def _attn_gqa_sample(q, kt, v, cache_kt, cache_vt, i):
    nq = DEC_SEQ // GQA_TQ
    kvw = NKV * HD
    return pl.pallas_call(
        _attn_gqa_sample_kernel,
        grid=(DEC_BATCH, nq),
        in_specs=[
            pl.BlockSpec((GQA_TQ, D), lambda b, t: (b * nq + t, 0)),
            pl.BlockSpec((kvw, DEC_SEQ), lambda b, t: (0, b)),
            pl.BlockSpec((DEC_SEQ, kvw), lambda b, t: (b, 0)),
            pl.BlockSpec((1, 1, kvw, PAST), lambda b, t: (b, i, 0, 0)),
            pl.BlockSpec((1, 1, kvw, PAST), lambda b, t: (b, i, 0, 0)),
        ],
        out_specs=pl.BlockSpec((GQA_TQ, D), lambda b, t: (b * nq + t, 0)),
        out_shape=jax.ShapeDtypeStruct((NS, D), BF),
        compiler_params=_params(("arbitrary", "arbitrary")),
        name="attn_gqa_sample",
    )(q, kt, v, cache_kt, cache_vt)


NA_QROWS = 4
NA_KBLK = NA_QROWS * GRID_W
NA_NKB = 3
NA_PAIRS = NA_QROWS // 2


def _attn_na_sample_kernel(*refs):
    q_ref = refs[0]
    kt_refs = refs[1:1 + NA_NKB]
    v_refs = refs[1 + NA_NKB:1 + 2 * NA_NKB]
    kct_ref, vct_ref, p_ref, o_ref = refs[1 + 2 * NA_NKB:]
    rb = pl.program_id(1)
    kb0 = jnp.clip(rb - 1, 0, ROWS // NA_QROWS - NA_NKB)
    tq = q_ref.shape[0]
    lo = _lane_lo(tq)
    hi = jnp.logical_not(lo)
    lo_w = _lane_lo(GRID_W)
    vc = vct_ref[0, 0].T.astype(BF)
    ones = jnp.ones((tq, LANES), BF)

    tiles = []
    for i in range(NA_QROWS):
        qr = rb * NA_QROWS + i
        rs = jnp.clip(qr - NA_KH // 2, 0, ROWS - NA_KH)
        for j in range(NA_NKB):
            for jj in range(NA_PAIRS):
                kra = (kb0 + j) * NA_QROWS + 2 * jj
                krb = kra + 1
                idx = jnp.clip(kra - qr + NA_KH, 0, 2 * NA_KH - 1)
                va = jnp.logical_and(kra >= rs, kra < rs + NA_KH).astype(jnp.int32)
                vb = jnp.logical_and(krb >= rs, krb < rs + NA_KH).astype(jnp.int32)
                tiles.append((idx, jnp.where(lo_w, va, vb) > 0))

    def scores(h):
        rows = slice((h // 2) * LANES, (h // 2 + 1) * LANES)
        q2 = q_ref[:, rows]
        qsel = jnp.where(lo if h % 2 == 0 else hi, q2, jnp.zeros_like(q2))
        s_lat = [_dot(qsel, r[rows, :]) for r in kt_refs]
        out_rows = []
        for i in range(NA_QROWS):
            rsl = slice(i * GRID_W, (i + 1) * GRID_W)
            row = []
            for j in range(NA_NKB):
                for jj in range(NA_PAIRS):
                    idx, valid = tiles[(i * NA_NKB + j) * NA_PAIRS + jj]
                    t = s_lat[j][rsl, jj * LANES:(jj + 1) * LANES] + p_ref[0, h, idx]
                    row.append(jnp.where(valid, t, NEG))
            out_rows.append(jnp.concatenate(row, axis=1))
        return jnp.concatenate(out_rows, axis=0), _dot(qsel, kct_ref[0, 0, rows, :].astype(BF))

    def weighted_values(h, s, s_ctx):
        rows = slice((h // 2) * LANES, (h // 2 + 1) * LANES)
        mine = lo if h % 2 == 0 else hi
        m = jnp.maximum(s.max(axis=-1, keepdims=True), s_ctx.max(axis=-1, keepdims=True))
        e = jnp.exp(s - m).astype(BF)
        e_ctx = jnp.exp(s_ctx - m).astype(BF)
        pv = _dot(e_ctx, jnp.where(mine, vc[:, rows], ones))
        for j in range(NA_NKB):
            vj = jnp.where(mine, v_refs[j][:, rows], ones)
            pv = pv + _dot(e[:, j * NA_KBLK:(j + 1) * NA_KBLK], vj)
        return pv / pltpu.roll(pv, HD, 1)

    nxt = scores(0)
    o2 = None
    for h in range(NH):
        cur = nxt
        if h + 1 < NH:
            nxt = scores(h + 1)
        pv = weighted_values(h, *cur)
        if h % 2 == 0:
            o2 = pv
        else:
            o_ref[:, (h // 2) * LANES:(h // 2 + 1) * LANES] = jnp.where(lo, o2, pv).astype(o_ref.dtype)


def _attn_na_sample(q, kt, v, cache_kt, cache_vt, i, ptab):
    nrb = ROWS // NA_QROWS

    def kb(b, r, j):
        return b * nrb + jnp.clip(r - 1, 0, nrb - NA_NKB) + j

    in_specs = [pl.BlockSpec((NA_KBLK, D), lambda b, r: (b * nrb + r, 0))]
    in_specs += [pl.BlockSpec((D, NA_KBLK), functools.partial(lambda b, r, j: (0, kb(b, r, j)), j=j))
                 for j in range(NA_NKB)]
    in_specs += [pl.BlockSpec((NA_KBLK, D), functools.partial(lambda b, r, j: (kb(b, r, j), 0), j=j))
                 for j in range(NA_NKB)]
    in_specs += [
        pl.BlockSpec((1, 1, D, PAST), lambda b, r: (b, i, 0, 0)),
        pl.BlockSpec((1, 1, D, PAST), lambda b, r: (b, i, 0, 0)),
        pl.BlockSpec((1, NH, 2 * NA_KH, GRID_W, LANES), lambda b, r: (i, 0, 0, 0, 0),
                     pipeline_mode=pl.Buffered(1)),
    ]
    return pl.pallas_call(
        _attn_na_sample_kernel,
        grid=(DEC_BATCH, nrb),
        in_specs=in_specs,
        out_specs=pl.BlockSpec((NA_KBLK, D), lambda b, r: (b * nrb + r, 0)),
        out_shape=jax.ShapeDtypeStruct((NS, D), BF),
        compiler_params=_params(("arbitrary", "arbitrary")),
        name="attn_na_sample",
    )(q, *([kt] * NA_NKB), *([v] * NA_NKB), cache_kt, cache_vt, ptab)


N_DC = 2 * NA_KW - 1
N_DR = 2 * NA_KH - 1
N_ENT = N_DR + 1
DC_PAD = 32
N_SPLIT = 3
TILE_ELEMS = GRID_W * LANES


def _na_bias_kernel(lhs_ref, sel_ref, ok_ref, o_ref):
    t = _dot(lhs_ref[0], sel_ref[...])
    ok = ok_ref[...] > 0.0
    for h in range(NH):
        rows = slice(h * N_ENT, (h + 1) * N_ENT)
        o_ref[0, rows, :] = jnp.where(ok, t[rows, :], NEG)


def _na_bias_tables(rpb):
    n_a = rpb.shape[0]
    qc = jnp.arange(GRID_W)[:, None]
    lane = jnp.arange(LANES)[None, :]
    kc = lane % GRID_W
    half = lane // GRID_W
    c_start = jnp.clip(qc - NA_KW // 2, 0, GRID_W - NA_KW)
    ok = (kc >= c_start) & (kc < c_start + NA_KW)
    slot = (half * DC_PAD + kc - qc + (NA_KW - 1)).reshape(-1)
    sel = (jnp.arange(2 * DC_PAD)[:, None] == slot[None, :]) & ok.reshape(1, -1)
    sel = jnp.tile(sel.astype(BF), (N_SPLIT, 1))
    d = jnp.arange(N_ENT)[:, None, None]
    ok_d = ok[None] & jnp.where(half[None] == 0, d >= 1, d <= N_DR - 1)
    ok_d = ok_d.reshape(N_ENT, TILE_ELEMS).astype(F32)
    pad = jnp.zeros((n_a, NH, 1, N_DC), F32)
    left = jnp.concatenate([pad, rpb], axis=2)
    right = jnp.concatenate([rpb, pad], axis=2)
    padc = lambda a: jnp.pad(a, ((0, 0), (0, 0), (0, 0), (0, DC_PAD - N_DC)))
    both = jnp.concatenate([padc(left), padc(right)], axis=-1).reshape(n_a, NH * N_ENT, 2 * DC_PAD)
    pieces = []
    rest = both
    for _ in range(N_SPLIT):
        piece = rest.astype(BF)
        pieces.append(piece)
        rest = rest - piece.astype(F32)
    lhs = jnp.concatenate(pieces, axis=-1)
    k = N_SPLIT * 2 * DC_PAD
    out = pl.pallas_call(
        _na_bias_kernel,
        grid=(n_a,),
        in_specs=[
            pl.BlockSpec((1, NH * N_ENT, k), lambda a: (a, 0, 0)),
            pl.BlockSpec((k, TILE_ELEMS), lambda a: (0, 0)),
            pl.BlockSpec((N_ENT, TILE_ELEMS), lambda a: (0, 0)),
        ],
        out_specs=pl.BlockSpec((1, NH * N_ENT, TILE_ELEMS), lambda a: (a, 0, 0)),
        out_shape=jax.ShapeDtypeStruct((n_a, NH * N_ENT, TILE_ELEMS), F32),
        compiler_params=_params(("arbitrary",)),
        name="na_bias",
    )(lhs, sel, ok_d)
    return out.reshape(n_a, NH, N_ENT, GRID_W, LANES)


HALO_O = 16


def _mix_ffn_kernel(*refs, halo, tiles_per_seq):
    if halo:
        (o_ref, op_ref, on_ref, x_ref, xp_ref, xn_ref, mod_ref, gmix_ref, wo_ref,
         gpre_ref, gpost_ref, wup_ref, cw_ref, cb_ref, wdn_ref, out_ref, act_scr) = refs
    else:
        (o_ref, x_ref, mod_ref, gmix_ref, wo_ref,
         gpre_ref, gpost_ref, wup_ref, cw_ref, cb_ref, wdn_ref, out_ref, act_scr) = refs
    tm = x_ref.shape[0]
    m = mod_ref[0]
    gate_mix = m[:, 2 * D:3 * D]
    sh = m[:, 3 * D:4 * D]
    sc = m[:, 4 * D:5 * D]
    gate = m[:, 5 * D:6 * D]
    zeros = jnp.zeros((SUBLANES, D), F32)

    if halo:
        o_ext = jnp.concatenate([op_ref[...].astype(F32)[HALO_O - SUBLANES:], o_ref[...].astype(F32),
                                 on_ref[...].astype(F32)[:SUBLANES]], axis=0).astype(BF)
        x_ext = jnp.concatenate([xp_ref[...], x_ref[...], xn_ref[...]], axis=0)
    else:
        o_ext = o_ref[...]
        x_ext = x_ref[...]
    x_ext = x_ext + _rms(_dot(o_ext, wo_ref[...]), gate_mix * gmix_ref[...])
    h = _rms(x_ext, gpre_ref[...] * (1.0 + sc)) + sh
    if halo:
        t = pl.program_id(0) % tiles_per_seq
        x = x_ext[SUBLANES:SUBLANES + tm]
        hp = jnp.where(t == 0, zeros, h[:SUBLANES])
        hn = jnp.where(t == tiles_per_seq - 1, zeros, h[SUBLANES + tm:])
        hext = jnp.concatenate([hp, h[SUBLANES:SUBLANES + tm], hn], axis=0).astype(BF)
        starts = [SUBLANES]
        span = tm
    else:
        x = x_ext
        parts = [zeros]
        for s in range(tm // SEQ):
            parts += [h[s * SEQ:(s + 1) * SEQ], zeros]
        hext = jnp.concatenate(parts, axis=0).astype(BF)
        starts = [SUBLANES + s * (SEQ + SUBLANES) for s in range(tm // SEQ)]
        span = SEQ
    mext = hext.shape[0]

    def token_rows(v):
        parts = [v[r0:r0 + span] for r0 in starts]
        return parts[0] if len(parts) == 1 else jnp.concatenate(parts, axis=0)

    def conv(u, c0):
        cols = slice(c0, c0 + FF_CHUNK)
        prev = token_rows(pltpu.roll(u, 1, 0))
        nxt = token_rows(pltpu.roll(u, mext - 1, 0))
        return (prev * cw_ref[0, 0:1, cols] + token_rows(u) * cw_ref[0, 1:2, cols]
                + nxt * cw_ref[0, 2:3, cols] + cb_ref[0, :, cols])

    for j in range(N_CHUNK):
        ca = j * FF_CHUNK
        cg = DFF + j * FF_CHUNK
        a = conv(_dot(hext, wup_ref[:, ca:ca + FF_CHUNK]), ca)
        gt = conv(_dot(hext, wup_ref[:, cg:cg + FF_CHUNK]), cg)
        ha = 0.5 * a
        act_scr[:, ca:ca + FF_CHUNK] = ((ha + ha * jnp.tanh(ha)) * gt).astype(BF)
    y = _dot(act_scr[...], wdn_ref[...])
    out_ref[...] = x + _rms(y, gate * gpost_ref[...])


def _mix_ffn(o, x, modl, mod_map, gmix, w_o, gpre, gpost, wup, cw, cb, wdn, l, *, tm, halo):
    n = x.shape[0]
    tps = DEC_SEQ // tm

    def prev_blk(rows):
        r = tm // rows
        return lambda t: (jnp.maximum(t * r - 1, 0), 0)

    def next_blk(rows):
        r = tm // rows
        return lambda t: (jnp.minimum((t + 1) * r, n // rows - 1), 0)

    in_specs = [pl.BlockSpec((tm, D), lambda t: (t, 0))]
    args = [o]
    if halo:
        in_specs += [pl.BlockSpec((HALO_O, D), prev_blk(HALO_O)), pl.BlockSpec((HALO_O, D), next_blk(HALO_O))]
        args += [o, o]
    in_specs.append(pl.BlockSpec((tm, D), lambda t: (t, 0)))
    args.append(x)
    if halo:
        in_specs += [pl.BlockSpec((SUBLANES, D), prev_blk(SUBLANES)),
                     pl.BlockSpec((SUBLANES, D), next_blk(SUBLANES))]
        args += [x, x]
    const = dict(pipeline_mode=pl.Buffered(1))
    in_specs += [
        pl.BlockSpec((1, 1, 6 * D), mod_map),
        pl.BlockSpec((1, D), lambda t: (0, 0)),
        pl.BlockSpec((D, D), lambda t: (0, 0), **const),
        pl.BlockSpec((1, D), lambda t: (0, 0)),
        pl.BlockSpec((1, D), lambda t: (0, 0)),
        pl.BlockSpec((D, 2 * DFF), lambda t: (0, 0), **const),
        pl.BlockSpec((1, 3, 2 * DFF), lambda t: (l, 0, 0)),
        pl.BlockSpec((1, 1, 2 * DFF), lambda t: (l, 0, 0)),
        pl.BlockSpec((DFF, D), lambda t: (0, 0), **const),
    ]
    args += [modl, gmix.reshape(1, D), w_o, gpre.reshape(1, D), gpost.reshape(1, D), wup, cw, cb, wdn]
    return pl.pallas_call(
        functools.partial(_mix_ffn_kernel, halo=halo, tiles_per_seq=tps),
        grid=(n // tm,),
        in_specs=in_specs,
        out_specs=pl.BlockSpec((tm, D), lambda t: (t, 0)),
        out_shape=jax.ShapeDtypeStruct((n, D), F32),
        scratch_shapes=[pltpu.VMEM((tm, DFF), BF)],
        compiler_params=_params(("arbitrary",)),
        name="mix_ffn",
    )(*args)


def _rope_tables():
    t = jnp.arange(DEC_SEQ)
    rows = (t // GRID_W).astype(F32)
    cols = (t % GRID_W).astype(F32)
    quarter = HD // 4
    freqs = ROPE_BASE ** (-jnp.arange(quarter, dtype=F32) / quarter)
    lane = jnp.arange(LANES)
    d = lane % HD
    e = d % (HD // 2)
    is_x2 = (e >= quarter)[None, :]
    pos = jnp.where((d // (HD // 2) == 0)[None, :], rows[:, None], cols[:, None])
    ang = pos * freqs[e % quarter][None, :]
    cos = jnp.cos(ang)
    sin = jnp.sin(ang)
    return cos, jnp.where(is_x2, 0.0, -sin), jnp.where(is_x2, sin, 0.0)


def _block_diag_ones():
    r = jnp.arange(256)
    return (r[:, None] // HD == r[None, :] // HD).astype(BF)


def _cache_t(cache):
    b, n, p, h, d = cache.shape
    return cache.transpose(0, 1, 3, 4, 2).reshape(b, n, h * d, p)


def _cache_out(kt, heads):
    b, n, _, s = kt.shape
    return kt.reshape(b, n, heads, HD, s).transpose(0, 1, 4, 2, 3)


def kernel(x_prompt, x_sample, cache_na_k, cache_na_v, cache_gqa_k, cache_gqa_v, c, c_ctx,
           ada_w, ada_b, norm_mix_pre, norm_mix_post, norm_ffn_pre, norm_ffn_post,
           na_w_qkv, na_w_o, na_rpb, gqa_w_qkv, gqa_w_o, gqa_q_norm, gqa_k_norm,
           ffn_w_up, ffn_conv_w, ffn_conv_b, ffn_w_down):
    n_a = cache_na_k.shape[1]
    n_b = cache_gqa_k.shape[1]
    cond8 = jnp.concatenate([c_ctx[None], c, jnp.zeros((SUBLANES - 1 - DEC_BATCH, D), F32)], axis=0)
    mod = _adaln(cond8, ada_w, ada_b)

    tm_p, tm_s = 512, 512
    tm_fp, tm_fs = 512, 512
    map_p = lambda t: (0, 0, 0)
    map_s = lambda t: (1 + t // (DEC_SEQ // tm_s), 0, 0)
    map_fs = lambda t: (1 + t // (DEC_SEQ // tm_fs), 0, 0)

    cna_kt, cna_vt = _cache_t(cache_na_k), _cache_t(cache_na_v)
    cg_kt, cg_vt = _cache_t(cache_gqa_k), _cache_t(cache_gqa_v)
    rope_tabs = _rope_tables()
    bd = _block_diag_ones()
    ptabs = _na_bias_tables(na_rpb)
    kvg = NKV * HD

    w_na, w_gq = na_w_qkv.astype(BF), gqa_w_qkv.astype(BF)
    cb = ffn_conv_b.reshape(DEPTH, 1, 2 * DFF)

    xp = x_prompt.reshape(NP, D)
    xs = x_sample.reshape(NS, D)
    na_kv = None
    gq_kv = None
    for l in range(DEPTH):
        i = l // 2
        modl = mod[l].reshape(SUBLANES, 1, 6 * D)
        if l % 2 == 0:
            qp, vp, kt, vt = _qkv_prompt(xp, modl, norm_mix_pre[l], w_na, i, n_a, na_kv,
                                         gqa=False, tm=tm_p)
            na_kv = (kt, vt)
            qs, kts, vs = _qkv_sample_na(xs, modl, map_s, norm_mix_pre[l], w_na, i, tm=tm_s)
            op, w_o, wup, wdn = _attn_prompt(qp, kt, vp, i, 1,
                                             [(na_w_o, i), (ffn_w_up, l), (ffn_w_down, l)])
            os_ = _attn_na_sample(qs, kts, vs, cna_kt, cna_vt, i, ptabs)
        else:
            qg = (jnp.tile(gqa_q_norm[i], NH) * 0.125).reshape(1, D)
            kg = jnp.tile(gqa_k_norm[i], NKV)
            qkg = jnp.concatenate([qg[0], kg]).reshape(1, D + kvg)
            kg2 = jnp.broadcast_to(kg[:, None], (kvg, SEQ))
            qp, vp, kt, vt = _qkv_prompt(xp, modl, norm_mix_pre[l], w_gq, i, n_b, gq_kv,
                                         gqa=True, tm=tm_p, extra=(bd, qg, kg2))
            gq_kv = (kt, vt)
            qs, kts, vs = _qkv_sample_gqa(xs, modl, map_s, norm_mix_pre[l], w_gq, i, bd, qkg,
                                          rope_tabs, tm=tm_s)
            op, w_o, wup, wdn = _attn_prompt(qp, kt, vp, i, NH // NKV,
                                             [(gqa_w_o, i), (ffn_w_up, l), (ffn_w_down, l)])
            os_ = _attn_gqa_sample(qs, kts, vs, cg_kt, cg_vt, i)
        xp = _mix_ffn(op, xp, modl, map_p, norm_mix_post[l], w_o, norm_ffn_pre[l], norm_ffn_post[l],
                      wup, ffn_conv_w, cb, wdn, l, tm=tm_fp, halo=False)
        xs = _mix_ffn(os_, xs, modl, map_fs, norm_mix_post[l], w_o, norm_ffn_pre[l], norm_ffn_post[l],
                      wup, ffn_conv_w, cb, wdn, l, tm=tm_fs, halo=True)

    return (xp.reshape(BATCH, SEQ, D), xs.reshape(DEC_BATCH, DEC_SEQ, D),
            _cache_out(na_kv[0], NH), _cache_out(na_kv[1], NH),
            _cache_out(gq_kv[0], NKV), _cache_out(gq_kv[1], NKV))
```

```python
import functools

import jax
import jax.numpy as jnp
from jax import lax
from jax.experimental import pallas as pl
from jax.experimental.pallas import tpu as pltpu

D = 1024
HD = 64
NH = 16
NKV = 4
DFF = 2816
DEPTH = 4
GRID_W = 64
ROWS = 32
NA_KH = 8
NA_KW = 16
ROPE_BASE = 10000.0
EPS = 1e-6
BATCH, SEQ = 32, 256
DEC_BATCH, DEC_SEQ = 2, 2048
PAST = 256
NP = BATCH * SEQ
NS = DEC_BATCH * DEC_SEQ

BF = jnp.bfloat16
F32 = jnp.float32
NEG = -1e30
LANES = 128
SUBLANES = 8
FF_CHUNK = 256
N_CHUNK = DFF // FF_CHUNK
VMEM_LIMIT = 50 * 1024 * 1024


def _params(sem):
    return pltpu.CompilerParams(dimension_semantics=sem, vmem_limit_bytes=VMEM_LIMIT)


def _rms(x, g):
    return x * lax.rsqrt(jnp.mean(x * x, axis=-1, keepdims=True) + EPS) * g


def _dot(a, b):
    return jnp.dot(a, b, preferred_element_type=F32)


def _dot_tn_nt(a, b):
    return lax.dot_general(a, b, (((0,), (1,)), ((), ())), preferred_element_type=F32)


def _qkv_weight_specs(gqa):
    if gqa:
        kvw = NKV * HD
        return [pl.BlockSpec((1, D, D), lambda t: (0, 0, 0)),
                pl.BlockSpec((1, D, kvw), lambda t: (0, 0, D // kvw)),
                pl.BlockSpec((1, D, kvw), lambda t: (0, 0, D // kvw + 1))]
    return [pl.BlockSpec((1, D, D), functools.partial(lambda t, c: (0, 0, c), c=c)) for c in range(3)]


def _mod_kernel(cond_ref, w_ref, b_ref, o_ref):
    s = cond_ref[...]
    s = s * jax.nn.sigmoid(s)
    o_ref[0] = _dot(s.astype(BF), w_ref[0].astype(BF)) + b_ref[0]


def _adaln(cond8, ada_w, ada_b):
    tn = 1536
    return pl.pallas_call(
        _mod_kernel,
        grid=(DEPTH, 6 * D // tn),
        in_specs=[
            pl.BlockSpec((SUBLANES, D), lambda l, n: (0, 0)),
            pl.BlockSpec((1, D, tn), lambda l, n: (l, 0, n)),
            pl.BlockSpec((1, 1, tn), lambda l, n: (l, 0, n)),
        ],
        out_specs=pl.BlockSpec((1, SUBLANES, tn), lambda l, n: (l, 0, n)),
        out_shape=jax.ShapeDtypeStruct((DEPTH, SUBLANES, 6 * D), F32),
        compiler_params=_params(("arbitrary", "arbitrary")),
        name="adaln",
    )(cond8, ada_w, ada_b.reshape(DEPTH, 1, 6 * D))


def _prenorm(x_ref, mod_ref, g_ref):
    m = mod_ref[0]
    return (_rms(x_ref[...], g_ref[...] * (1.0 + m[:, D:2 * D])) + m[:, 0:D]).astype(BF)


def _head_rms_lanes(blk, bd_ref):
    ss = _dot((blk * blk).astype(BF), bd_ref[...])
    return blk * lax.rsqrt(ss * (1.0 / HD) + EPS)


def _qkv_prompt_kernel(*refs, gqa, slot):
    if gqa:
        (x_ref, mod_ref, g_ref, wq_ref, wk_ref, wv_ref, bd_ref, qg_ref, kg_ref,
         q_ref, v_ref, kt_ref, vt_ref) = refs
    else:
        x_ref, mod_ref, g_ref, wq_ref, wk_ref, wv_ref, q_ref, v_ref, kt_ref, vt_ref = refs
    hb = _prenorm(x_ref, mod_ref, g_ref)
    q = _dot(hb, wq_ref[0])
    v = _dot(hb, wv_ref[0])
    kt = _dot_tn_nt(wk_ref[0], hb)
    if gqa:
        for t in range(D // 256):
            sl = slice(t * 256, (t + 1) * 256)
            q_ref[:, sl] = (_head_rms_lanes(q[:, sl], bd_ref) * qg_ref[:, sl]).astype(q_ref.dtype)
        heads = []
        for h in range(NKV):
            blk = kt[h * HD:(h + 1) * HD, :]
            heads.append(blk * lax.rsqrt(jnp.mean(blk * blk, axis=0, keepdims=True) + EPS))
        kt = jnp.concatenate(heads, axis=0)
    else:
        q_ref[...] = (q * 0.125).astype(q_ref.dtype)
    v_ref[...] = v.astype(v_ref.dtype)
    for s in range(x_ref.shape[0] // SEQ):
        sl = slice(s * SEQ, (s + 1) * SEQ)
        kts = kt[:, sl]
        if gqa:
            kts = kts * kg_ref[...]
        kt_ref[s, slot] = kts
        vt_ref[s, slot] = v[sl, :].T
        for other in range(kt_ref.shape[1]):
            if other != slot:
                kt_ref[s, other] = jnp.zeros_like(kts)
                vt_ref[s, other] = jnp.zeros_like(kts)


def _qkv_prompt(x, modl, g, w, i, n_layers, prev, *, gqa, tm, extra=()):
    kvw = NKV * HD if gqa else D
    in_specs = [
        pl.BlockSpec((tm, D), lambda t: (t, 0)),
        pl.BlockSpec((1, 1, 6 * D), lambda t: (0, 0, 0)),
        pl.BlockSpec((1, D), lambda t: (0, 0)),
    ] + _qkv_weight_specs(gqa)
    args = [x, modl, g.reshape(1, D), w, w, w]
    if gqa:
        in_specs += [pl.BlockSpec((256, 256), lambda t: (0, 0)),
                     pl.BlockSpec((1, D), lambda t: (0, 0)),
                     pl.BlockSpec((kvw, SEQ), lambda t: (0, 0))]
        args += list(extra)
    n_in = len(args)
    aliases = {}
    if prev is None:
        kv_spec = pl.BlockSpec((tm // SEQ, n_layers, kvw, SEQ), lambda t: (t, 0, 0, 0))
        slot = i
    else:
        in_specs += [pl.BlockSpec(memory_space=pl.ANY), pl.BlockSpec(memory_space=pl.ANY)]
        args += list(prev)
        aliases = {n_in: 2, n_in + 1: 3}
        kv_spec = pl.BlockSpec((tm // SEQ, 1, kvw, SEQ), lambda t: (t, i, 0, 0))
        slot = 0
    kv_shape = jax.ShapeDtypeStruct((BATCH, n_layers, kvw, SEQ), F32)

    def body(*refs):
        _qkv_prompt_kernel(*(refs[:n_in] + refs[len(args):]), gqa=gqa, slot=slot)

    return pl.pallas_call(
        body,
        grid=(NP // tm,),
        in_specs=in_specs,
        out_specs=[
            pl.BlockSpec((tm, D), lambda t: (t, 0)),
            pl.BlockSpec((tm, kvw), lambda t: (t, 0)),
            kv_spec, kv_spec,
        ],
        out_shape=[
            jax.ShapeDtypeStruct((NP, D), BF),
            jax.ShapeDtypeStruct((NP, kvw), BF),
            kv_shape, kv_shape,
        ],
        input_output_aliases=aliases,
        compiler_params=_params(("arbitrary",)),
        name="qkv_prompt_gqa" if gqa else "qkv_prompt_na",
    )(*args)


def _qkv_sample_na_kernel(x_ref, mod_ref, g_ref, wq_ref, wk_ref, wv_ref, q_ref, kt_ref, v_ref):
    hb = _prenorm(x_ref, mod_ref, g_ref)
    q_ref[...] = (_dot(hb, wq_ref[0]) * 0.125).astype(q_ref.dtype)
    v_ref[...] = _dot(hb, wv_ref[0]).astype(v_ref.dtype)
    kt_ref[...] = _dot_tn_nt(wk_ref[0], hb).astype(kt_ref.dtype)


def _qkv_sample_na(x, modl, mod_map, g, w, *, tm):
    return pl.pallas_call(
        _qkv_sample_na_kernel,
        grid=(NS // tm,),
        in_specs=[
            pl.BlockSpec((tm, D), lambda t: (t, 0)),
            pl.BlockSpec((1, 1, 6 * D), mod_map),
            pl.BlockSpec((1, D), lambda t: (0, 0)),
        ] + _qkv_weight_specs(False),
        out_specs=[
            pl.BlockSpec((tm, D), lambda t: (t, 0)),
            pl.BlockSpec((D, tm), lambda t: (0, t)),
            pl.BlockSpec((tm, D), lambda t: (t, 0)),
        ],
        out_shape=[
            jax.ShapeDtypeStruct((NS, D), BF),
            jax.ShapeDtypeStruct((D, NS), BF),
            jax.ShapeDtypeStruct((NS, D), BF),
        ],
        compiler_params=_params(("arbitrary",)),
        name="qkv_sample_na",
    )(x, modl, g.reshape(1, D), w, w, w)


def _qkv_sample_gqa_kernel(x_ref, mod_ref, g_ref, w_ref, bd_ref, qkg_ref, cos_ref, s1_ref, s2_ref,
                           q_ref, kt_ref, v_ref):
    qkv = _dot(_prenorm(x_ref, mod_ref, g_ref), w_ref[0])
    nqk = D + NKV * HD
    for t in range(nqk // LANES):
        if t % 2 == 0:
            nrm = _head_rms_lanes(qkv[:, (t // 2) * 256:(t // 2 + 1) * 256], bd_ref)
        xt = nrm[:, (t % 2) * LANES:(t % 2 + 1) * LANES] * qkg_ref[:, t * LANES:(t + 1) * LANES]
        xt = (xt * cos_ref[...]
              + pltpu.roll(xt, LANES - 16, 1) * s1_ref[...]
              + pltpu.roll(xt, 16, 1) * s2_ref[...])
        if t < D // LANES:
            q_ref[:, t * LANES:(t + 1) * LANES] = xt.astype(q_ref.dtype)
        else:
            tk = t - D // LANES
            kt_ref[tk * LANES:(tk + 1) * LANES, :] = xt.T.astype(kt_ref.dtype)
    v_ref[...] = qkv[:, nqk:].astype(v_ref.dtype)


def _qkv_sample_gqa(x, modl, mod_map, g, w, bd, qkg, rope_tabs, *, tm):
    nq = w.shape[2]
    kvw = NKV * HD
    tps = DEC_SEQ // tm
    in_specs = [
        pl.BlockSpec((tm, D), lambda t: (t, 0)),
        pl.BlockSpec((1, 1, 6 * D), mod_map),
        pl.BlockSpec((1, D), lambda t: (0, 0)),
        pl.BlockSpec((1, D, nq), lambda t: (0, 0, 0)),
        pl.BlockSpec((256, 256), lambda t: (0, 0)),
        pl.BlockSpec((1, D + kvw), lambda t: (0, 0)),
    ]
    in_specs += [pl.BlockSpec((tm, LANES), lambda t: (t % tps, 0)) for _ in rope_tabs]
    return pl.pallas_call(
        _qkv_sample_gqa_kernel,
        grid=(NS // tm,),
        in_specs=in_specs,
        out_specs=[
            pl.BlockSpec((tm, D), lambda t: (t, 0)),
            pl.BlockSpec((kvw, tm), lambda t: (0, t)),
            pl.BlockSpec((tm, kvw), lambda t: (t, 0)),
        ],
        out_shape=[
            jax.ShapeDtypeStruct((NS, D), BF),
            jax.ShapeDtypeStruct((kvw, NS), BF),
            jax.ShapeDtypeStruct((NS, kvw), BF),
        ],
        compiler_params=_params(("arbitrary",)),
        name="qkv_sample_gqa",
    )(x, modl, g.reshape(1, D), w, bd, qkg, *rope_tabs)


def _lane_lo(rows):
    return lax.broadcasted_iota(jnp.int32, (rows, LANES), 1) < HD


def _pipelined(n, first, second):
    nxt = first(0)
    for i in range(n):
        cur = nxt
        if i + 1 < n:
            nxt = first(i + 1)
        second(i, cur)


def _attn_grouped(problems, group, stack, *, ones_sum):
    tq = problems[0][0].shape[0]
    lo = _lane_lo(tq)
    lo_s = _lane_lo(stack * tq)
    pending = {}
    per = NH // stack

    def scores(n):
        q_ref, _, key_tiles, _ = problems[n // per]
        u = n % per
        g = u * stack // group
        parts = []
        for h in range(u * stack, (u + 1) * stack):
            q2 = q_ref[:, (h // 2) * LANES:(h // 2 + 1) * LANES]
            if h % 2 != g % 2:
                q2 = pltpu.roll(q2.astype(F32), HD, 1).astype(BF)
            parts.append(q2)
        qs = parts[0] if stack == 1 else jnp.concatenate(parts, axis=0)
        mine = lo_s if g % 2 == 0 else jnp.logical_not(lo_s)
        qsel = jnp.where(mine, qs, jnp.zeros_like(qs))
        return [_dot(qsel, kt) for kt in key_tiles(g)]

    def finish(n, ss):
        _, o_ref, _, value_tiles = problems[n // per]
        u = n % per
        g = u * stack // group
        m = ss[0].max(axis=-1, keepdims=True)
        for s in ss[1:]:
            m = jnp.maximum(m, s.max(axis=-1, keepdims=True))
        es = [jnp.exp(s - m) for s in ss]
        pv = None
        for e, v in zip(es, value_tiles(g)):
            if ones_sum:
                mine = _lane_lo(v.shape[0])
                if g % 2:
                    mine = jnp.logical_not(mine)
                v = jnp.where(mine, v, jnp.ones_like(v))
            d = _dot(e.astype(BF), v)
            pv = d if pv is None else pv + d
        if ones_sum:
            pv = pv / pltpu.roll(pv, HD, 1)
        else:
            l = es[0].sum(axis=-1, keepdims=True)
            for e in es[1:]:
                l = l + e.sum(axis=-1, keepdims=True)
            pv = pv / l
        for j in range(stack):
            h = u * stack + j
            pj = pv[j * tq:(j + 1) * tq]
            if h % 2 != g % 2:
                pj = pltpu.roll(pj, HD, 1)
            if h % 2 == 0:
                pending[h // 2] = pj
            else:
                sl = slice((h // 2) * LANES, (h // 2 + 1) * LANES)
                o_ref[:, sl] = jnp.where(lo, pending.pop(h // 2), pj).astype(o_ref.dtype)

    _pipelined(per * len(problems), scores, finish)


ATTN_P_SEQS = 4


def _attn_prompt_kernel(q_ref, kt_ref, v_ref, *rest, group):
    n_w = (len(rest) - 1) // 2
    o_ref = rest[n_w]
    _cast_slabs(rest[:n_w], rest[n_w + 1:])

    def tile(g):
        return slice((g // 2) * LANES, (g // 2 + 1) * LANES)

    def problem(s):
        rows = pl.ds(s * SEQ, SEQ)
        return (q_ref.at[rows, :], o_ref.at[rows, :],
                lambda g: [kt_ref[s, 0, tile(g), :].astype(BF)],
                lambda g: [v_ref[pl.ds(s * SEQ, SEQ), tile(g)]])
    _attn_grouped([problem(s) for s in range(ATTN_P_SEQS)], group, group, ones_sum=False)


def _attn_prompt(q, kt_all, v, i, group, weights):
    kvw = v.shape[1]
    rows = ATTN_P_SEQS * SEQ
    steps = BATCH // ATTN_P_SEQS
    w_in, w_out, w_shape = _cast_specs(weights, steps, lambda b: b)
    return pl.pallas_call(
        functools.partial(_attn_prompt_kernel, group=group),
        grid=(steps,),
        in_specs=[
            pl.BlockSpec((rows, D), lambda b: (b, 0)),
            pl.BlockSpec((ATTN_P_SEQS, 1, kvw, SEQ), lambda b: (b, i, 0, 0)),
            pl.BlockSpec((rows, kvw), lambda b: (b, 0)),
        ] + w_in,
        out_specs=[pl.BlockSpec((rows, D), lambda b: (b, 0))] + w_out,
        out_shape=[jax.ShapeDtypeStruct((NP, D), BF)] + w_shape,
        compiler_params=_params(("arbitrary",)),
        name="attn_prompt",
    )(q, kt_all, v, *[w for w, _ in weights])


def _cast_specs(weights, steps, slab):
    w_in, w_out, w_shape = [], [], []
    for w, layer in weights:
        _, r, c = w.shape
        w_in.append(pl.BlockSpec((1, r // steps, c),
                                 functools.partial(lambda *ids, layer: (layer, slab(*ids), 0), layer=layer)))
        w_out.append(pl.BlockSpec((r // steps, c), lambda *ids: (slab(*ids), 0)))
        w_shape.append(jax.ShapeDtypeStruct((r, c), BF))
    return w_in, w_out, w_shape


def _cast_slabs(w_refs, wb_refs):
    for w_ref, wb_ref in zip(w_refs, wb_refs):
        wb_ref[...] = w_ref[0].astype(BF)


def _attn_gqa_sample_kernel(q_ref, kt_ref, v_ref, kct_ref, vct_ref, *rest):
    n_w = (len(rest) - 1) // 2
    o_ref = rest[n_w]
    _cast_slabs(rest[:n_w], rest[n_w + 1:])
    vc = vct_ref[0, 0].T.astype(BF)

    def tile(g):
        return slice((g // 2) * LANES, (g // 2 + 1) * LANES)
    _attn_grouped([(q_ref, o_ref,
                    lambda g: [kt_ref[tile(g), :], kct_ref[0, 0, tile(g), :].astype(BF)],
                    lambda g: [v_ref[:, tile(g)], vc[:, tile(g)]])],
                  NH // NKV, NH // NKV, ones_sum=True)


GQA_TQ = 256


def _attn_gqa_sample(q, kt, v, cache_kt, cache_vt, i, weights=()):
    nq = DEC_SEQ // GQA_TQ
    kvw = NKV * HD
    w_in, w_out, w_shape = _cast_specs(weights, DEC_BATCH * nq, lambda b, t: b * nq + t)
    return pl.pallas_call(
        _attn_gqa_sample_kernel,
        grid=(DEC_BATCH, nq),
        in_specs=[
            pl.BlockSpec((GQA_TQ, D), lambda b, t: (b * nq + t, 0)),
            pl.BlockSpec((kvw, DEC_SEQ), lambda b, t: (0, b)),
            pl.BlockSpec((DEC_SEQ, kvw), lambda b, t: (b, 0)),
            pl.BlockSpec((1, 1, kvw, PAST), lambda b, t: (b, i, 0, 0)),
            pl.BlockSpec((1, 1, kvw, PAST), lambda b, t: (b, i, 0, 0)),
        ] + w_in,
        out_specs=[pl.BlockSpec((GQA_TQ, D), lambda b, t: (b * nq + t, 0))] + w_out,
        out_shape=[jax.ShapeDtypeStruct((NS, D), BF)] + w_shape,
        compiler_params=_params(("arbitrary", "arbitrary")),
        name="attn_gqa_sample",
    )(q, kt, v, cache_kt, cache_vt, *[w for w, _ in weights])


NA_QROWS = 4
NA_KBLK = NA_QROWS * GRID_W
NA_NKB = 3
NA_PAIRS = NA_QROWS // 2


def _attn_na_sample_kernel(*refs):
    q_ref = refs[0]
    kt_refs = refs[1:1 + NA_NKB]
    v_refs = refs[1 + NA_NKB:1 + 2 * NA_NKB]
    kct_ref, vct_ref, p_ref = refs[1 + 2 * NA_NKB:4 + 2 * NA_NKB]
    rest = refs[4 + 2 * NA_NKB:]
    n_w = (len(rest) - 1) // 2
    o_ref = rest[n_w]
    _cast_slabs(rest[:n_w], rest[n_w + 1:])
    rb = pl.program_id(1)
    kb0 = jnp.clip(rb - 1, 0, ROWS // NA_QROWS - NA_NKB)
    tq = q_ref.shape[0]
    lo = _lane_lo(tq)
    hi = jnp.logical_not(lo)
    lo_w = _lane_lo(GRID_W)
    vc = vct_ref[0, 0].T.astype(BF)
    ones = jnp.ones((tq, LANES), BF)

    tiles = []
    for i in range(NA_QROWS):
        qr = rb * NA_QROWS + i
        rs = jnp.clip(qr - NA_KH // 2, 0, ROWS - NA_KH)
        for j in range(NA_NKB):
            for jj in range(NA_PAIRS):
                kra = (kb0 + j) * NA_QROWS + 2 * jj
                krb = kra + 1
                idx = jnp.clip(kra - qr + NA_KH, 0, 2 * NA_KH - 1)
                va = jnp.logical_and(kra >= rs, kra < rs + NA_KH).astype(jnp.int32)
                vb = jnp.logical_and(krb >= rs, krb < rs + NA_KH).astype(jnp.int32)
                tiles.append((idx, jnp.where(lo_w, va, vb) > 0))

    def scores(h):
        rows = slice((h // 2) * LANES, (h // 2 + 1) * LANES)
        q2 = q_ref[:, rows]
        qsel = jnp.where(lo if h % 2 == 0 else hi, q2, jnp.zeros_like(q2))
        s_lat = [_dot(qsel, r[rows, :]) for r in kt_refs]
        out_rows = []
        for i in range(NA_QROWS):
            rsl = slice(i * GRID_W, (i + 1) * GRID_W)
            row = []
            for j in range(NA_NKB):
                for jj in range(NA_PAIRS):
                    idx, valid = tiles[(i * NA_NKB + j) * NA_PAIRS + jj]
                    t = s_lat[j][rsl, jj * LANES:(jj + 1) * LANES] + p_ref[0, h, idx]
                    row.append(jnp.where(valid, t, NEG))
            out_rows.append(jnp.concatenate(row, axis=1))
        return jnp.concatenate(out_rows, axis=0), _dot(qsel, kct_ref[0, 0, rows, :].astype(BF))

    def weighted_values(h, s, s_ctx):
        rows = slice((h // 2) * LANES, (h // 2 + 1) * LANES)
        mine = lo if h % 2 == 0 else hi
        m = jnp.maximum(s.max(axis=-1, keepdims=True), s_ctx.max(axis=-1, keepdims=True))
        e = jnp.exp(s - m).astype(BF)
        e_ctx = jnp.exp(s_ctx - m).astype(BF)
        pv = _dot(e_ctx, jnp.where(mine, vc[:, rows], ones))
        for j in range(NA_NKB):
            vj = jnp.where(mine, v_refs[j][:, rows], ones)
            pv = pv + _dot(e[:, j * NA_KBLK:(j + 1) * NA_KBLK], vj)
        return pv / pltpu.roll(pv, HD, 1)

    nxt = scores(0)
    o2 = None
    for h in range(NH):
        cur = nxt
        if h + 1 < NH:
            nxt = scores(h + 1)
        pv = weighted_values(h, *cur)
        if h % 2 == 0:
            o2 = pv
        else:
            o_ref[:, (h // 2) * LANES:(h // 2 + 1) * LANES] = jnp.where(lo, o2, pv).astype(o_ref.dtype)


def _attn_na_sample(q, kt, v, cache_kt, cache_vt, i, ptab, weights=()):
    nrb = ROWS // NA_QROWS
    w_in, w_out, w_shape = _cast_specs(weights, DEC_BATCH * nrb, lambda b, r: b * nrb + r)

    def kb(b, r, j):
        return b * nrb + jnp.clip(r - 1, 0, nrb - NA_NKB) + j

    in_specs = [pl.BlockSpec((NA_KBLK, D), lambda b, r: (b * nrb + r, 0))]
    in_specs += [pl.BlockSpec((D, NA_KBLK), functools.partial(lambda b, r, j: (0, kb(b, r, j)), j=j))
                 for j in range(NA_NKB)]
    in_specs += [pl.BlockSpec((NA_KBLK, D), functools.partial(lambda b, r, j: (kb(b, r, j), 0), j=j))
                 for j in range(NA_NKB)]
    in_specs += [
        pl.BlockSpec((1, 1, D, PAST), lambda b, r: (b, i, 0, 0)),
        pl.BlockSpec((1, 1, D, PAST), lambda b, r: (b, i, 0, 0)),
        pl.BlockSpec((1, NH, 2 * NA_KH, GRID_W, LANES), lambda b, r: (i, 0, 0, 0, 0),
                     pipeline_mode=pl.Buffered(1)),
    ]
    return pl.pallas_call(
        _attn_na_sample_kernel,
        grid=(DEC_BATCH, nrb),
        in_specs=in_specs + w_in,
        out_specs=[pl.BlockSpec((NA_KBLK, D), lambda b, r: (b * nrb + r, 0))] + w_out,
        out_shape=[jax.ShapeDtypeStruct((NS, D), BF)] + w_shape,
        compiler_params=_params(("arbitrary", "arbitrary")),
        name="attn_na_sample",
    )(q, *([kt] * NA_NKB), *([v] * NA_NKB), cache_kt, cache_vt, ptab, *[w for w, _ in weights])


N_DC = 2 * NA_KW - 1
N_DR = 2 * NA_KH - 1
N_ENT = N_DR + 1
DC_PAD = 32
N_SPLIT = 3
TILE_ELEMS = GRID_W * LANES


def _na_bias_kernel(lhs_ref, sel_ref, ok_ref, o_ref):
    t = _dot(lhs_ref[0], sel_ref[...])
    ok = ok_ref[...] > 0.0
    for h in range(NH):
        rows = slice(h * N_ENT, (h + 1) * N_ENT)
        o_ref[0, rows, :] = jnp.where(ok, t[rows, :], NEG)


def _na_bias_tables(rpb):
    n_a = rpb.shape[0]
    qc = jnp.arange(GRID_W)[:, None]
    lane = jnp.arange(LANES)[None, :]
    kc = lane % GRID_W
    half = lane // GRID_W
    c_start = jnp.clip(qc - NA_KW // 2, 0, GRID_W - NA_KW)
    ok = (kc >= c_start) & (kc < c_start + NA_KW)
    slot = (half * DC_PAD + kc - qc + (NA_KW - 1)).reshape(-1)
    sel = (jnp.arange(2 * DC_PAD)[:, None] == slot[None, :]) & ok.reshape(1, -1)
    sel = jnp.tile(sel.astype(BF), (N_SPLIT, 1))
    d = jnp.arange(N_ENT)[:, None, None]
    ok_d = ok[None] & jnp.where(half[None] == 0, d >= 1, d <= N_DR - 1)
    ok_d = ok_d.reshape(N_ENT, TILE_ELEMS).astype(F32)
    pad = jnp.zeros((n_a, NH, 1, N_DC), F32)
    left = jnp.concatenate([pad, rpb], axis=2)
    right = jnp.concatenate([rpb, pad], axis=2)
    padc = lambda a: jnp.pad(a, ((0, 0), (0, 0), (0, 0), (0, DC_PAD - N_DC)))
    both = jnp.concatenate([padc(left), padc(right)], axis=-1).reshape(n_a, NH * N_ENT, 2 * DC_PAD)
    pieces = []
    rest = both
    for _ in range(N_SPLIT):
        piece = rest.astype(BF)
        pieces.append(piece)
        rest = rest - piece.astype(F32)
    lhs = jnp.concatenate(pieces, axis=-1)
    k = N_SPLIT * 2 * DC_PAD
    out = pl.pallas_call(
        _na_bias_kernel,
        grid=(n_a,),
        in_specs=[
            pl.BlockSpec((1, NH * N_ENT, k), lambda a: (a, 0, 0)),
            pl.BlockSpec((k, TILE_ELEMS), lambda a: (0, 0)),
            pl.BlockSpec((N_ENT, TILE_ELEMS), lambda a: (0, 0)),
        ],
        out_specs=pl.BlockSpec((1, NH * N_ENT, TILE_ELEMS), lambda a: (a, 0, 0)),
        out_shape=jax.ShapeDtypeStruct((n_a, NH * N_ENT, TILE_ELEMS), F32),
        compiler_params=_params(("arbitrary",)),
        name="na_bias",
    )(lhs, sel, ok_d)
    return out.reshape(n_a, NH, N_ENT, GRID_W, LANES)


HALO_O = 16


def _mix_ffn_kernel(*refs, halo, tiles_per_seq):
    if halo:
        (o_ref, op_ref, on_ref, x_ref, xp_ref, xn_ref, mod_ref, gmix_ref, wo_ref,
         gpre_ref, gpost_ref, wup_ref, cw_ref, cb_ref, wdn_ref, out_ref, act_scr) = refs
    else:
        (o_ref, x_ref, mod_ref, gmix_ref, wo_ref,
         gpre_ref, gpost_ref, wup_ref, cw_ref, cb_ref, wdn_ref, out_ref, act_scr) = refs
    tm = x_ref.shape[0]
    m = mod_ref[0]
    gate_mix = m[:, 2 * D:3 * D]
    sh = m[:, 3 * D:4 * D]
    sc = m[:, 4 * D:5 * D]
    gate = m[:, 5 * D:6 * D]
    zeros = jnp.zeros((SUBLANES, D), F32)

    if halo:
        o_ext = jnp.concatenate([op_ref[...].astype(F32)[HALO_O - SUBLANES:], o_ref[...].astype(F32),
                                 on_ref[...].astype(F32)[:SUBLANES]], axis=0).astype(BF)
        x_ext = jnp.concatenate([xp_ref[...], x_ref[...], xn_ref[...]], axis=0)
    else:
        o_ext = o_ref[...]
        x_ext = x_ref[...]
    x_ext = x_ext + _rms(_dot(o_ext, wo_ref[...]), gate_mix * gmix_ref[...])
    h = _rms(x_ext, gpre_ref[...] * (1.0 + sc)) + sh
    if halo:
        t = pl.program_id(0) % tiles_per_seq
        x = x_ext[SUBLANES:SUBLANES + tm]
        hp = jnp.where(t == 0, zeros, h[:SUBLANES])
        hn = jnp.where(t == tiles_per_seq - 1, zeros, h[SUBLANES + tm:])
        hext = jnp.concatenate([hp, h[SUBLANES:SUBLANES + tm], hn], axis=0).astype(BF)
        starts = [SUBLANES]
        span = tm
    else:
        x = x_ext
        parts = [zeros]
        for s in range(tm // SEQ):
            parts += [h[s * SEQ:(s + 1) * SEQ], zeros]
        hext = jnp.concatenate(parts, axis=0).astype(BF)
        starts = [SUBLANES + s * (SEQ + SUBLANES) for s in range(tm // SEQ)]
        span = SEQ
    mext = hext.shape[0]

    def token_rows(v):
        parts = [v[r0:r0 + span] for r0 in starts]
        return parts[0] if len(parts) == 1 else jnp.concatenate(parts, axis=0)

    def conv(u, c0):
        cols = slice(c0, c0 + FF_CHUNK)
        prev = token_rows(pltpu.roll(u, 1, 0))
        nxt = token_rows(pltpu.roll(u, mext - 1, 0))
        return (prev * cw_ref[0, 0:1, cols] + token_rows(u) * cw_ref[0, 1:2, cols]
                + nxt * cw_ref[0, 2:3, cols] + cb_ref[0, :, cols])

    for j in range(N_CHUNK):
        ca = j * FF_CHUNK
        cg = DFF + j * FF_CHUNK
        a = conv(_dot(hext, wup_ref[:, ca:ca + FF_CHUNK]), ca)
        gt = conv(_dot(hext, wup_ref[:, cg:cg + FF_CHUNK]), cg)
        ha = 0.5 * a
        act_scr[:, ca:ca + FF_CHUNK] = ((ha + ha * jnp.tanh(ha)) * gt).astype(BF)
    y = _dot(act_scr[...], wdn_ref[...])
    out_ref[...] = x + _rms(y, gate * gpost_ref[...])


def _mix_ffn(o, x, modl, mod_map, gmix, w_o, gpre, gpost, wup, cw, cb, wdn, l, *, tm, halo):
    n = x.shape[0]
    tps = DEC_SEQ // tm

    def prev_blk(rows):
        r = tm // rows
        return lambda t: (jnp.maximum(t * r - 1, 0), 0)

    def next_blk(rows):
        r = tm // rows
        return lambda t: (jnp.minimum((t + 1) * r, n // rows - 1), 0)

    in_specs = [pl.BlockSpec((tm, D), lambda t: (t, 0))]
    args = [o]
    if halo:
        in_specs += [pl.BlockSpec((HALO_O, D), prev_blk(HALO_O)), pl.BlockSpec((HALO_O, D), next_blk(HALO_O))]
        args += [o, o]
    in_specs.append(pl.BlockSpec((tm, D), lambda t: (t, 0)))
    args.append(x)
    if halo:
        in_specs += [pl.BlockSpec((SUBLANES, D), prev_blk(SUBLANES)),
                     pl.BlockSpec((SUBLANES, D), next_blk(SUBLANES))]
        args += [x, x]
    const = dict(pipeline_mode=pl.Buffered(1))
    in_specs += [
        pl.BlockSpec((1, 1, 6 * D), mod_map),
        pl.BlockSpec((1, D), lambda t: (0, 0)),
        pl.BlockSpec((D, D), lambda t: (0, 0), **const),
        pl.BlockSpec((1, D), lambda t: (0, 0)),
        pl.BlockSpec((1, D), lambda t: (0, 0)),
        pl.BlockSpec((D, 2 * DFF), lambda t: (0, 0), **const),
        pl.BlockSpec((1, 3, 2 * DFF), lambda t: (l, 0, 0)),
        pl.BlockSpec((1, 1, 2 * DFF), lambda t: (l, 0, 0)),
        pl.BlockSpec((DFF, D), lambda t: (0, 0), **const),
    ]
    args += [modl, gmix.reshape(1, D), w_o, gpre.reshape(1, D), gpost.reshape(1, D), wup, cw, cb, wdn]
    return pl.pallas_call(
        functools.partial(_mix_ffn_kernel, halo=halo, tiles_per_seq=tps),
        grid=(n // tm,),
        in_specs=in_specs,
        out_specs=pl.BlockSpec((tm, D), lambda t: (t, 0)),
        out_shape=jax.ShapeDtypeStruct((n, D), F32),
        scratch_shapes=[pltpu.VMEM((tm, DFF), BF)],
        compiler_params=_params(("arbitrary",)),
        name="mix_ffn",
    )(*args)


def _rope_tables():
    t = jnp.arange(DEC_SEQ)
    rows = (t // GRID_W).astype(F32)
    cols = (t % GRID_W).astype(F32)
    quarter = HD // 4
    freqs = ROPE_BASE ** (-jnp.arange(quarter, dtype=F32) / quarter)
    lane = jnp.arange(LANES)
    d = lane % HD
    e = d % (HD // 2)
    is_x2 = (e >= quarter)[None, :]
    pos = jnp.where((d // (HD // 2) == 0)[None, :], rows[:, None], cols[:, None])
    ang = pos * freqs[e % quarter][None, :]
    cos = jnp.cos(ang)
    sin = jnp.sin(ang)
    return cos, jnp.where(is_x2, 0.0, -sin), jnp.where(is_x2, sin, 0.0)


def _block_diag_ones():
    r = jnp.arange(256)
    return (r[:, None] // HD == r[None, :] // HD).astype(BF)


def _cache_t(cache):
    b, n, p, h, d = cache.shape
    return cache.transpose(0, 1, 3, 4, 2).reshape(b, n, h * d, p)


def _cache_out(kt, heads):
    b, n, _, s = kt.shape
    return kt.reshape(b, n, heads, HD, s).transpose(0, 1, 4, 2, 3)


def kernel(x_prompt, x_sample, cache_na_k, cache_na_v, cache_gqa_k, cache_gqa_v, c, c_ctx,
           ada_w, ada_b, norm_mix_pre, norm_mix_post, norm_ffn_pre, norm_ffn_post,
           na_w_qkv, na_w_o, na_rpb, gqa_w_qkv, gqa_w_o, gqa_q_norm, gqa_k_norm,
           ffn_w_up, ffn_conv_w, ffn_conv_b, ffn_w_down):
    n_a = cache_na_k.shape[1]
    n_b = cache_gqa_k.shape[1]
    cond8 = jnp.concatenate([c_ctx[None], c, jnp.zeros((SUBLANES - 1 - DEC_BATCH, D), F32)], axis=0)
    mod = _adaln(cond8, ada_w, ada_b)

    tm_p, tm_s = 512, 512
    tm_fp, tm_fs = 512, 512
    map_p = lambda t: (0, 0, 0)
    map_s = lambda t: (1 + t // (DEC_SEQ // tm_s), 0, 0)
    map_fs = lambda t: (1 + t // (DEC_SEQ // tm_fs), 0, 0)

    cna_kt, cna_vt = _cache_t(cache_na_k), _cache_t(cache_na_v)
    cg_kt, cg_vt = _cache_t(cache_gqa_k), _cache_t(cache_gqa_v)
    rope_tabs = _rope_tables()
    bd = _block_diag_ones()
    ptabs = _na_bias_tables(na_rpb)
    kvg = NKV * HD

    cb = ffn_conv_b.reshape(DEPTH, 1, 2 * DFF)
    w_qkv = na_w_qkv[0:1].astype(BF)

    xp = x_prompt.reshape(NP, D)
    xs = x_sample.reshape(NS, D)
    na_kv = None
    gq_kv = None
    for l in range(DEPTH):
        i = l // 2
        modl = mod[l].reshape(SUBLANES, 1, 6 * D)
        if l + 1 < DEPTH:
            nxt = [(gqa_w_qkv if (l + 1) % 2 else na_w_qkv, (l + 1) // 2)]
        else:
            nxt = []
        if l % 2 == 0:
            qp, vp, kt, vt = _qkv_prompt(xp, modl, norm_mix_pre[l], w_qkv, i, n_a, na_kv,
                                         gqa=False, tm=tm_p)
            na_kv = (kt, vt)
            qs, kts, vs = _qkv_sample_na(xs, modl, map_s, norm_mix_pre[l], w_qkv, tm=tm_s)
            op, w_o, wup, wdn = _attn_prompt(qp, kt, vp, i, 1,
                                             [(na_w_o, i), (ffn_w_up, l), (ffn_w_down, l)])
            os_, *w_nxt = _attn_na_sample(qs, kts, vs, cna_kt, cna_vt, i, ptabs, nxt)
        else:
            qg = (jnp.tile(gqa_q_norm[i], NH) * 0.125).reshape(1, D)
            kg = jnp.tile(gqa_k_norm[i], NKV)
            qkg = jnp.concatenate([qg[0], kg]).reshape(1, D + kvg)
            kg2 = jnp.broadcast_to(kg[:, None], (kvg, SEQ))
            qp, vp, kt, vt = _qkv_prompt(xp, modl, norm_mix_pre[l], w_qkv, i, n_b, gq_kv,
                                         gqa=True, tm=tm_p, extra=(bd, qg, kg2))
            gq_kv = (kt, vt)
            qs, kts, vs = _qkv_sample_gqa(xs, modl, map_s, norm_mix_pre[l], w_qkv, bd, qkg,
                                          rope_tabs, tm=tm_s)
            op, w_o, wup, wdn = _attn_prompt(qp, kt, vp, i, NH // NKV,
                                             [(gqa_w_o, i), (ffn_w_up, l), (ffn_w_down, l)])
            os_, *w_nxt = _attn_gqa_sample(qs, kts, vs, cg_kt, cg_vt, i, nxt)
        if w_nxt:
            w_qkv = w_nxt[0][None]
        xp = _mix_ffn(op, xp, modl, map_p, norm_mix_post[l], w_o, norm_ffn_pre[l], norm_ffn_post[l],
                      wup, ffn_conv_w, cb, wdn, l, tm=tm_fp, halo=False)
        xs = _mix_ffn(os_, xs, modl, map_fs, norm_mix_post[l], w_o, norm_ffn_pre[l], norm_ffn_post[l],
                      wup, ffn_conv_w, cb, wdn, l, tm=tm_fs, halo=True)

    return (xp.reshape(BATCH, SEQ, D), xs.reshape(DEC_BATCH, DEC_SEQ, D),
            _cache_out(na_kv[0], NH), _cache_out(na_kv[1], NH),
            _cache_out(gq_kv[0], NKV), _cache_out(gq_kv[1], NKV))
```

```python
import functools

import jax
import jax.numpy as jnp
import numpy as np
from jax import lax
from jax.experimental import pallas as pl
from jax.experimental.pallas import tpu as pltpu

D = 1024
HD = 64
NH = 16
NKV = 4
DFF = 2816
DEPTH = 4
GRID_W = 64
ROWS = 32
NA_KH = 8
NA_KW = 16
ROPE_BASE = 10000.0
EPS = 1e-6
BATCH, SEQ = 32, 256
DEC_BATCH, DEC_SEQ = 2, 2048
PAST = 256
NP = BATCH * SEQ
NS = DEC_BATCH * DEC_SEQ

BF = jnp.bfloat16
F32 = jnp.float32
NEG = -1e30
LANES = 128
SUBLANES = 8
FF_CHUNK = 256
N_CHUNK = DFF // FF_CHUNK
VMEM_LIMIT = 50 * 1024 * 1024


def _params(sem):
    return pltpu.CompilerParams(dimension_semantics=sem, vmem_limit_bytes=VMEM_LIMIT)


def _rms(x, g):
    return x * lax.rsqrt(jnp.mean(x * x, axis=-1, keepdims=True) + EPS) * g


def _dot(a, b):
    return jnp.dot(a, b, preferred_element_type=F32)


def _dot_tn_nt(a, b):
    return lax.dot_general(a, b, (((0,), (1,)), ((), ())), preferred_element_type=F32)


def _qkv_weight_specs(gqa):
    if gqa:
        kvw = NKV * HD
        return [pl.BlockSpec((1, D, D), lambda t: (0, 0, 0)),
                pl.BlockSpec((1, D, kvw), lambda t: (0, 0, D // kvw)),
                pl.BlockSpec((1, D, kvw), lambda t: (0, 0, D // kvw + 1))]
    return [pl.BlockSpec((1, D, D), functools.partial(lambda t, c: (0, 0, c), c=c)) for c in range(3)]


def _mod_kernel(cond_ref, w_ref, b_ref, *rest):
    n_w = (len(rest) - 1) // 2
    o_ref = rest[n_w]
    _cast_slabs(rest[:n_w], rest[n_w + 1:])
    s = cond_ref[...]
    s = s * jax.nn.sigmoid(s)
    o_ref[0] = _dot(s.astype(BF), w_ref[0].astype(BF)) + b_ref[0]


def _adaln(cond8, ada_w, ada_b, weights=()):
    tn = 1536
    nn = 6 * D // tn
    w_in, w_out, w_shape = _cast_specs(weights, DEPTH * nn, lambda l, n: l * nn + n)
    return pl.pallas_call(
        _mod_kernel,
        grid=(DEPTH, nn),
        in_specs=[
            pl.BlockSpec((SUBLANES, D), lambda l, n: (0, 0)),
            pl.BlockSpec((1, D, tn), lambda l, n: (l, 0, n)),
            pl.BlockSpec((1, 1, tn), lambda l, n: (l, 0, n)),
        ] + w_in,
        out_specs=[pl.BlockSpec((1, SUBLANES, tn), lambda l, n: (l, 0, n))] + w_out,
        out_shape=[jax.ShapeDtypeStruct((DEPTH, SUBLANES, 6 * D), F32)] + w_shape,
        compiler_params=_params(("arbitrary", "arbitrary")),
        name="adaln",
    )(cond8, ada_w, ada_b.reshape(DEPTH, 1, 6 * D), *[w for w, _ in weights])


def _prenorm(x_ref, mod_ref, g_ref):
    m = mod_ref[0]
    return (_rms(x_ref[...], g_ref[...] * (1.0 + m[:, D:2 * D])) + m[:, 0:D]).astype(BF)


def _head_rms_lanes(blk, bd_ref):
    ss = _dot((blk * blk).astype(BF), bd_ref[...])
    return blk * lax.rsqrt(ss * (1.0 / HD) + EPS)


def _qkv_prompt_kernel(*refs, gqa, slot):
    if gqa:
        (x_ref, mod_ref, g_ref, wq_ref, wk_ref, wv_ref, bd_ref, qg_ref, kg_ref,
         q_ref, v_ref, kt_ref, vt_ref) = refs
    else:
        x_ref, mod_ref, g_ref, wq_ref, wk_ref, wv_ref, q_ref, v_ref, kt_ref, vt_ref = refs
    hb = _prenorm(x_ref, mod_ref, g_ref)
    q = _dot(hb, wq_ref[0])
    v = _dot(hb, wv_ref[0])
    kt = _dot_tn_nt(wk_ref[0], hb)
    if gqa:
        for t in range(D // 256):
            sl = slice(t * 256, (t + 1) * 256)
            q_ref[:, sl] = (_head_rms_lanes(q[:, sl], bd_ref) * qg_ref[:, sl]).astype(q_ref.dtype)
        heads = []
        for h in range(NKV):
            blk = kt[h * HD:(h + 1) * HD, :]
            heads.append(blk * lax.rsqrt(jnp.mean(blk * blk, axis=0, keepdims=True) + EPS))
        kt = jnp.concatenate(heads, axis=0)
    else:
        q_ref[...] = (q * 0.125).astype(q_ref.dtype)
    v_ref[...] = v.astype(v_ref.dtype)
    for s in range(x_ref.shape[0] // SEQ):
        sl = slice(s * SEQ, (s + 1) * SEQ)
        kts = kt[:, sl]
        if gqa:
            kts = kts * kg_ref[...]
        kt_ref[s, slot] = kts
        vt_ref[s, slot] = v[sl, :].T
        for other in range(kt_ref.shape[1]):
            if other != slot:
                kt_ref[s, other] = jnp.zeros_like(kts)
                vt_ref[s, other] = jnp.zeros_like(kts)


def _qkv_prompt(x, modl, g, w, i, n_layers, prev, *, gqa, tm, extra=()):
    kvw = NKV * HD if gqa else D
    in_specs = [
        pl.BlockSpec((tm, D), lambda t: (t, 0)),
        pl.BlockSpec((1, 1, 6 * D), lambda t: (0, 0, 0)),
        pl.BlockSpec((1, D), lambda t: (0, 0)),
    ] + _qkv_weight_specs(gqa)
    args = [x, modl, g.reshape(1, D), w, w, w]
    if gqa:
        in_specs += [pl.BlockSpec((256, 256), lambda t: (0, 0)),
                     pl.BlockSpec((1, D), lambda t: (0, 0)),
                     pl.BlockSpec((kvw, SEQ), lambda t: (0, 0))]
        args += list(extra)
    n_in = len(args)
    aliases = {}
    if prev is None:
        kv_spec = pl.BlockSpec((tm // SEQ, n_layers, kvw, SEQ), lambda t: (t, 0, 0, 0))
        slot = i
    else:
        in_specs += [pl.BlockSpec(memory_space=pl.ANY), pl.BlockSpec(memory_space=pl.ANY)]
        args += list(prev)
        aliases = {n_in: 2, n_in + 1: 3}
        kv_spec = pl.BlockSpec((tm // SEQ, 1, kvw, SEQ), lambda t: (t, i, 0, 0))
        slot = 0
    kv_shape = jax.ShapeDtypeStruct((BATCH, n_layers, kvw, SEQ), F32)

    def body(*refs):
        _qkv_prompt_kernel(*(refs[:n_in] + refs[len(args):]), gqa=gqa, slot=slot)

    return pl.pallas_call(
        body,
        grid=(NP // tm,),
        in_specs=in_specs,
        out_specs=[
            pl.BlockSpec((tm, D), lambda t: (t, 0)),
            pl.BlockSpec((tm, kvw), lambda t: (t, 0)),
            kv_spec, kv_spec,
        ],
        out_shape=[
            jax.ShapeDtypeStruct((NP, D), BF),
            jax.ShapeDtypeStruct((NP, kvw), BF),
            kv_shape, kv_shape,
        ],
        input_output_aliases=aliases,
        compiler_params=_params(("arbitrary",)),
        name="qkv_prompt_gqa" if gqa else "qkv_prompt_na",
    )(*args)


def _qkv_sample_na_kernel(x_ref, mod_ref, g_ref, wq_ref, wk_ref, wv_ref, q_ref, kt_ref, v_ref):
    hb = _prenorm(x_ref, mod_ref, g_ref)
    q_ref[...] = (_dot(hb, wq_ref[0]) * 0.125).astype(q_ref.dtype)
    v_ref[...] = _dot(hb, wv_ref[0]).astype(v_ref.dtype)
    kt_ref[...] = _dot_tn_nt(wk_ref[0], hb).astype(kt_ref.dtype)


def _qkv_sample_na(x, modl, mod_map, g, w, *, tm):
    return pl.pallas_call(
        _qkv_sample_na_kernel,
        grid=(NS // tm,),
        in_specs=[
            pl.BlockSpec((tm, D), lambda t: (t, 0)),
            pl.BlockSpec((1, 1, 6 * D), mod_map),
            pl.BlockSpec((1, D), lambda t: (0, 0)),
        ] + _qkv_weight_specs(False),
        out_specs=[
            pl.BlockSpec((tm, D), lambda t: (t, 0)),
            pl.BlockSpec((D, tm), lambda t: (0, t)),
            pl.BlockSpec((tm, D), lambda t: (t, 0)),
        ],
        out_shape=[
            jax.ShapeDtypeStruct((NS, D), BF),
            jax.ShapeDtypeStruct((D, NS), BF),
            jax.ShapeDtypeStruct((NS, D), BF),
        ],
        compiler_params=_params(("arbitrary",)),
        name="qkv_sample_na",
    )(x, modl, g.reshape(1, D), w, w, w)


def _qkv_sample_gqa_kernel(x_ref, mod_ref, g_ref, w_ref, bd_ref, qkg_ref, cos_ref, s1_ref, s2_ref,
                           q_ref, kt_ref, v_ref):
    qkv = _dot(_prenorm(x_ref, mod_ref, g_ref), w_ref[0])
    nqk = D + NKV * HD
    for t in range(nqk // LANES):
        if t % 2 == 0:
            nrm = _head_rms_lanes(qkv[:, (t // 2) * 256:(t // 2 + 1) * 256], bd_ref)
        xt = nrm[:, (t % 2) * LANES:(t % 2 + 1) * LANES] * qkg_ref[:, t * LANES:(t + 1) * LANES]
        xt = (xt * cos_ref[...]
              + pltpu.roll(xt, LANES - 16, 1) * s1_ref[...]
              + pltpu.roll(xt, 16, 1) * s2_ref[...])
        if t < D // LANES:
            q_ref[:, t * LANES:(t + 1) * LANES] = xt.astype(q_ref.dtype)
        else:
            tk = t - D // LANES
            kt_ref[tk * LANES:(tk + 1) * LANES, :] = xt.T.astype(kt_ref.dtype)
    v_ref[...] = qkv[:, nqk:].astype(v_ref.dtype)


def _qkv_sample_gqa(x, modl, mod_map, g, w, bd, qkg, rope_tabs, *, tm):
    nq = w.shape[2]
    kvw = NKV * HD
    tps = DEC_SEQ // tm
    in_specs = [
        pl.BlockSpec((tm, D), lambda t: (t, 0)),
        pl.BlockSpec((1, 1, 6 * D), mod_map),
        pl.BlockSpec((1, D), lambda t: (0, 0)),
        pl.BlockSpec((1, D, nq), lambda t: (0, 0, 0)),
        pl.BlockSpec((256, 256), lambda t: (0, 0)),
        pl.BlockSpec((1, D + kvw), lambda t: (0, 0)),
    ]
    in_specs += [pl.BlockSpec((tm, LANES), lambda t: (t % tps, 0)) for _ in rope_tabs]
    return pl.pallas_call(
        _qkv_sample_gqa_kernel,
        grid=(NS // tm,),
        in_specs=in_specs,
        out_specs=[
            pl.BlockSpec((tm, D), lambda t: (t, 0)),
            pl.BlockSpec((kvw, tm), lambda t: (0, t)),
            pl.BlockSpec((tm, kvw), lambda t: (t, 0)),
        ],
        out_shape=[
            jax.ShapeDtypeStruct((NS, D), BF),
            jax.ShapeDtypeStruct((kvw, NS), BF),
            jax.ShapeDtypeStruct((NS, kvw), BF),
        ],
        compiler_params=_params(("arbitrary",)),
        name="qkv_sample_gqa",
    )(x, modl, g.reshape(1, D), w, bd, qkg, *rope_tabs)


def _lane_lo(rows):
    return lax.broadcasted_iota(jnp.int32, (rows, LANES), 1) < HD


def _pipelined(n, first, second):
    nxt = first(0)
    for i in range(n):
        cur = nxt
        if i + 1 < n:
            nxt = first(i + 1)
        second(i, cur)


def _attn_grouped(problems, group, stack, *, ones_sum):
    tq = problems[0][0].shape[0]
    lo = _lane_lo(tq)
    lo_s = _lane_lo(stack * tq)
    pending = {}
    per = NH // stack

    def scores(n):
        q_ref, _, key_tiles, _ = problems[n // per]
        u = n % per
        g = u * stack // group
        parts = []
        for h in range(u * stack, (u + 1) * stack):
            q2 = q_ref[:, (h // 2) * LANES:(h // 2 + 1) * LANES]
            if h % 2 != g % 2:
                q2 = pltpu.roll(q2.astype(F32), HD, 1).astype(BF)
            parts.append(q2)
        qs = parts[0] if stack == 1 else jnp.concatenate(parts, axis=0)
        mine = lo_s if g % 2 == 0 else jnp.logical_not(lo_s)
        qsel = jnp.where(mine, qs, jnp.zeros_like(qs))
        return [_dot(qsel, kt) for kt in key_tiles(g)]

    def finish(n, ss):
        _, o_ref, _, value_tiles = problems[n // per]
        u = n % per
        g = u * stack // group
        m = ss[0].max(axis=-1, keepdims=True)
        for s in ss[1:]:
            m = jnp.maximum(m, s.max(axis=-1, keepdims=True))
        es = [jnp.exp(s - m) for s in ss]
        pv = None
        for e, v in zip(es, value_tiles(g)):
            if ones_sum:
                mine = _lane_lo(v.shape[0])
                if g % 2:
                    mine = jnp.logical_not(mine)
                v = jnp.where(mine, v, jnp.ones_like(v))
            d = _dot(e.astype(BF), v)
            pv = d if pv is None else pv + d
        if ones_sum:
            pv = pv / pltpu.roll(pv, HD, 1)
        else:
            l = es[0].sum(axis=-1, keepdims=True)
            for e in es[1:]:
                l = l + e.sum(axis=-1, keepdims=True)
            pv = pv / l
        for j in range(stack):
            h = u * stack + j
            pj = pv[j * tq:(j + 1) * tq]
            if h % 2 != g % 2:
                pj = pltpu.roll(pj, HD, 1)
            if h % 2 == 0:
                pending[h // 2] = pj
            else:
                sl = slice((h // 2) * LANES, (h // 2 + 1) * LANES)
                o_ref[:, sl] = jnp.where(lo, pending.pop(h // 2), pj).astype(o_ref.dtype)

    _pipelined(per * len(problems), scores, finish)


ATTN_P_SEQS = 4


def _attn_prompt_kernel(q_ref, kt_ref, v_ref, *rest, group):
    n_w = (len(rest) - 1) // 2
    o_ref = rest[n_w]
    _cast_slabs(rest[:n_w], rest[n_w + 1:])

    def tile(g):
        return slice((g // 2) * LANES, (g // 2 + 1) * LANES)

    def problem(s):
        rows = pl.ds(s * SEQ, SEQ)
        return (q_ref.at[rows, :], o_ref.at[rows, :],
                lambda g: [kt_ref[s, 0, tile(g), :].astype(BF)],
                lambda g: [v_ref[pl.ds(s * SEQ, SEQ), tile(g)]])
    _attn_grouped([problem(s) for s in range(ATTN_P_SEQS)], group, group, ones_sum=False)


def _attn_prompt(q, kt_all, v, i, group, weights):
    kvw = v.shape[1]
    rows = ATTN_P_SEQS * SEQ
    steps = BATCH // ATTN_P_SEQS
    w_in, w_out, w_shape = _cast_specs(weights, steps, lambda b: b)
    return pl.pallas_call(
        functools.partial(_attn_prompt_kernel, group=group),
        grid=(steps,),
        in_specs=[
            pl.BlockSpec((rows, D), lambda b: (b, 0)),
            pl.BlockSpec((ATTN_P_SEQS, 1, kvw, SEQ), lambda b: (b, i, 0, 0)),
            pl.BlockSpec((rows, kvw), lambda b: (b, 0)),
        ] + w_in,
        out_specs=[pl.BlockSpec((rows, D), lambda b: (b, 0))] + w_out,
        out_shape=[jax.ShapeDtypeStruct((NP, D), BF)] + w_shape,
        compiler_params=_params(("arbitrary",)),
        name="attn_prompt",
    )(q, kt_all, v, *[w for w, _ in weights])


def _cast_specs(weights, steps, slab):
    w_in, w_out, w_shape = [], [], []
    for w, layer in weights:
        _, r, c = w.shape
        w_in.append(pl.BlockSpec((1, r // steps, c),
                                 functools.partial(lambda *ids, layer: (layer, slab(*ids), 0), layer=layer)))
        w_out.append(pl.BlockSpec((r // steps, c), lambda *ids: (slab(*ids), 0)))
        w_shape.append(jax.ShapeDtypeStruct((r, c), BF))
    return w_in, w_out, w_shape


def _cast_slabs(w_refs, wb_refs):
    for w_ref, wb_ref in zip(w_refs, wb_refs):
        wb_ref[...] = w_ref[0].astype(BF)


def _attn_gqa_sample_kernel(q_ref, kt_ref, v_ref, kct_ref, vct_ref, *rest):
    n_w = (len(rest) - 1) // 2
    o_ref = rest[n_w]
    _cast_slabs(rest[:n_w], rest[n_w + 1:])
    vc = vct_ref[0, 0].T.astype(BF)

    def tile(g):
        return slice((g // 2) * LANES, (g // 2 + 1) * LANES)
    _attn_grouped([(q_ref, o_ref,
                    lambda g: [kt_ref[tile(g), :], kct_ref[0, 0, tile(g), :].astype(BF)],
                    lambda g: [v_ref[:, tile(g)], vc[:, tile(g)]])],
                  NH // NKV, NH // NKV, ones_sum=True)


GQA_TQ = 256


def _attn_gqa_sample(q, kt, v, cache_kt, cache_vt, i, weights=()):
    nq = DEC_SEQ // GQA_TQ
    kvw = NKV * HD
    w_in, w_out, w_shape = _cast_specs(weights, DEC_BATCH * nq, lambda b, t: b * nq + t)
    return pl.pallas_call(
        _attn_gqa_sample_kernel,
        grid=(DEC_BATCH, nq),
        in_specs=[
            pl.BlockSpec((GQA_TQ, D), lambda b, t: (b * nq + t, 0)),
            pl.BlockSpec((kvw, DEC_SEQ), lambda b, t: (0, b)),
            pl.BlockSpec((DEC_SEQ, kvw), lambda b, t: (b, 0)),
            pl.BlockSpec((1, 1, kvw, PAST), lambda b, t: (b, i, 0, 0)),
            pl.BlockSpec((1, 1, kvw, PAST), lambda b, t: (b, i, 0, 0)),
        ] + w_in,
        out_specs=[pl.BlockSpec((GQA_TQ, D), lambda b, t: (b * nq + t, 0))] + w_out,
        out_shape=[jax.ShapeDtypeStruct((NS, D), BF)] + w_shape,
        compiler_params=_params(("arbitrary", "arbitrary")),
        name="attn_gqa_sample",
    )(q, kt, v, cache_kt, cache_vt, *[w for w, _ in weights])


NA_QROWS = 4
NA_KBLK = NA_QROWS * GRID_W
NA_NKB = 3
NA_PAIRS = NA_QROWS // 2


def _attn_na_sample_kernel(*refs):
    q_ref = refs[0]
    kt_refs = refs[1:1 + NA_NKB]
    v_refs = refs[1 + NA_NKB:1 + 2 * NA_NKB]
    kct_ref, vct_ref, p_ref = refs[1 + 2 * NA_NKB:4 + 2 * NA_NKB]
    rest = refs[4 + 2 * NA_NKB:]
    n_w = (len(rest) - 1) // 2
    o_ref = rest[n_w]
    _cast_slabs(rest[:n_w], rest[n_w + 1:])
    rb = pl.program_id(1)
    kb0 = jnp.clip(rb - 1, 0, ROWS // NA_QROWS - NA_NKB)
    tq = q_ref.shape[0]
    lo = _lane_lo(tq)
    hi = jnp.logical_not(lo)
    lo_w = _lane_lo(GRID_W)
    vc = vct_ref[0, 0].T.astype(BF)
    ones = jnp.ones((tq, LANES), BF)

    tiles = []
    for i in range(NA_QROWS):
        qr = rb * NA_QROWS + i
        rs = jnp.clip(qr - NA_KH // 2, 0, ROWS - NA_KH)
        for j in range(NA_NKB):
            for jj in range(NA_PAIRS):
                kra = (kb0 + j) * NA_QROWS + 2 * jj
                krb = kra + 1
                idx = jnp.clip(kra - qr + NA_KH, 0, 2 * NA_KH - 1)
                va = jnp.logical_and(kra >= rs, kra < rs + NA_KH).astype(jnp.int32)
                vb = jnp.logical_and(krb >= rs, krb < rs + NA_KH).astype(jnp.int32)
                tiles.append((idx, jnp.where(lo_w, va, vb) > 0))

    def scores(h):
        rows = slice((h // 2) * LANES, (h // 2 + 1) * LANES)
        q2 = q_ref[:, rows]
        qsel = jnp.where(lo if h % 2 == 0 else hi, q2, jnp.zeros_like(q2))
        s_lat = [_dot(qsel, r[rows, :]) for r in kt_refs]
        out_rows = []
        for i in range(NA_QROWS):
            rsl = slice(i * GRID_W, (i + 1) * GRID_W)
            row = []
            for j in range(NA_NKB):
                for jj in range(NA_PAIRS):
                    idx, valid = tiles[(i * NA_NKB + j) * NA_PAIRS + jj]
                    t = s_lat[j][rsl, jj * LANES:(jj + 1) * LANES] + p_ref[0, h, idx]
                    row.append(jnp.where(valid, t, NEG))
            out_rows.append(jnp.concatenate(row, axis=1))
        return jnp.concatenate(out_rows, axis=0), _dot(qsel, kct_ref[0, 0, rows, :].astype(BF))

    def weighted_values(h, s, s_ctx):
        rows = slice((h // 2) * LANES, (h // 2 + 1) * LANES)
        mine = lo if h % 2 == 0 else hi
        m = jnp.maximum(s.max(axis=-1, keepdims=True), s_ctx.max(axis=-1, keepdims=True))
        e = jnp.exp(s - m).astype(BF)
        e_ctx = jnp.exp(s_ctx - m).astype(BF)
        pv = _dot(e_ctx, jnp.where(mine, vc[:, rows], ones))
        for j in range(NA_NKB):
            vj = jnp.where(mine, v_refs[j][:, rows], ones)
            pv = pv + _dot(e[:, j * NA_KBLK:(j + 1) * NA_KBLK], vj)
        return pv / pltpu.roll(pv, HD, 1)

    nxt = scores(0)
    o2 = None
    for h in range(NH):
        cur = nxt
        if h + 1 < NH:
            nxt = scores(h + 1)
        pv = weighted_values(h, *cur)
        if h % 2 == 0:
            o2 = pv
        else:
            o_ref[:, (h // 2) * LANES:(h // 2 + 1) * LANES] = jnp.where(lo, o2, pv).astype(o_ref.dtype)


def _attn_na_sample(q, kt, v, cache_kt, cache_vt, i, ptab, weights=()):
    nrb = ROWS // NA_QROWS
    w_in, w_out, w_shape = _cast_specs(weights, DEC_BATCH * nrb, lambda b, r: b * nrb + r)

    def kb(b, r, j):
        return b * nrb + jnp.clip(r - 1, 0, nrb - NA_NKB) + j

    in_specs = [pl.BlockSpec((NA_KBLK, D), lambda b, r: (b * nrb + r, 0))]
    in_specs += [pl.BlockSpec((D, NA_KBLK), functools.partial(lambda b, r, j: (0, kb(b, r, j)), j=j))
                 for j in range(NA_NKB)]
    in_specs += [pl.BlockSpec((NA_KBLK, D), functools.partial(lambda b, r, j: (kb(b, r, j), 0), j=j))
                 for j in range(NA_NKB)]
    in_specs += [
        pl.BlockSpec((1, 1, D, PAST), lambda b, r: (b, i, 0, 0)),
        pl.BlockSpec((1, 1, D, PAST), lambda b, r: (b, i, 0, 0)),
        pl.BlockSpec((1, NH, 2 * NA_KH, GRID_W, LANES), lambda b, r: (i, 0, 0, 0, 0),
                     pipeline_mode=pl.Buffered(1)),
    ]
    return pl.pallas_call(
        _attn_na_sample_kernel,
        grid=(DEC_BATCH, nrb),
        in_specs=in_specs + w_in,
        out_specs=[pl.BlockSpec((NA_KBLK, D), lambda b, r: (b * nrb + r, 0))] + w_out,
        out_shape=[jax.ShapeDtypeStruct((NS, D), BF)] + w_shape,
        compiler_params=_params(("arbitrary", "arbitrary")),
        name="attn_na_sample",
    )(q, *([kt] * NA_NKB), *([v] * NA_NKB), cache_kt, cache_vt, ptab, *[w for w, _ in weights])


N_DC = 2 * NA_KW - 1
N_DR = 2 * NA_KH - 1
N_ENT = N_DR + 1
DC_PAD = 32
N_SPLIT = 3
TILE_ELEMS = GRID_W * LANES


def _na_bias_kernel(lhs_ref, sel_ref, ok_ref, o_ref):
    t = _dot(lhs_ref[0], sel_ref[...])
    ok = ok_ref[...] > 0.0
    for h in range(NH):
        rows = slice(h * N_ENT, (h + 1) * N_ENT)
        o_ref[0, rows, :] = jnp.where(ok, t[rows, :], NEG)


def _na_bias_tables(rpb):
    n_a = rpb.shape[0]
    qc = jnp.arange(GRID_W)[:, None]
    lane = jnp.arange(LANES)[None, :]
    kc = lane % GRID_W
    half = lane // GRID_W
    c_start = jnp.clip(qc - NA_KW // 2, 0, GRID_W - NA_KW)
    ok = (kc >= c_start) & (kc < c_start + NA_KW)
    slot = (half * DC_PAD + kc - qc + (NA_KW - 1)).reshape(-1)
    sel = (jnp.arange(2 * DC_PAD)[:, None] == slot[None, :]) & ok.reshape(1, -1)
    sel = jnp.tile(sel.astype(BF), (N_SPLIT, 1))
    d = jnp.arange(N_ENT)[:, None, None]
    ok_d = ok[None] & jnp.where(half[None] == 0, d >= 1, d <= N_DR - 1)
    ok_d = ok_d.reshape(N_ENT, TILE_ELEMS).astype(F32)
    pad = jnp.zeros((n_a, NH, 1, N_DC), F32)
    left = jnp.concatenate([pad, rpb], axis=2)
    right = jnp.concatenate([rpb, pad], axis=2)
    padc = lambda a: jnp.pad(a, ((0, 0), (0, 0), (0, 0), (0, DC_PAD - N_DC)))
    both = jnp.concatenate([padc(left), padc(right)], axis=-1).reshape(n_a, NH * N_ENT, 2 * DC_PAD)
    pieces = []
    rest = both
    for _ in range(N_SPLIT):
        piece = rest.astype(BF)
        pieces.append(piece)
        rest = rest - piece.astype(F32)
    lhs = jnp.concatenate(pieces, axis=-1)
    k = N_SPLIT * 2 * DC_PAD
    out = pl.pallas_call(
        _na_bias_kernel,
        grid=(n_a,),
        in_specs=[
            pl.BlockSpec((1, NH * N_ENT, k), lambda a: (a, 0, 0)),
            pl.BlockSpec((k, TILE_ELEMS), lambda a: (0, 0)),
            pl.BlockSpec((N_ENT, TILE_ELEMS), lambda a: (0, 0)),
        ],
        out_specs=pl.BlockSpec((1, NH * N_ENT, TILE_ELEMS), lambda a: (a, 0, 0)),
        out_shape=jax.ShapeDtypeStruct((n_a, NH * N_ENT, TILE_ELEMS), F32),
        compiler_params=_params(("arbitrary",)),
        name="na_bias",
    )(lhs, sel, ok_d)
    return out.reshape(n_a, NH, N_ENT, GRID_W, LANES)


HALO_O = 16


def _mix_ffn_kernel(*refs, halo, tiles_per_seq):
    if halo:
        (o_ref, op_ref, on_ref, x_ref, xp_ref, xn_ref, mod_ref, gmix_ref, wo_ref,
         gpre_ref, gpost_ref, wup_ref, cw_ref, cb_ref, wdn_ref, out_ref, act_scr) = refs
    else:
        (o_ref, x_ref, mod_ref, gmix_ref, wo_ref,
         gpre_ref, gpost_ref, wup_ref, cw_ref, cb_ref, wdn_ref, out_ref, act_scr) = refs
    tm = x_ref.shape[0]
    m = mod_ref[0]
    gate_mix = m[:, 2 * D:3 * D]
    sh = m[:, 3 * D:4 * D]
    sc = m[:, 4 * D:5 * D]
    gate = m[:, 5 * D:6 * D]
    zeros = jnp.zeros((SUBLANES, D), F32)

    if halo:
        o_ext = jnp.concatenate([op_ref[...].astype(F32)[HALO_O - SUBLANES:], o_ref[...].astype(F32),
                                 on_ref[...].astype(F32)[:SUBLANES]], axis=0).astype(BF)
        x_ext = jnp.concatenate([xp_ref[...], x_ref[...], xn_ref[...]], axis=0)
    else:
        o_ext = o_ref[...]
        x_ext = x_ref[...]
    x_ext = x_ext + _rms(_dot(o_ext, wo_ref[...]), gate_mix * gmix_ref[...])
    h = _rms(x_ext, gpre_ref[...] * (1.0 + sc)) + sh
    if halo:
        t = pl.program_id(0) % tiles_per_seq
        x = x_ext[SUBLANES:SUBLANES + tm]
        hp = jnp.where(t == 0, zeros, h[:SUBLANES])
        hn = jnp.where(t == tiles_per_seq - 1, zeros, h[SUBLANES + tm:])
        hext = jnp.concatenate([hp, h[SUBLANES:SUBLANES + tm], hn], axis=0).astype(BF)
        starts = [SUBLANES]
        span = tm
    else:
        x = x_ext
        parts = [zeros]
        for s in range(tm // SEQ):
            parts += [h[s * SEQ:(s + 1) * SEQ], zeros]
        hext = jnp.concatenate(parts, axis=0).astype(BF)
        starts = [SUBLANES + s * (SEQ + SUBLANES) for s in range(tm // SEQ)]
        span = SEQ
    mext = hext.shape[0]

    def token_rows(v):
        parts = [v[r0:r0 + span] for r0 in starts]
        return parts[0] if len(parts) == 1 else jnp.concatenate(parts, axis=0)

    def conv(u, c0):
        cols = slice(c0, c0 + FF_CHUNK)
        prev = token_rows(pltpu.roll(u, 1, 0))
        nxt = token_rows(pltpu.roll(u, mext - 1, 0))
        return (prev * cw_ref[0, 0:1, cols] + token_rows(u) * cw_ref[0, 1:2, cols]
                + nxt * cw_ref[0, 2:3, cols] + cb_ref[0, :, cols])

    for j in range(N_CHUNK):
        ca = j * FF_CHUNK
        cg = DFF + j * FF_CHUNK
        a = conv(_dot(hext, wup_ref[:, ca:ca + FF_CHUNK]), ca)
        gt = conv(_dot(hext, wup_ref[:, cg:cg + FF_CHUNK]), cg)
        ha = 0.5 * a
        act_scr[:, ca:ca + FF_CHUNK] = ((ha + ha * jnp.tanh(ha)) * gt).astype(BF)
    y = _dot(act_scr[...], wdn_ref[...])
    out_ref[...] = x + _rms(y, gate * gpost_ref[...])


def _mix_ffn(o, x, modl, mod_map, gmix, w_o, gpre, gpost, wup, cw, cb, wdn, l, *, tm, halo):
    n = x.shape[0]
    tps = DEC_SEQ // tm

    def prev_blk(rows):
        r = tm // rows
        return lambda t: (jnp.maximum(t * r - 1, 0), 0)

    def next_blk(rows):
        r = tm // rows
        return lambda t: (jnp.minimum((t + 1) * r, n // rows - 1), 0)

    in_specs = [pl.BlockSpec((tm, D), lambda t: (t, 0))]
    args = [o]
    if halo:
        in_specs += [pl.BlockSpec((HALO_O, D), prev_blk(HALO_O)), pl.BlockSpec((HALO_O, D), next_blk(HALO_O))]
        args += [o, o]
    in_specs.append(pl.BlockSpec((tm, D), lambda t: (t, 0)))
    args.append(x)
    if halo:
        in_specs += [pl.BlockSpec((SUBLANES, D), prev_blk(SUBLANES)),
                     pl.BlockSpec((SUBLANES, D), next_blk(SUBLANES))]
        args += [x, x]
    const = dict(pipeline_mode=pl.Buffered(1))
    in_specs += [
        pl.BlockSpec((1, 1, 6 * D), mod_map),
        pl.BlockSpec((1, D), lambda t: (0, 0)),
        pl.BlockSpec((D, D), lambda t: (0, 0), **const),
        pl.BlockSpec((1, D), lambda t: (0, 0)),
        pl.BlockSpec((1, D), lambda t: (0, 0)),
        pl.BlockSpec((D, 2 * DFF), lambda t: (0, 0), **const),
        pl.BlockSpec((1, 3, 2 * DFF), lambda t: (l, 0, 0)),
        pl.BlockSpec((1, 1, 2 * DFF), lambda t: (l, 0, 0)),
        pl.BlockSpec((DFF, D), lambda t: (0, 0), **const),
    ]
    args += [modl, gmix.reshape(1, D), w_o, gpre.reshape(1, D), gpost.reshape(1, D), wup, cw, cb, wdn]
    return pl.pallas_call(
        functools.partial(_mix_ffn_kernel, halo=halo, tiles_per_seq=tps),
        grid=(n // tm,),
        in_specs=in_specs,
        out_specs=pl.BlockSpec((tm, D), lambda t: (t, 0)),
        out_shape=jax.ShapeDtypeStruct((n, D), F32),
        scratch_shapes=[pltpu.VMEM((tm, DFF), BF)],
        compiler_params=_params(("arbitrary",)),
        name="mix_ffn",
    )(*args)


def _rope_tables():
    f32 = np.float32
    t = np.arange(DEC_SEQ)
    rows = (t // GRID_W).astype(f32)
    cols = (t % GRID_W).astype(f32)
    quarter = HD // 4
    freqs = f32(ROPE_BASE) ** (-np.arange(quarter, dtype=f32) / f32(quarter))
    lane = np.arange(LANES)
    d = lane % HD
    e = d % (HD // 2)
    is_x2 = (e >= quarter)[None, :]
    pos = np.where((d // (HD // 2) == 0)[None, :], rows[:, None], cols[:, None])
    ang = (pos * freqs[e % quarter][None, :]).astype(f32)
    cos = np.cos(ang).astype(f32)
    sin = np.sin(ang).astype(f32)
    zero = f32(0.0)
    return (jnp.asarray(cos), jnp.asarray(np.where(is_x2, zero, -sin)),
            jnp.asarray(np.where(is_x2, sin, zero)))


def _block_diag_ones():
    r = jnp.arange(256)
    return (r[:, None] // HD == r[None, :] // HD).astype(BF)


def _cache_t(cache):
    b, n, p, h, d = cache.shape
    return cache.transpose(0, 1, 3, 4, 2).reshape(b, n, h * d, p)


def _cache_out(kt, heads):
    b, n, _, s = kt.shape
    return kt.reshape(b, n, heads, HD, s).transpose(0, 1, 4, 2, 3)


def kernel(x_prompt, x_sample, cache_na_k, cache_na_v, cache_gqa_k, cache_gqa_v, c, c_ctx,
           ada_w, ada_b, norm_mix_pre, norm_mix_post, norm_ffn_pre, norm_ffn_post,
           na_w_qkv, na_w_o, na_rpb, gqa_w_qkv, gqa_w_o, gqa_q_norm, gqa_k_norm,
           ffn_w_up, ffn_conv_w, ffn_conv_b, ffn_w_down):
    n_a = cache_na_k.shape[1]
    n_b = cache_gqa_k.shape[1]
    cond8 = jnp.concatenate([c_ctx[None], c, jnp.zeros((SUBLANES - 1 - DEC_BATCH, D), F32)], axis=0)
    mod, w_qkv = _adaln(cond8, ada_w, ada_b, [(na_w_qkv, 0)])
    w_qkv = w_qkv[None]

    tm_p, tm_s = 512, 512
    tm_fp, tm_fs = 512, 512
    map_p = lambda t: (0, 0, 0)
    map_s = lambda t: (1 + t // (DEC_SEQ // tm_s), 0, 0)
    map_fs = lambda t: (1 + t // (DEC_SEQ // tm_fs), 0, 0)

    cna_kt, cna_vt = _cache_t(cache_na_k), _cache_t(cache_na_v)
    cg_kt, cg_vt = _cache_t(cache_gqa_k), _cache_t(cache_gqa_v)
    rope_tabs = _rope_tables()
    bd = _block_diag_ones()
    ptabs = _na_bias_tables(na_rpb)
    kvg = NKV * HD

    cb = ffn_conv_b.reshape(DEPTH, 1, 2 * DFF)

    xp = x_prompt.reshape(NP, D)
    xs = x_sample.reshape(NS, D)
    na_kv = None
    gq_kv = None
    for l in range(DEPTH):
        i = l // 2
        modl = mod[l].reshape(SUBLANES, 1, 6 * D)
        casts = [(na_w_o if l % 2 == 0 else gqa_w_o, i), (ffn_w_down, l)]
        if l + 1 < DEPTH:
            casts.append((gqa_w_qkv if (l + 1) % 2 else na_w_qkv, (l + 1) // 2))
        if l % 2 == 0:
            qp, vp, kt, vt = _qkv_prompt(xp, modl, norm_mix_pre[l], w_qkv, i, n_a, na_kv,
                                         gqa=False, tm=tm_p)
            na_kv = (kt, vt)
            qs, kts, vs = _qkv_sample_na(xs, modl, map_s, norm_mix_pre[l], w_qkv, tm=tm_s)
            op, wup = _attn_prompt(qp, kt, vp, i, 1, [(ffn_w_up, l)])
            os_, w_o, wdn, *w_nxt = _attn_na_sample(qs, kts, vs, cna_kt, cna_vt, i, ptabs, casts)
        else:
            qg = (jnp.tile(gqa_q_norm[i], NH) * 0.125).reshape(1, D)
            kg = jnp.tile(gqa_k_norm[i], NKV)
            qkg = jnp.concatenate([qg[0], kg]).reshape(1, D + kvg)
            kg2 = jnp.broadcast_to(kg[:, None], (kvg, SEQ))
            qp, vp, kt, vt = _qkv_prompt(xp, modl, norm_mix_pre[l], w_qkv, i, n_b, gq_kv,
                                         gqa=True, tm=tm_p, extra=(bd, qg, kg2))
            gq_kv = (kt, vt)
            qs, kts, vs = _qkv_sample_gqa(xs, modl, map_s, norm_mix_pre[l], w_qkv, bd, qkg,
                                          rope_tabs, tm=tm_s)
            op, wup = _attn_prompt(qp, kt, vp, i, NH // NKV, [(ffn_w_up, l)])
            os_, w_o, wdn, *w_nxt = _attn_gqa_sample(qs, kts, vs, cg_kt, cg_vt, i, casts)
        if w_nxt:
            w_qkv = w_nxt[0][None]
        xp = _mix_ffn(op, xp, modl, map_p, norm_mix_post[l], w_o, norm_ffn_pre[l], norm_ffn_post[l],
                      wup, ffn_conv_w, cb, wdn, l, tm=tm_fp, halo=False)
        xs = _mix_ffn(os_, xs, modl, map_fs, norm_mix_post[l], w_o, norm_ffn_pre[l], norm_ffn_post[l],
                      wup, ffn_conv_w, cb, wdn, l, tm=tm_fs, halo=True)

    return (xp.reshape(BATCH, SEQ, D), xs.reshape(DEC_BATCH, DEC_SEQ, D),
            _cache_out(na_kv[0], NH), _cache_out(na_kv[1], NH),
            _cache_out(gq_kv[0], NKV), _cache_out(gq_kv[1], NKV))
```

```python
import functools

import jax
import jax.numpy as jnp
import numpy as np
from jax import lax
from jax.experimental import pallas as pl
from jax.experimental.pallas import tpu as pltpu

D = 1024
HD = 64
NH = 16
NKV = 4
DFF = 2816
DEPTH = 4
GRID_W = 64
ROWS = 32
NA_KH = 8
NA_KW = 16
ROPE_BASE = 10000.0
EPS = 1e-6
BATCH, SEQ = 32, 256
DEC_BATCH, DEC_SEQ = 2, 2048
PAST = 256
NP = BATCH * SEQ
NS = DEC_BATCH * DEC_SEQ

BF = jnp.bfloat16
F32 = jnp.float32
NEG = -1e30
LANES = 128
SUBLANES = 8
Q_SCALE = HD ** -0.5
MXU_TILE = 256
HEAD_BLOCK = MXU_TILE
FF_CHUNK = MXU_TILE
N_CHUNK = DFF // FF_CHUNK
TOKEN_TILE = 512
V7X_VMEM_BYTES = 64 * 1024 * 1024
VMEM_LIMIT = V7X_VMEM_BYTES - 14 * 1024 * 1024


def _params(sem):
    return pltpu.CompilerParams(dimension_semantics=sem, vmem_limit_bytes=VMEM_LIMIT)


def _rms(x, g):
    return x * lax.rsqrt(jnp.mean(x * x, axis=-1, keepdims=True) + EPS) * g


def _dot(a, b):
    return jnp.dot(a, b, preferred_element_type=F32)


def _dot_tn_nt(a, b):
    return lax.dot_general(a, b, (((0,), (1,)), ((), ())), preferred_element_type=F32)


def _qkv_weight_specs(gqa):
    if gqa:
        kvw = NKV * HD
        return [pl.BlockSpec((1, D, D), lambda t: (0, 0, 0)),
                pl.BlockSpec((1, D, kvw), lambda t: (0, 0, D // kvw)),
                pl.BlockSpec((1, D, kvw), lambda t: (0, 0, D // kvw + 1))]
    return [pl.BlockSpec((1, D, D), functools.partial(lambda t, c: (0, 0, c), c=c)) for c in range(3)]


def _mod_kernel(cond_ref, w_ref, b_ref, *rest):
    n_w = (len(rest) - 1) // 2
    o_ref = rest[n_w]
    _cast_slabs(rest[:n_w], rest[n_w + 1:])
    s = cond_ref[...]
    s = s * jax.nn.sigmoid(s)
    o_ref[0] = _dot(s.astype(BF), w_ref[0].astype(BF)) + b_ref[0]


def _adaln(cond8, ada_w, ada_b, weights=()):
    tn = 1536
    nn = 6 * D // tn
    w_in, w_out, w_shape = _cast_specs(weights, DEPTH * nn, lambda l, n: l * nn + n)
    return pl.pallas_call(
        _mod_kernel,
        grid=(DEPTH, nn),
        in_specs=[
            pl.BlockSpec((SUBLANES, D), lambda l, n: (0, 0)),
            pl.BlockSpec((1, D, tn), lambda l, n: (l, 0, n)),
            pl.BlockSpec((1, 1, tn), lambda l, n: (l, 0, n)),
        ] + w_in,
        out_specs=[pl.BlockSpec((1, SUBLANES, tn), lambda l, n: (l, 0, n))] + w_out,
        out_shape=[jax.ShapeDtypeStruct((DEPTH, SUBLANES, 6 * D), F32)] + w_shape,
        compiler_params=_params(("arbitrary", "arbitrary")),
        name="adaln",
    )(cond8, ada_w, ada_b.reshape(DEPTH, 1, 6 * D), *[w for w, _ in weights])


def _prenorm(x_ref, mod_ref, g_ref):
    m = mod_ref[0]
    return (_rms(x_ref[...], g_ref[...] * (1.0 + m[:, D:2 * D])) + m[:, 0:D]).astype(BF)


def _head_rms_lanes(blk, bd_ref):
    ss = _dot((blk * blk).astype(BF), bd_ref[...])
    return blk * lax.rsqrt(ss * (1.0 / HD) + EPS)


def _qkv_prompt_kernel(*refs, gqa, slot):
    if gqa:
        (x_ref, mod_ref, g_ref, wq_ref, wk_ref, wv_ref, bd_ref, qg_ref, kg_ref,
         q_ref, v_ref, kt_ref, vt_ref) = refs
    else:
        x_ref, mod_ref, g_ref, wq_ref, wk_ref, wv_ref, q_ref, v_ref, kt_ref, vt_ref = refs
    hb = _prenorm(x_ref, mod_ref, g_ref)
    q = _dot(hb, wq_ref[0])
    v = _dot(hb, wv_ref[0])
    kt = _dot_tn_nt(wk_ref[0], hb)
    if gqa:
        for t in range(D // HEAD_BLOCK):
            sl = slice(t * HEAD_BLOCK, (t + 1) * HEAD_BLOCK)
            q_ref[:, sl] = (_head_rms_lanes(q[:, sl], bd_ref) * qg_ref[:, sl]).astype(q_ref.dtype)
        heads = []
        for h in range(NKV):
            blk = kt[h * HD:(h + 1) * HD, :]
            heads.append(blk * lax.rsqrt(jnp.mean(blk * blk, axis=0, keepdims=True) + EPS))
        kt = jnp.concatenate(heads, axis=0)
    else:
        q_ref[...] = (q * Q_SCALE).astype(q_ref.dtype)
    v_ref[...] = v.astype(v_ref.dtype)
    for s in range(x_ref.shape[0] // SEQ):
        sl = slice(s * SEQ, (s + 1) * SEQ)
        kts = kt[:, sl]
        if gqa:
            kts = kts * kg_ref[...]
        kt_ref[s, slot] = kts
        vt_ref[s, slot] = v[sl, :].T
        for other in range(kt_ref.shape[1]):
            if other != slot:
                kt_ref[s, other] = jnp.zeros_like(kts)
                vt_ref[s, other] = jnp.zeros_like(kts)


def _qkv_prompt(x, modl, g, w, i, n_layers, prev, *, gqa, tm, extra=()):
    kvw = NKV * HD if gqa else D
    in_specs = [
        pl.BlockSpec((tm, D), lambda t: (t, 0)),
        pl.BlockSpec((1, 1, 6 * D), lambda t: (0, 0, 0)),
        pl.BlockSpec((1, D), lambda t: (0, 0)),
    ] + _qkv_weight_specs(gqa)
    args = [x, modl, g.reshape(1, D), w, w, w]
    if gqa:
        in_specs += [pl.BlockSpec((HEAD_BLOCK, HEAD_BLOCK), lambda t: (0, 0)),
                     pl.BlockSpec((1, D), lambda t: (0, 0)),
                     pl.BlockSpec((kvw, SEQ), lambda t: (0, 0))]
        args += list(extra)
    n_in = len(args)
    aliases = {}
    if prev is None:
        kv_spec = pl.BlockSpec((tm // SEQ, n_layers, kvw, SEQ), lambda t: (t, 0, 0, 0))
        slot = i
    else:
        in_specs += [pl.BlockSpec(memory_space=pl.ANY), pl.BlockSpec(memory_space=pl.ANY)]
        args += list(prev)
        aliases = {n_in: 2, n_in + 1: 3}
        kv_spec = pl.BlockSpec((tm // SEQ, 1, kvw, SEQ), lambda t: (t, i, 0, 0))
        slot = 0
    kv_shape = jax.ShapeDtypeStruct((BATCH, n_layers, kvw, SEQ), F32)

    def body(*refs):
        _qkv_prompt_kernel(*(refs[:n_in] + refs[len(args):]), gqa=gqa, slot=slot)

    return pl.pallas_call(
        body,
        grid=(NP // tm,),
        in_specs=in_specs,
        out_specs=[
            pl.BlockSpec((tm, D), lambda t: (t, 0)),
            pl.BlockSpec((tm, kvw), lambda t: (t, 0)),
            kv_spec, kv_spec,
        ],
        out_shape=[
            jax.ShapeDtypeStruct((NP, D), BF),
            jax.ShapeDtypeStruct((NP, kvw), BF),
            kv_shape, kv_shape,
        ],
        input_output_aliases=aliases,
        compiler_params=_params(("arbitrary",)),
        name="qkv_prompt_gqa" if gqa else "qkv_prompt_na",
    )(*args)


def _qkv_sample_na_kernel(x_ref, mod_ref, g_ref, wq_ref, wk_ref, wv_ref, q_ref, kt_ref, v_ref):
    hb = _prenorm(x_ref, mod_ref, g_ref)
    q_ref[...] = (_dot(hb, wq_ref[0]) * Q_SCALE).astype(q_ref.dtype)
    v_ref[...] = _dot(hb, wv_ref[0]).astype(v_ref.dtype)
    kt_ref[...] = _dot_tn_nt(wk_ref[0], hb).astype(kt_ref.dtype)


def _qkv_sample_na(x, modl, mod_map, g, w, *, tm):
    return pl.pallas_call(
        _qkv_sample_na_kernel,
        grid=(NS // tm,),
        in_specs=[
            pl.BlockSpec((tm, D), lambda t: (t, 0)),
            pl.BlockSpec((1, 1, 6 * D), mod_map),
            pl.BlockSpec((1, D), lambda t: (0, 0)),
        ] + _qkv_weight_specs(False),
        out_specs=[
            pl.BlockSpec((tm, D), lambda t: (t, 0)),
            pl.BlockSpec((D, tm), lambda t: (0, t)),
            pl.BlockSpec((tm, D), lambda t: (t, 0)),
        ],
        out_shape=[
            jax.ShapeDtypeStruct((NS, D), BF),
            jax.ShapeDtypeStruct((D, NS), BF),
            jax.ShapeDtypeStruct((NS, D), BF),
        ],
        compiler_params=_params(("arbitrary",)),
        name="qkv_sample_na",
    )(x, modl, g.reshape(1, D), w, w, w)


def _qkv_sample_gqa_kernel(x_ref, mod_ref, g_ref, w_ref, bd_ref, qkg_ref, cos_ref, s1_ref, s2_ref,
                           q_ref, kt_ref, v_ref):
    qkv = _dot(_prenorm(x_ref, mod_ref, g_ref), w_ref[0])
    nqk = D + NKV * HD
    for t in range(nqk // LANES):
        if t % 2 == 0:
            nrm = _head_rms_lanes(qkv[:, (t // 2) * HEAD_BLOCK:(t // 2 + 1) * HEAD_BLOCK], bd_ref)
        xt = nrm[:, (t % 2) * LANES:(t % 2 + 1) * LANES] * qkg_ref[:, t * LANES:(t + 1) * LANES]
        xt = (xt * cos_ref[...]
              + pltpu.roll(xt, LANES - 16, 1) * s1_ref[...]
              + pltpu.roll(xt, 16, 1) * s2_ref[...])
        if t < D // LANES:
            q_ref[:, t * LANES:(t + 1) * LANES] = xt.astype(q_ref.dtype)
        else:
            tk = t - D // LANES
            kt_ref[tk * LANES:(tk + 1) * LANES, :] = xt.T.astype(kt_ref.dtype)
    v_ref[...] = qkv[:, nqk:].astype(v_ref.dtype)


def _qkv_sample_gqa(x, modl, mod_map, g, w, bd, qkg, rope_tabs, *, tm):
    nq = w.shape[2]
    kvw = NKV * HD
    tps = DEC_SEQ // tm
    in_specs = [
        pl.BlockSpec((tm, D), lambda t: (t, 0)),
        pl.BlockSpec((1, 1, 6 * D), mod_map),
        pl.BlockSpec((1, D), lambda t: (0, 0)),
        pl.BlockSpec((1, D, nq), lambda t: (0, 0, 0)),
        pl.BlockSpec((HEAD_BLOCK, HEAD_BLOCK), lambda t: (0, 0)),
        pl.BlockSpec((1, D + kvw), lambda t: (0, 0)),
    ]
    in_specs += [pl.BlockSpec((tm, LANES), lambda t: (t % tps, 0)) for _ in rope_tabs]
    return pl.pallas_call(
        _qkv_sample_gqa_kernel,
        grid=(NS // tm,),
        in_specs=in_specs,
        out_specs=[
            pl.BlockSpec((tm, D), lambda t: (t, 0)),
            pl.BlockSpec((kvw, tm), lambda t: (0, t)),
            pl.BlockSpec((tm, kvw), lambda t: (t, 0)),
        ],
        out_shape=[
            jax.ShapeDtypeStruct((NS, D), BF),
            jax.ShapeDtypeStruct((kvw, NS), BF),
            jax.ShapeDtypeStruct((NS, kvw), BF),
        ],
        compiler_params=_params(("arbitrary",)),
        name="qkv_sample_gqa",
    )(x, modl, g.reshape(1, D), w, bd, qkg, *rope_tabs)


def _lane_lo(rows):
    return lax.broadcasted_iota(jnp.int32, (rows, LANES), 1) < HD


def _pipelined(n, first, second):
    nxt = first(0)
    for i in range(n):
        cur = nxt
        if i + 1 < n:
            nxt = first(i + 1)
        second(i, cur)


def _attn_grouped(problems, group, stack, *, ones_sum):
    tq = problems[0][0].shape[0]
    lo = _lane_lo(tq)
    lo_s = _lane_lo(stack * tq)
    pending = {}
    per = NH // stack

    def scores(n):
        q_ref, _, key_tiles, _ = problems[n // per]
        u = n % per
        g = u * stack // group
        parts = []
        for h in range(u * stack, (u + 1) * stack):
            q2 = q_ref[:, (h // 2) * LANES:(h // 2 + 1) * LANES]
            if h % 2 != g % 2:
                q2 = pltpu.roll(q2.astype(F32), HD, 1).astype(BF)
            parts.append(q2)
        qs = parts[0] if stack == 1 else jnp.concatenate(parts, axis=0)
        mine = lo_s if g % 2 == 0 else jnp.logical_not(lo_s)
        qsel = jnp.where(mine, qs, jnp.zeros_like(qs))
        return [_dot(qsel, kt) for kt in key_tiles(g)]

    def finish(n, ss):
        _, o_ref, _, value_tiles = problems[n // per]
        u = n % per
        g = u * stack // group
        m = ss[0].max(axis=-1, keepdims=True)
        for s in ss[1:]:
            m = jnp.maximum(m, s.max(axis=-1, keepdims=True))
        es = [jnp.exp(s - m) for s in ss]
        pv = None
        for e, v in zip(es, value_tiles(g)):
            if ones_sum:
                mine = _lane_lo(v.shape[0])
                if g % 2:
                    mine = jnp.logical_not(mine)
                v = jnp.where(mine, v, jnp.ones_like(v))
            d = _dot(e.astype(BF), v)
            pv = d if pv is None else pv + d
        if ones_sum:
            pv = pv / pltpu.roll(pv, HD, 1)
        else:
            l = es[0].sum(axis=-1, keepdims=True)
            for e in es[1:]:
                l = l + e.sum(axis=-1, keepdims=True)
            pv = pv / l
        for j in range(stack):
            h = u * stack + j
            pj = pv[j * tq:(j + 1) * tq]
            if h % 2 != g % 2:
                pj = pltpu.roll(pj, HD, 1)
            if h % 2 == 0:
                pending[h // 2] = pj
            else:
                sl = slice((h // 2) * LANES, (h // 2 + 1) * LANES)
                o_ref[:, sl] = jnp.where(lo, pending.pop(h // 2), pj).astype(o_ref.dtype)

    _pipelined(per * len(problems), scores, finish)


ATTN_P_SEQS = 4


def _attn_prompt_kernel(q_ref, kt_ref, v_ref, *rest, group):
    n_w = (len(rest) - 1) // 2
    o_ref = rest[n_w]
    _cast_slabs(rest[:n_w], rest[n_w + 1:])

    def tile(g):
        return slice((g // 2) * LANES, (g // 2 + 1) * LANES)

    def problem(s):
        rows = pl.ds(s * SEQ, SEQ)
        return (q_ref.at[rows, :], o_ref.at[rows, :],
                lambda g: [kt_ref[s, 0, tile(g), :].astype(BF)],
                lambda g: [v_ref[pl.ds(s * SEQ, SEQ), tile(g)]])
    _attn_grouped([problem(s) for s in range(ATTN_P_SEQS)], group, group, ones_sum=False)


def _attn_prompt(q, kt_all, v, i, group, weights):
    kvw = v.shape[1]
    rows = ATTN_P_SEQS * SEQ
    steps = BATCH // ATTN_P_SEQS
    w_in, w_out, w_shape = _cast_specs(weights, steps, lambda b: b)
    return pl.pallas_call(
        functools.partial(_attn_prompt_kernel, group=group),
        grid=(steps,),
        in_specs=[
            pl.BlockSpec((rows, D), lambda b: (b, 0)),
            pl.BlockSpec((ATTN_P_SEQS, 1, kvw, SEQ), lambda b: (b, i, 0, 0)),
            pl.BlockSpec((rows, kvw), lambda b: (b, 0)),
        ] + w_in,
        out_specs=[pl.BlockSpec((rows, D), lambda b: (b, 0))] + w_out,
        out_shape=[jax.ShapeDtypeStruct((NP, D), BF)] + w_shape,
        compiler_params=_params(("arbitrary",)),
        name="attn_prompt",
    )(q, kt_all, v, *[w for w, _ in weights])


def _cast_specs(weights, steps, slab):
    w_in, w_out, w_shape = [], [], []
    for w, layer in weights:
        _, r, c = w.shape
        w_in.append(pl.BlockSpec((1, r // steps, c),
                                 functools.partial(lambda *ids, layer: (layer, slab(*ids), 0), layer=layer)))
        w_out.append(pl.BlockSpec((r // steps, c), lambda *ids: (slab(*ids), 0)))
        w_shape.append(jax.ShapeDtypeStruct((r, c), BF))
    return w_in, w_out, w_shape


def _cast_slabs(w_refs, wb_refs):
    for w_ref, wb_ref in zip(w_refs, wb_refs):
        wb_ref[...] = w_ref[0].astype(BF)


def _attn_gqa_sample_kernel(q_ref, kt_ref, v_ref, kct_ref, vct_ref, *rest):
    n_w = (len(rest) - 1) // 2
    o_ref = rest[n_w]
    _cast_slabs(rest[:n_w], rest[n_w + 1:])
    vc = vct_ref[0, 0].T.astype(BF)

    def tile(g):
        return slice((g // 2) * LANES, (g // 2 + 1) * LANES)
    _attn_grouped([(q_ref, o_ref,
                    lambda g: [kt_ref[tile(g), :], kct_ref[0, 0, tile(g), :].astype(BF)],
                    lambda g: [v_ref[:, tile(g)], vc[:, tile(g)]])],
                  NH // NKV, NH // NKV, ones_sum=True)


GQA_TQ = 256


def _attn_gqa_sample(q, kt, v, cache_kt, cache_vt, i, weights=()):
    nq = DEC_SEQ // GQA_TQ
    kvw = NKV * HD
    w_in, w_out, w_shape = _cast_specs(weights, DEC_BATCH * nq, lambda b, t: b * nq + t)
    return pl.pallas_call(
        _attn_gqa_sample_kernel,
        grid=(DEC_BATCH, nq),
        in_specs=[
            pl.BlockSpec((GQA_TQ, D), lambda b, t: (b * nq + t, 0)),
            pl.BlockSpec((kvw, DEC_SEQ), lambda b, t: (0, b)),
            pl.BlockSpec((DEC_SEQ, kvw), lambda b, t: (b, 0)),
            pl.BlockSpec((1, 1, kvw, PAST), lambda b, t: (b, i, 0, 0)),
            pl.BlockSpec((1, 1, kvw, PAST), lambda b, t: (b, i, 0, 0)),
        ] + w_in,
        out_specs=[pl.BlockSpec((GQA_TQ, D), lambda b, t: (b * nq + t, 0))] + w_out,
        out_shape=[jax.ShapeDtypeStruct((NS, D), BF)] + w_shape,
        compiler_params=_params(("arbitrary", "arbitrary")),
        name="attn_gqa_sample",
    )(q, kt, v, cache_kt, cache_vt, *[w for w, _ in weights])


NA_QROWS = 4
NA_KBLK = NA_QROWS * GRID_W
NA_NKB = 3
NA_PAIRS = NA_QROWS // 2


def _attn_na_sample_kernel(*refs):
    q_ref = refs[0]
    kt_refs = refs[1:1 + NA_NKB]
    v_refs = refs[1 + NA_NKB:1 + 2 * NA_NKB]
    kct_ref, vct_ref, p_ref = refs[1 + 2 * NA_NKB:4 + 2 * NA_NKB]
    rest = refs[4 + 2 * NA_NKB:]
    n_w = (len(rest) - 1) // 2
    o_ref = rest[n_w]
    _cast_slabs(rest[:n_w], rest[n_w + 1:])
    rb = pl.program_id(1)
    kb0 = jnp.clip(rb - 1, 0, ROWS // NA_QROWS - NA_NKB)
    tq = q_ref.shape[0]
    lo = _lane_lo(tq)
    hi = jnp.logical_not(lo)
    lo_w = _lane_lo(GRID_W)
    vc = vct_ref[0, 0].T.astype(BF)
    ones = jnp.ones((tq, LANES), BF)

    tiles = []
    for i in range(NA_QROWS):
        qr = rb * NA_QROWS + i
        rs = jnp.clip(qr - NA_KH // 2, 0, ROWS - NA_KH)
        for j in range(NA_NKB):
            for jj in range(NA_PAIRS):
                kra = (kb0 + j) * NA_QROWS + 2 * jj
                krb = kra + 1
                idx = jnp.clip(kra - qr + NA_KH, 0, 2 * NA_KH - 1)
                va = jnp.logical_and(kra >= rs, kra < rs + NA_KH).astype(jnp.int32)
                vb = jnp.logical_and(krb >= rs, krb < rs + NA_KH).astype(jnp.int32)
                tiles.append((idx, jnp.where(lo_w, va, vb) > 0))

    def scores(h):
        rows = slice((h // 2) * LANES, (h // 2 + 1) * LANES)
        q2 = q_ref[:, rows]
        qsel = jnp.where(lo if h % 2 == 0 else hi, q2, jnp.zeros_like(q2))
        s_lat = [_dot(qsel, r[rows, :]) for r in kt_refs]
        out_rows = []
        for i in range(NA_QROWS):
            rsl = slice(i * GRID_W, (i + 1) * GRID_W)
            row = []
            for j in range(NA_NKB):
                for jj in range(NA_PAIRS):
                    idx, valid = tiles[(i * NA_NKB + j) * NA_PAIRS + jj]
                    t = s_lat[j][rsl, jj * LANES:(jj + 1) * LANES] + p_ref[0, h, idx]
                    row.append(jnp.where(valid, t, NEG))
            out_rows.append(jnp.concatenate(row, axis=1))
        return jnp.concatenate(out_rows, axis=0), _dot(qsel, kct_ref[0, 0, rows, :].astype(BF))

    def weighted_values(h, s, s_ctx):
        rows = slice((h // 2) * LANES, (h // 2 + 1) * LANES)
        mine = lo if h % 2 == 0 else hi
        m = jnp.maximum(s.max(axis=-1, keepdims=True), s_ctx.max(axis=-1, keepdims=True))
        e = jnp.exp(s - m).astype(BF)
        e_ctx = jnp.exp(s_ctx - m).astype(BF)
        pv = _dot(e_ctx, jnp.where(mine, vc[:, rows], ones))
        for j in range(NA_NKB):
            vj = jnp.where(mine, v_refs[j][:, rows], ones)
            pv = pv + _dot(e[:, j * NA_KBLK:(j + 1) * NA_KBLK], vj)
        return pv / pltpu.roll(pv, HD, 1)

    nxt = scores(0)
    o2 = None
    for h in range(NH):
        cur = nxt
        if h + 1 < NH:
            nxt = scores(h + 1)
        pv = weighted_values(h, *cur)
        if h % 2 == 0:
            o2 = pv
        else:
            o_ref[:, (h // 2) * LANES:(h // 2 + 1) * LANES] = jnp.where(lo, o2, pv).astype(o_ref.dtype)


def _attn_na_sample(q, kt, v, cache_kt, cache_vt, i, ptab, weights=()):
    nrb = ROWS // NA_QROWS
    w_in, w_out, w_shape = _cast_specs(weights, DEC_BATCH * nrb, lambda b, r: b * nrb + r)

    def kb(b, r, j):
        return b * nrb + jnp.clip(r - 1, 0, nrb - NA_NKB) + j

    in_specs = [pl.BlockSpec((NA_KBLK, D), lambda b, r: (b * nrb + r, 0))]
    in_specs += [pl.BlockSpec((D, NA_KBLK), functools.partial(lambda b, r, j: (0, kb(b, r, j)), j=j))
                 for j in range(NA_NKB)]
    in_specs += [pl.BlockSpec((NA_KBLK, D), functools.partial(lambda b, r, j: (kb(b, r, j), 0), j=j))
                 for j in range(NA_NKB)]
    in_specs += [
        pl.BlockSpec((1, 1, D, PAST), lambda b, r: (b, i, 0, 0)),
        pl.BlockSpec((1, 1, D, PAST), lambda b, r: (b, i, 0, 0)),
        pl.BlockSpec((1, NH, 2 * NA_KH, GRID_W, LANES), lambda b, r: (i, 0, 0, 0, 0),
                     pipeline_mode=pl.Buffered(1)),
    ]
    return pl.pallas_call(
        _attn_na_sample_kernel,
        grid=(DEC_BATCH, nrb),
        in_specs=in_specs + w_in,
        out_specs=[pl.BlockSpec((NA_KBLK, D), lambda b, r: (b * nrb + r, 0))] + w_out,
        out_shape=[jax.ShapeDtypeStruct((NS, D), BF)] + w_shape,
        compiler_params=_params(("arbitrary", "arbitrary")),
        name="attn_na_sample",
    )(q, *([kt] * NA_NKB), *([v] * NA_NKB), cache_kt, cache_vt, ptab, *[w for w, _ in weights])


N_DC = 2 * NA_KW - 1
N_DR = 2 * NA_KH - 1
N_ENT = N_DR + 1
DC_PAD = 32
N_SPLIT = 3
TILE_ELEMS = GRID_W * LANES


def _na_bias_kernel(lhs_ref, sel_ref, ok_ref, *rest):
    n_w = (len(rest) - 1) // 2
    o_ref = rest[n_w]
    _cast_slabs(rest[:n_w], rest[n_w + 1:])
    t = _dot(lhs_ref[0], sel_ref[...])
    ok = ok_ref[...] > 0.0
    for h in range(NH):
        rows = slice(h * N_ENT, (h + 1) * N_ENT)
        o_ref[0, rows, :] = jnp.where(ok, t[rows, :], NEG)


BIAS_COL_BLOCKS = 4


def _na_bias_tables(rpb, weights=()):
    n_a = rpb.shape[0]
    qc = jnp.arange(GRID_W)[:, None]
    lane = jnp.arange(LANES)[None, :]
    kc = lane % GRID_W
    half = lane // GRID_W
    c_start = jnp.clip(qc - NA_KW // 2, 0, GRID_W - NA_KW)
    ok = (kc >= c_start) & (kc < c_start + NA_KW)
    slot = (half * DC_PAD + kc - qc + (NA_KW - 1)).reshape(-1)
    sel = (jnp.arange(2 * DC_PAD)[:, None] == slot[None, :]) & ok.reshape(1, -1)
    sel = jnp.tile(sel.astype(BF), (N_SPLIT, 1))
    d = jnp.arange(N_ENT)[:, None, None]
    ok_d = ok[None] & jnp.where(half[None] == 0, d >= 1, d <= N_DR - 1)
    ok_d = ok_d.reshape(N_ENT, TILE_ELEMS).astype(F32)
    pad = jnp.zeros((n_a, NH, 1, N_DC), F32)
    left = jnp.concatenate([pad, rpb], axis=2)
    right = jnp.concatenate([rpb, pad], axis=2)
    padc = lambda a: jnp.pad(a, ((0, 0), (0, 0), (0, 0), (0, DC_PAD - N_DC)))
    both = jnp.concatenate([padc(left), padc(right)], axis=-1).reshape(n_a, NH * N_ENT, 2 * DC_PAD)
    pieces = []
    rest = both
    for _ in range(N_SPLIT):
        piece = rest.astype(BF)
        pieces.append(piece)
        rest = rest - piece.astype(F32)
    lhs = jnp.concatenate(pieces, axis=-1)
    k = N_SPLIT * 2 * DC_PAD
    ncb = BIAS_COL_BLOCKS
    cols = TILE_ELEMS // ncb
    w_in, w_out, w_shape = _cast_specs(weights, n_a * ncb, lambda a, cb: a * ncb + cb)
    out, *w_bf = pl.pallas_call(
        _na_bias_kernel,
        grid=(n_a, ncb),
        in_specs=[
            pl.BlockSpec((1, NH * N_ENT, k), lambda a, cb: (a, 0, 0)),
            pl.BlockSpec((k, cols), lambda a, cb: (0, cb)),
            pl.BlockSpec((N_ENT, cols), lambda a, cb: (0, cb)),
        ] + w_in,
        out_specs=[pl.BlockSpec((1, NH * N_ENT, cols), lambda a, cb: (a, 0, cb))] + w_out,
        out_shape=[jax.ShapeDtypeStruct((n_a, NH * N_ENT, TILE_ELEMS), F32)] + w_shape,
        compiler_params=_params(("arbitrary", "arbitrary")),
        name="na_bias",
    )(lhs, sel, ok_d, *[w for w, _ in weights])
    return [out.reshape(n_a, NH, N_ENT, GRID_W, LANES)] + w_bf


HALO_O = 16


def _mix_ffn_kernel(*refs, halo, tiles_per_seq):
    if halo:
        (o_ref, op_ref, on_ref, x_ref, xp_ref, xn_ref, mod_ref, gmix_ref, wo_ref,
         gpre_ref, gpost_ref, wup_ref, cw_ref, cb_ref, wdn_ref, out_ref, act_scr) = refs
    else:
        (o_ref, x_ref, mod_ref, gmix_ref, wo_ref,
         gpre_ref, gpost_ref, wup_ref, cw_ref, cb_ref, wdn_ref, out_ref, act_scr) = refs
    tm = x_ref.shape[0]
    m = mod_ref[0]
    gate_mix = m[:, 2 * D:3 * D]
    sh = m[:, 3 * D:4 * D]
    sc = m[:, 4 * D:5 * D]
    gate = m[:, 5 * D:6 * D]
    zeros = jnp.zeros((SUBLANES, D), F32)

    if halo:
        o_ext = jnp.concatenate([op_ref[...].astype(F32)[HALO_O - SUBLANES:], o_ref[...].astype(F32),
                                 on_ref[...].astype(F32)[:SUBLANES]], axis=0).astype(BF)
        x_ext = jnp.concatenate([xp_ref[...], x_ref[...], xn_ref[...]], axis=0)
    else:
        o_ext = o_ref[...]
        x_ext = x_ref[...]
    mrows = x_ext.shape[0]
    cut = (mrows // 2 + 15) // 16 * 16
    spans = [(0, cut), (cut, mrows)]
    ys = [_dot(o_ext[a:b], wo_ref[...]) for a, b in spans]
    g_mix = gate_mix * gmix_ref[...]
    g_pre = gpre_ref[...] * (1.0 + sc)
    x_parts, h_parts = [], []
    for (a, b), yv in zip(spans, ys):
        xv = x_ext[a:b] + _rms(yv, g_mix)
        x_parts.append(xv)
        h_parts.append(_rms(xv, g_pre) + sh)
    x_ext = jnp.concatenate(x_parts, axis=0)
    h = jnp.concatenate(h_parts, axis=0)
    if halo:
        t = pl.program_id(0) % tiles_per_seq
        x = x_ext[SUBLANES:SUBLANES + tm]
        hp = jnp.where(t == 0, zeros, h[:SUBLANES])
        hn = jnp.where(t == tiles_per_seq - 1, zeros, h[SUBLANES + tm:])
        hext = jnp.concatenate([hp, h[SUBLANES:SUBLANES + tm], hn], axis=0).astype(BF)
        starts = [SUBLANES]
        span = tm
    else:
        x = x_ext
        parts = [zeros]
        for s in range(tm // SEQ):
            parts += [h[s * SEQ:(s + 1) * SEQ], zeros]
        hext = jnp.concatenate(parts, axis=0).astype(BF)
        starts = [SUBLANES + s * (SEQ + SUBLANES) for s in range(tm // SEQ)]
        span = SEQ
    mext = hext.shape[0]

    def token_rows(v):
        parts = [v[r0:r0 + span] for r0 in starts]
        return parts[0] if len(parts) == 1 else jnp.concatenate(parts, axis=0)

    def conv(u, c0):
        cols = slice(c0, c0 + FF_CHUNK)
        prev = token_rows(pltpu.roll(u, 1, 0))
        nxt = token_rows(pltpu.roll(u, mext - 1, 0))
        return (prev * cw_ref[0, 0:1, cols] + token_rows(u) * cw_ref[0, 1:2, cols]
                + nxt * cw_ref[0, 2:3, cols] + cb_ref[0, :, cols])

    for j in range(N_CHUNK):
        ca = j * FF_CHUNK
        cg = DFF + j * FF_CHUNK
        a = conv(_dot(hext, wup_ref[:, ca:ca + FF_CHUNK]), ca)
        gt = conv(_dot(hext, wup_ref[:, cg:cg + FF_CHUNK]), cg)
        ha = 0.5 * a
        act_scr[:, ca:ca + FF_CHUNK] = ((ha + ha * jnp.tanh(ha)) * gt).astype(BF)
    g_post = gate * gpost_ref[...]
    half = tm // 2
    ys = [_dot(act_scr[r:r + half, :], wdn_ref[...]) for r in (0, half)]
    for r, yv in zip((0, half), ys):
        out_ref[r:r + half, :] = x[r:r + half] + _rms(yv, g_post)


def _mix_ffn(o, x, modl, mod_map, gmix, w_o, gpre, gpost, wup, cw, cb, wdn, l, *, tm, halo):
    n = x.shape[0]
    tps = DEC_SEQ // tm

    def prev_blk(rows):
        r = tm // rows
        return lambda t: (jnp.maximum(t * r - 1, 0), 0)

    def next_blk(rows):
        r = tm // rows
        return lambda t: (jnp.minimum((t + 1) * r, n // rows - 1), 0)

    in_specs = [pl.BlockSpec((tm, D), lambda t: (t, 0))]
    args = [o]
    if halo:
        in_specs += [pl.BlockSpec((HALO_O, D), prev_blk(HALO_O)), pl.BlockSpec((HALO_O, D), next_blk(HALO_O))]
        args += [o, o]
    in_specs.append(pl.BlockSpec((tm, D), lambda t: (t, 0)))
    args.append(x)
    if halo:
        in_specs += [pl.BlockSpec((SUBLANES, D), prev_blk(SUBLANES)),
                     pl.BlockSpec((SUBLANES, D), next_blk(SUBLANES))]
        args += [x, x]
    const = dict(pipeline_mode=pl.Buffered(1))
    in_specs += [
        pl.BlockSpec((1, 1, 6 * D), mod_map),
        pl.BlockSpec((1, D), lambda t: (0, 0)),
        pl.BlockSpec((D, D), lambda t: (0, 0), **const),
        pl.BlockSpec((1, D), lambda t: (0, 0)),
        pl.BlockSpec((1, D), lambda t: (0, 0)),
        pl.BlockSpec((D, 2 * DFF), lambda t: (0, 0), **const),
        pl.BlockSpec((1, 3, 2 * DFF), lambda t: (l, 0, 0)),
        pl.BlockSpec((1, 1, 2 * DFF), lambda t: (l, 0, 0)),
        pl.BlockSpec((DFF, D), lambda t: (0, 0), **const),
    ]
    args += [modl, gmix.reshape(1, D), w_o, gpre.reshape(1, D), gpost.reshape(1, D), wup, cw, cb, wdn]
    return pl.pallas_call(
        functools.partial(_mix_ffn_kernel, halo=halo, tiles_per_seq=tps),
        grid=(n // tm,),
        in_specs=in_specs,
        out_specs=pl.BlockSpec((tm, D), lambda t: (t, 0)),
        out_shape=jax.ShapeDtypeStruct((n, D), F32),
        scratch_shapes=[pltpu.VMEM((tm, DFF), BF)],
        compiler_params=_params(("arbitrary",)),
        name="mix_ffn",
    )(*args)


def _rope_tables():
    f32 = np.float32
    t = np.arange(DEC_SEQ)
    rows = (t // GRID_W).astype(f32)
    cols = (t % GRID_W).astype(f32)
    quarter = HD // 4
    freqs = f32(ROPE_BASE) ** (-np.arange(quarter, dtype=f32) / f32(quarter))
    lane = np.arange(LANES)
    d = lane % HD
    e = d % (HD // 2)
    is_x2 = (e >= quarter)[None, :]
    pos = np.where((d // (HD // 2) == 0)[None, :], rows[:, None], cols[:, None])
    ang = (pos * freqs[e % quarter][None, :]).astype(f32)
    cos = np.cos(ang).astype(f32)
    sin = np.sin(ang).astype(f32)
    zero = f32(0.0)
    return (jnp.asarray(cos), jnp.asarray(np.where(is_x2, zero, -sin)),
            jnp.asarray(np.where(is_x2, sin, zero)))


def _block_diag_ones():
    r = jnp.arange(HEAD_BLOCK)
    return (r[:, None] // HD == r[None, :] // HD).astype(BF)


def _cache_t(cache):
    b, n, p, h, d = cache.shape
    return cache.transpose(0, 1, 3, 4, 2).reshape(b, n, h * d, p)


def _cache_out(kt, heads):
    b, n, _, s = kt.shape
    return kt.reshape(b, n, heads, HD, s).transpose(0, 1, 4, 2, 3)


def kernel(x_prompt, x_sample, cache_na_k, cache_na_v, cache_gqa_k, cache_gqa_v, c, c_ctx,
           ada_w, ada_b, norm_mix_pre, norm_mix_post, norm_ffn_pre, norm_ffn_post,
           na_w_qkv, na_w_o, na_rpb, gqa_w_qkv, gqa_w_o, gqa_q_norm, gqa_k_norm,
           ffn_w_up, ffn_conv_w, ffn_conv_b, ffn_w_down):
    n_a = cache_na_k.shape[1]
    n_b = cache_gqa_k.shape[1]
    cond8 = jnp.concatenate([c_ctx[None], c, jnp.zeros((SUBLANES - 1 - DEC_BATCH, D), F32)], axis=0)
    mod, = _adaln(cond8, ada_w, ada_b)

    tm_p = tm_s = tm_fp = tm_fs = TOKEN_TILE
    map_p = lambda t: (0, 0, 0)
    map_s = lambda t: (1 + t // (DEC_SEQ // tm_s), 0, 0)
    map_fs = lambda t: (1 + t // (DEC_SEQ // tm_fs), 0, 0)

    cna_kt, cna_vt = _cache_t(cache_na_k), _cache_t(cache_na_v)
    cg_kt, cg_vt = _cache_t(cache_gqa_k), _cache_t(cache_gqa_v)
    rope_tabs = _rope_tables()
    bd = _block_diag_ones()
    ptabs, w_qkv = _na_bias_tables(na_rpb, [(na_w_qkv, 0)])
    w_qkv = w_qkv[None]
    kvg = NKV * HD

    cb = ffn_conv_b.reshape(DEPTH, 1, 2 * DFF)

    xp = x_prompt.reshape(NP, D)
    xs = x_sample.reshape(NS, D)
    na_kv = None
    gq_kv = None
    for l in range(DEPTH):
        i = l // 2
        modl = mod[l].reshape(SUBLANES, 1, 6 * D)
        casts = [(na_w_o if l % 2 == 0 else gqa_w_o, i), (ffn_w_down, l)]
        if l + 1 < DEPTH:
            casts.append((gqa_w_qkv if (l + 1) % 2 else na_w_qkv, (l + 1) // 2))
        if l % 2 == 0:
            qp, vp, kt, vt = _qkv_prompt(xp, modl, norm_mix_pre[l], w_qkv, i, n_a, na_kv,
                                         gqa=False, tm=tm_p)
            na_kv = (kt, vt)
            qs, kts, vs = _qkv_sample_na(xs, modl, map_s, norm_mix_pre[l], w_qkv, tm=tm_s)
            op, wup = _attn_prompt(qp, kt, vp, i, 1, [(ffn_w_up, l)])
            os_, w_o, wdn, *w_nxt = _attn_na_sample(qs, kts, vs, cna_kt, cna_vt, i, ptabs, casts)
        else:
            qg = (jnp.tile(gqa_q_norm[i], NH) * Q_SCALE).reshape(1, D)
            kg = jnp.tile(gqa_k_norm[i], NKV)
            qkg = jnp.concatenate([qg[0], kg]).reshape(1, D + kvg)
            kg2 = jnp.broadcast_to(kg[:, None], (kvg, SEQ))
            qp, vp, kt, vt = _qkv_prompt(xp, modl, norm_mix_pre[l], w_qkv, i, n_b, gq_kv,
                                         gqa=True, tm=tm_p, extra=(bd, qg, kg2))
            gq_kv = (kt, vt)
            qs, kts, vs = _qkv_sample_gqa(xs, modl, map_s, norm_mix_pre[l], w_qkv, bd, qkg,
                                          rope_tabs, tm=tm_s)
            op, wup = _attn_prompt(qp, kt, vp, i, NH // NKV, [(ffn_w_up, l)])
            os_, w_o, wdn, *w_nxt = _attn_gqa_sample(qs, kts, vs, cg_kt, cg_vt, i, casts)
        if w_nxt:
            w_qkv = w_nxt[0][None]
        xp = _mix_ffn(op, xp, modl, map_p, norm_mix_post[l], w_o, norm_ffn_pre[l], norm_ffn_post[l],
                      wup, ffn_conv_w, cb, wdn, l, tm=tm_fp, halo=False)
        xs = _mix_ffn(os_, xs, modl, map_fs, norm_mix_post[l], w_o, norm_ffn_pre[l], norm_ffn_post[l],
                      wup, ffn_conv_w, cb, wdn, l, tm=tm_fs, halo=True)

    return (xp.reshape(BATCH, SEQ, D), xs.reshape(DEC_BATCH, DEC_SEQ, D),
            _cache_out(na_kv[0], NH), _cache_out(na_kv[1], NH),
            _cache_out(gq_kv[0], NKV), _cache_out(gq_kv[1], NKV))
```

```python
import functools

import jax
import jax.numpy as jnp
import numpy as np
from jax import lax
from jax.experimental import pallas as pl
from jax.experimental.pallas import tpu as pltpu

D = 1024
HD = 64
NH = 16
NKV = 4
DFF = 2816
DEPTH = 4
GRID_W = 64
ROWS = 32
NA_KH = 8
NA_KW = 16
ROPE_BASE = 10000.0
EPS = 1e-6
BATCH, SEQ = 32, 256
DEC_BATCH, DEC_SEQ = 2, 2048
PAST = 256
NP = BATCH * SEQ
NS = DEC_BATCH * DEC_SEQ

BF = jnp.bfloat16
F32 = jnp.float32
NEG = -1e30
LANES = 128
SUBLANES = 8
Q_SCALE = HD ** -0.5
MXU_TILE = 256
HEAD_BLOCK = MXU_TILE
FF_CHUNK = MXU_TILE
N_CHUNK = DFF // FF_CHUNK
TOKEN_TILE = 512
V7X_VMEM_BYTES = 64 * 1024 * 1024
VMEM_LIMIT = V7X_VMEM_BYTES - 14 * 1024 * 1024


def _params(sem):
    return pltpu.CompilerParams(dimension_semantics=sem, vmem_limit_bytes=VMEM_LIMIT)


def _rms(x, g):
    return x * lax.rsqrt(jnp.mean(x * x, axis=-1, keepdims=True) + EPS) * g


def _dot(a, b):
    return jnp.dot(a, b, preferred_element_type=F32)


def _dot_tn_nt(a, b):
    return lax.dot_general(a, b, (((0,), (1,)), ((), ())), preferred_element_type=F32)


def _qkv_weight_specs(gqa):
    if gqa:
        kvw = NKV * HD
        return [pl.BlockSpec((1, D, D), lambda t: (0, 0, 0)),
                pl.BlockSpec((1, D, kvw), lambda t: (0, 0, D // kvw)),
                pl.BlockSpec((1, D, kvw), lambda t: (0, 0, D // kvw + 1))]
    return [pl.BlockSpec((1, D, D), functools.partial(lambda t, c: (0, 0, c), c=c)) for c in range(3)]


def _mod_kernel(cond_ref, w_ref, b_ref, *rest):
    n_w = (len(rest) - 1) // 2
    o_ref = rest[n_w]
    _cast_slabs(rest[:n_w], rest[n_w + 1:])
    s = cond_ref[...]
    s = s * jax.nn.sigmoid(s)
    o_ref[0] = _dot(s.astype(BF), w_ref[0].astype(BF)) + b_ref[0]


def _adaln(cond8, ada_w, ada_b, weights=()):
    tn = 1536
    nn = 6 * D // tn
    w_in, w_out, w_shape = _cast_specs(weights, DEPTH * nn, lambda l, n: l * nn + n)
    return pl.pallas_call(
        _mod_kernel,
        grid=(DEPTH, nn),
        in_specs=[
            pl.BlockSpec((SUBLANES, D), lambda l, n: (0, 0)),
            pl.BlockSpec((1, D, tn), lambda l, n: (l, 0, n)),
            pl.BlockSpec((1, 1, tn), lambda l, n: (l, 0, n)),
        ] + w_in,
        out_specs=[pl.BlockSpec((1, SUBLANES, tn), lambda l, n: (l, 0, n))] + w_out,
        out_shape=[jax.ShapeDtypeStruct((DEPTH, SUBLANES, 6 * D), F32)] + w_shape,
        compiler_params=_params(("arbitrary", "arbitrary")),
        name="adaln",
    )(cond8, ada_w, ada_b.reshape(DEPTH, 1, 6 * D), *[w for w, _ in weights])


def _prenorm(x_ref, mod_ref, g_ref):
    m = mod_ref[0]
    return (_rms(x_ref[...], g_ref[...] * (1.0 + m[:, D:2 * D])) + m[:, 0:D]).astype(BF)


def _head_rms_lanes(blk, bd_ref):
    ss = _dot((blk * blk).astype(BF), bd_ref[...])
    return blk * lax.rsqrt(ss * (1.0 / HD) + EPS)


def _qkv_prompt_kernel(*refs, gqa, slot):
    if gqa:
        (x_ref, mod_ref, g_ref, wq_ref, wk_ref, wv_ref, bd_ref, qg_ref, kg_ref,
         q_ref, v_ref, kt_ref, vt_ref) = refs
    else:
        x_ref, mod_ref, g_ref, wq_ref, wk_ref, wv_ref, q_ref, v_ref, kt_ref, vt_ref = refs
    hb = _prenorm(x_ref, mod_ref, g_ref)
    q = _dot(hb, wq_ref[0])
    v = _dot(hb, wv_ref[0])
    kt = _dot_tn_nt(wk_ref[0], hb)
    if gqa:
        for t in range(D // HEAD_BLOCK):
            sl = slice(t * HEAD_BLOCK, (t + 1) * HEAD_BLOCK)
            q_ref[:, sl] = (_head_rms_lanes(q[:, sl], bd_ref) * qg_ref[:, sl]).astype(q_ref.dtype)
        heads = []
        for h in range(NKV):
            blk = kt[h * HD:(h + 1) * HD, :]
            heads.append(blk * lax.rsqrt(jnp.mean(blk * blk, axis=0, keepdims=True) + EPS))
        kt = jnp.concatenate(heads, axis=0)
    else:
        q_ref[...] = (q * Q_SCALE).astype(q_ref.dtype)
    v_ref[...] = v.astype(v_ref.dtype)
    for s in range(x_ref.shape[0] // SEQ):
        sl = slice(s * SEQ, (s + 1) * SEQ)
        kts = kt[:, sl]
        if gqa:
            kts = kts * kg_ref[...]
        kt_ref[s, slot] = kts
        vt_ref[s, slot] = v[sl, :].T
        for other in range(kt_ref.shape[1]):
            if other != slot:
                kt_ref[s, other] = jnp.zeros_like(kts)
                vt_ref[s, other] = jnp.zeros_like(kts)


def _qkv_prompt(x, modl, g, w, i, n_layers, prev, *, gqa, tm, extra=()):
    kvw = NKV * HD if gqa else D
    in_specs = [
        pl.BlockSpec((tm, D), lambda t: (t, 0)),
        pl.BlockSpec((1, 1, 6 * D), lambda t: (0, 0, 0)),
        pl.BlockSpec((1, D), lambda t: (0, 0)),
    ] + _qkv_weight_specs(gqa)
    args = [x, modl, g.reshape(1, D), w, w, w]
    if gqa:
        in_specs += [pl.BlockSpec((HEAD_BLOCK, HEAD_BLOCK), lambda t: (0, 0)),
                     pl.BlockSpec((1, D), lambda t: (0, 0)),
                     pl.BlockSpec((kvw, SEQ), lambda t: (0, 0))]
        args += list(extra)
    n_in = len(args)
    aliases = {}
    if prev is None:
        kv_spec = pl.BlockSpec((tm // SEQ, n_layers, kvw, SEQ), lambda t: (t, 0, 0, 0))
        slot = i
    else:
        in_specs += [pl.BlockSpec(memory_space=pl.ANY), pl.BlockSpec(memory_space=pl.ANY)]
        args += list(prev)
        aliases = {n_in: 2, n_in + 1: 3}
        kv_spec = pl.BlockSpec((tm // SEQ, 1, kvw, SEQ), lambda t: (t, i, 0, 0))
        slot = 0
    kv_shape = jax.ShapeDtypeStruct((BATCH, n_layers, kvw, SEQ), F32)

    def body(*refs):
        _qkv_prompt_kernel(*(refs[:n_in] + refs[len(args):]), gqa=gqa, slot=slot)

    return pl.pallas_call(
        body,
        grid=(NP // tm,),
        in_specs=in_specs,
        out_specs=[
            pl.BlockSpec((tm, D), lambda t: (t, 0)),
            pl.BlockSpec((tm, kvw), lambda t: (t, 0)),
            kv_spec, kv_spec,
        ],
        out_shape=[
            jax.ShapeDtypeStruct((NP, D), BF),
            jax.ShapeDtypeStruct((NP, kvw), BF),
            kv_shape, kv_shape,
        ],
        input_output_aliases=aliases,
        compiler_params=_params(("arbitrary",)),
        name="qkv_prompt_gqa" if gqa else "qkv_prompt_na",
    )(*args)


def _qkv_sample_na_kernel(x_ref, mod_ref, g_ref, wq_ref, wk_ref, wv_ref, q_ref, kt_ref, v_ref):
    hb = _prenorm(x_ref, mod_ref, g_ref)
    q_ref[...] = (_dot(hb, wq_ref[0]) * Q_SCALE).astype(q_ref.dtype)
    v_ref[...] = _dot(hb, wv_ref[0]).astype(v_ref.dtype)
    kt_ref[...] = _dot_tn_nt(wk_ref[0], hb).astype(kt_ref.dtype)


def _qkv_sample_na(x, modl, mod_map, g, w, *, tm):
    return pl.pallas_call(
        _qkv_sample_na_kernel,
        grid=(NS // tm,),
        in_specs=[
            pl.BlockSpec((tm, D), lambda t: (t, 0)),
            pl.BlockSpec((1, 1, 6 * D), mod_map),
            pl.BlockSpec((1, D), lambda t: (0, 0)),
        ] + _qkv_weight_specs(False),
        out_specs=[
            pl.BlockSpec((tm, D), lambda t: (t, 0)),
            pl.BlockSpec((D, tm), lambda t: (0, t)),
            pl.BlockSpec((tm, D), lambda t: (t, 0)),
        ],
        out_shape=[
            jax.ShapeDtypeStruct((NS, D), BF),
            jax.ShapeDtypeStruct((D, NS), BF),
            jax.ShapeDtypeStruct((NS, D), BF),
        ],
        compiler_params=_params(("arbitrary",)),
        name="qkv_sample_na",
    )(x, modl, g.reshape(1, D), w, w, w)


def _qkv_sample_gqa_kernel(x_ref, mod_ref, g_ref, w_ref, bd_ref, qkg_ref, cos_ref, s1_ref, s2_ref,
                           q_ref, kt_ref, v_ref):
    qkv = _dot(_prenorm(x_ref, mod_ref, g_ref), w_ref[0])
    nqk = D + NKV * HD
    for t in range(nqk // LANES):
        if t % 2 == 0:
            nrm = _head_rms_lanes(qkv[:, (t // 2) * HEAD_BLOCK:(t // 2 + 1) * HEAD_BLOCK], bd_ref)
        xt = nrm[:, (t % 2) * LANES:(t % 2 + 1) * LANES] * qkg_ref[:, t * LANES:(t + 1) * LANES]
        xt = (xt * cos_ref[...]
              + pltpu.roll(xt, LANES - 16, 1) * s1_ref[...]
              + pltpu.roll(xt, 16, 1) * s2_ref[...])
        if t < D // LANES:
            q_ref[:, t * LANES:(t + 1) * LANES] = xt.astype(q_ref.dtype)
        else:
            tk = t - D // LANES
            kt_ref[tk * LANES:(tk + 1) * LANES, :] = xt.T.astype(kt_ref.dtype)
    v_ref[...] = qkv[:, nqk:].astype(v_ref.dtype)


def _qkv_sample_gqa(x, modl, mod_map, g, w, bd, qkg, rope_tabs, *, tm):
    nq = w.shape[2]
    kvw = NKV * HD
    tps = DEC_SEQ // tm
    in_specs = [
        pl.BlockSpec((tm, D), lambda t: (t, 0)),
        pl.BlockSpec((1, 1, 6 * D), mod_map),
        pl.BlockSpec((1, D), lambda t: (0, 0)),
        pl.BlockSpec((1, D, nq), lambda t: (0, 0, 0)),
        pl.BlockSpec((HEAD_BLOCK, HEAD_BLOCK), lambda t: (0, 0)),
        pl.BlockSpec((1, D + kvw), lambda t: (0, 0)),
    ]
    in_specs += [pl.BlockSpec((tm, LANES), lambda t: (t % tps, 0)) for _ in rope_tabs]
    return pl.pallas_call(
        _qkv_sample_gqa_kernel,
        grid=(NS // tm,),
        in_specs=in_specs,
        out_specs=[
            pl.BlockSpec((tm, D), lambda t: (t, 0)),
            pl.BlockSpec((kvw, tm), lambda t: (0, t)),
            pl.BlockSpec((tm, kvw), lambda t: (t, 0)),
        ],
        out_shape=[
            jax.ShapeDtypeStruct((NS, D), BF),
            jax.ShapeDtypeStruct((kvw, NS), BF),
            jax.ShapeDtypeStruct((NS, kvw), BF),
        ],
        compiler_params=_params(("arbitrary",)),
        name="qkv_sample_gqa",
    )(x, modl, g.reshape(1, D), w, bd, qkg, *rope_tabs)


def _lane_lo(rows):
    return lax.broadcasted_iota(jnp.int32, (rows, LANES), 1) < HD


def _pipelined(n, first, second):
    nxt = first(0)
    for i in range(n):
        cur = nxt
        if i + 1 < n:
            nxt = first(i + 1)
        second(i, cur)


def _swap_halves(a, axis):
    h = a.shape[axis] // 2
    lo, hi = (a[:h], a[h:]) if axis == 0 else (a[:, :h], a[:, h:])
    return jnp.concatenate([hi, lo], axis=axis)


def _attn_grouped(problems, group, stack, *, ones_sum):
    tq = problems[0][0].shape[0]
    lo = _lane_lo(tq)
    pending = {}
    per = NH // stack

    def parts(u):
        g = u * stack // group
        heads = range(u * stack, (u + 1) * stack)
        same = [h for h in heads if h % 2 == g % 2]
        other = [h for h in heads if h % 2 != g % 2]
        return g, [(p, swapped) for p, swapped in ((same, False), (other, True)) if p]

    def scores(n):
        q_ref, _, key_tiles, _ = problems[n // per]
        g, plist = parts(n % per)
        out = []
        for heads, swapped in plist:
            qs = [q_ref[:, (h // 2) * LANES:(h // 2 + 1) * LANES] for h in heads]
            qs = qs[0] if len(qs) == 1 else jnp.concatenate(qs, axis=0)
            mine = _lane_lo(qs.shape[0])
            if heads[0] % 2:
                mine = jnp.logical_not(mine)
            qsel = jnp.where(mine, qs, jnp.zeros_like(qs))
            kts = key_tiles(g)
            if swapped:
                kts = [_swap_halves(kt, 0) for kt in kts]
            out.append([_dot(qsel, kt) for kt in kts])
        return out

    def finish(n, sss):
        _, o_ref, _, value_tiles = problems[n // per]
        g, plist = parts(n % per)
        for (heads, swapped), ss in zip(plist, sss):
            m = ss[0].max(axis=-1, keepdims=True)
            for s in ss[1:]:
                m = jnp.maximum(m, s.max(axis=-1, keepdims=True))
            es = [jnp.exp(s - m) for s in ss]
            pv = None
            for e, v in zip(es, value_tiles(g)):
                if swapped:
                    v = _swap_halves(v, 1)
                if ones_sum:
                    mine = _lane_lo(v.shape[0])
                    if heads[0] % 2:
                        mine = jnp.logical_not(mine)
                    v = jnp.where(mine, v, jnp.ones_like(v))
                d = _dot(e.astype(BF), v)
                pv = d if pv is None else pv + d
            if ones_sum:
                pv = pv / pltpu.roll(pv, HD, 1)
            else:
                l = es[0].sum(axis=-1, keepdims=True)
                for e in es[1:]:
                    l = l + e.sum(axis=-1, keepdims=True)
                pv = pv / l
            for j, h in enumerate(heads):
                pending[h] = pv[j * tq:(j + 1) * tq]
        for h in sorted({h - h % 2 for h in range((n % per) * stack, (n % per + 1) * stack)}):
            if h in pending and h + 1 in pending:
                sl = slice((h // 2) * LANES, (h // 2 + 1) * LANES)
                o_ref[:, sl] = jnp.where(lo, pending.pop(h), pending.pop(h + 1)).astype(o_ref.dtype)

    _pipelined(per * len(problems), scores, finish)


ATTN_P_SEQS = 4


def _attn_prompt_kernel(q_ref, kt_ref, v_ref, *rest, group):
    n_w = (len(rest) - 1) // 2
    o_ref = rest[n_w]
    _cast_slabs(rest[:n_w], rest[n_w + 1:])

    def tile(g):
        return slice((g // 2) * LANES, (g // 2 + 1) * LANES)

    def problem(s):
        rows = pl.ds(s * SEQ, SEQ)
        return (q_ref.at[rows, :], o_ref.at[rows, :],
                lambda g: [kt_ref[s, 0, tile(g), :].astype(BF)],
                lambda g: [v_ref[pl.ds(s * SEQ, SEQ), tile(g)]])
    _attn_grouped([problem(s) for s in range(ATTN_P_SEQS)], group, group, ones_sum=False)


def _attn_prompt(q, kt_all, v, i, group, weights):
    kvw = v.shape[1]
    rows = ATTN_P_SEQS * SEQ
    steps = BATCH // ATTN_P_SEQS
    w_in, w_out, w_shape = _cast_specs(weights, steps, lambda b: b)
    return pl.pallas_call(
        functools.partial(_attn_prompt_kernel, group=group),
        grid=(steps,),
        in_specs=[
            pl.BlockSpec((rows, D), lambda b: (b, 0)),
            pl.BlockSpec((ATTN_P_SEQS, 1, kvw, SEQ), lambda b: (b, i, 0, 0)),
            pl.BlockSpec((rows, kvw), lambda b: (b, 0)),
        ] + w_in,
        out_specs=[pl.BlockSpec((rows, D), lambda b: (b, 0))] + w_out,
        out_shape=[jax.ShapeDtypeStruct((NP, D), BF)] + w_shape,
        compiler_params=_params(("arbitrary",)),
        name="attn_prompt",
    )(q, kt_all, v, *[w for w, _ in weights])


def _cast_specs(weights, steps, slab):
    w_in, w_out, w_shape = [], [], []
    for w, layer in weights:
        _, r, c = w.shape
        w_in.append(pl.BlockSpec((1, r // steps, c),
                                 functools.partial(lambda *ids, layer: (layer, slab(*ids), 0), layer=layer)))
        w_out.append(pl.BlockSpec((r // steps, c), lambda *ids: (slab(*ids), 0)))
        w_shape.append(jax.ShapeDtypeStruct((r, c), BF))
    return w_in, w_out, w_shape


def _cast_slabs(w_refs, wb_refs):
    for w_ref, wb_ref in zip(w_refs, wb_refs):
        wb_ref[...] = w_ref[0].astype(BF)


def _attn_gqa_sample_kernel(q_ref, kt_ref, v_ref, kct_ref, vct_ref, *rest):
    n_w = (len(rest) - 1) // 2
    o_ref = rest[n_w]
    _cast_slabs(rest[:n_w], rest[n_w + 1:])
    vc = vct_ref[0, 0].T.astype(BF)

    def tile(g):
        return slice((g // 2) * LANES, (g // 2 + 1) * LANES)
    _attn_grouped([(q_ref, o_ref,
                    lambda g: [kt_ref[tile(g), :], kct_ref[0, 0, tile(g), :].astype(BF)],
                    lambda g: [v_ref[:, tile(g)], vc[:, tile(g)]])],
                  NH // NKV, NH // NKV, ones_sum=True)


GQA_TQ = 256


def _attn_gqa_sample(q, kt, v, cache_kt, cache_vt, i, weights=()):
    nq = DEC_SEQ // GQA_TQ
    kvw = NKV * HD
    w_in, w_out, w_shape = _cast_specs(weights, DEC_BATCH * nq, lambda b, t: b * nq + t)
    return pl.pallas_call(
        _attn_gqa_sample_kernel,
        grid=(DEC_BATCH, nq),
        in_specs=[
            pl.BlockSpec((GQA_TQ, D), lambda b, t: (b * nq + t, 0)),
            pl.BlockSpec((kvw, DEC_SEQ), lambda b, t: (0, b)),
            pl.BlockSpec((DEC_SEQ, kvw), lambda b, t: (b, 0)),
            pl.BlockSpec((1, 1, kvw, PAST), lambda b, t: (b, i, 0, 0)),
            pl.BlockSpec((1, 1, kvw, PAST), lambda b, t: (b, i, 0, 0)),
        ] + w_in,
        out_specs=[pl.BlockSpec((GQA_TQ, D), lambda b, t: (b * nq + t, 0))] + w_out,
        out_shape=[jax.ShapeDtypeStruct((NS, D), BF)] + w_shape,
        compiler_params=_params(("arbitrary", "arbitrary")),
        name="attn_gqa_sample",
    )(q, kt, v, cache_kt, cache_vt, *[w for w, _ in weights])


NA_QROWS = 4
NA_KBLK = NA_QROWS * GRID_W
NA_NKB = 3
NA_PAIRS = NA_QROWS // 2


def _attn_na_sample_kernel(*refs):
    q_ref = refs[0]
    kt_refs = refs[1:1 + NA_NKB]
    v_refs = refs[1 + NA_NKB:1 + 2 * NA_NKB]
    kct_ref, vct_ref, p_ref = refs[1 + 2 * NA_NKB:4 + 2 * NA_NKB]
    rest = refs[4 + 2 * NA_NKB:]
    n_w = (len(rest) - 1) // 2
    o_ref = rest[n_w]
    _cast_slabs(rest[:n_w], rest[n_w + 1:])
    rb = pl.program_id(1)
    kb0 = jnp.clip(rb - 1, 0, ROWS // NA_QROWS - NA_NKB)
    tq = q_ref.shape[0]
    lo = _lane_lo(tq)
    hi = jnp.logical_not(lo)
    lo_w = _lane_lo(GRID_W)
    vc = vct_ref[0, 0].T.astype(BF)
    ones = jnp.ones((tq, LANES), BF)

    tiles = []
    for i in range(NA_QROWS):
        qr = rb * NA_QROWS + i
        rs = jnp.clip(qr - NA_KH // 2, 0, ROWS - NA_KH)
        for j in range(NA_NKB):
            for jj in range(NA_PAIRS):
                kra = (kb0 + j) * NA_QROWS + 2 * jj
                krb = kra + 1
                idx = jnp.clip(kra - qr + NA_KH, 0, 2 * NA_KH - 1)
                va = jnp.logical_and(kra >= rs, kra < rs + NA_KH).astype(jnp.int32)
                vb = jnp.logical_and(krb >= rs, krb < rs + NA_KH).astype(jnp.int32)
                tiles.append((idx, jnp.where(lo_w, va, vb) > 0))

    def scores(h):
        rows = slice((h // 2) * LANES, (h // 2 + 1) * LANES)
        q2 = q_ref[:, rows]
        qsel = jnp.where(lo if h % 2 == 0 else hi, q2, jnp.zeros_like(q2))
        s_lat = [_dot(qsel, r[rows, :]) for r in kt_refs]
        out_rows = []
        for i in range(NA_QROWS):
            rsl = slice(i * GRID_W, (i + 1) * GRID_W)
            row = []
            for j in range(NA_NKB):
                for jj in range(NA_PAIRS):
                    idx, valid = tiles[(i * NA_NKB + j) * NA_PAIRS + jj]
                    t = s_lat[j][rsl, jj * LANES:(jj + 1) * LANES] + p_ref[0, h, idx]
                    row.append(jnp.where(valid, t, NEG))
            out_rows.append(jnp.concatenate(row, axis=1))
        return jnp.concatenate(out_rows, axis=0), _dot(qsel, kct_ref[0, 0, rows, :].astype(BF))

    def weighted_values(h, s, s_ctx):
        rows = slice((h // 2) * LANES, (h // 2 + 1) * LANES)
        mine = lo if h % 2 == 0 else hi
        m = jnp.maximum(s.max(axis=-1, keepdims=True), s_ctx.max(axis=-1, keepdims=True))
        e = jnp.exp(s - m).astype(BF)
        e_ctx = jnp.exp(s_ctx - m).astype(BF)
        pv = _dot(e_ctx, jnp.where(mine, vc[:, rows], ones))
        for j in range(NA_NKB):
            vj = jnp.where(mine, v_refs[j][:, rows], ones)
            pv = pv + _dot(e[:, j * NA_KBLK:(j + 1) * NA_KBLK], vj)
        return pv / pltpu.roll(pv, HD, 1)

    nxt = scores(0)
    o2 = None
    for h in range(NH):
        cur = nxt
        if h + 1 < NH:
            nxt = scores(h + 1)
        pv = weighted_values(h, *cur)
        if h % 2 == 0:
            o2 = pv
        else:
            o_ref[:, (h // 2) * LANES:(h // 2 + 1) * LANES] = jnp.where(lo, o2, pv).astype(o_ref.dtype)


def _attn_na_sample(q, kt, v, cache_kt, cache_vt, i, ptab, weights=()):
    nrb = ROWS // NA_QROWS
    w_in, w_out, w_shape = _cast_specs(weights, DEC_BATCH * nrb, lambda b, r: b * nrb + r)

    def kb(b, r, j):
        return b * nrb + jnp.clip(r - 1, 0, nrb - NA_NKB) + j

    in_specs = [pl.BlockSpec((NA_KBLK, D), lambda b, r: (b * nrb + r, 0))]
    in_specs += [pl.BlockSpec((D, NA_KBLK), functools.partial(lambda b, r, j: (0, kb(b, r, j)), j=j))
                 for j in range(NA_NKB)]
    in_specs += [pl.BlockSpec((NA_KBLK, D), functools.partial(lambda b, r, j: (kb(b, r, j), 0), j=j))
                 for j in range(NA_NKB)]
    in_specs += [
        pl.BlockSpec((1, 1, D, PAST), lambda b, r: (b, i, 0, 0)),
        pl.BlockSpec((1, 1, D, PAST), lambda b, r: (b, i, 0, 0)),
        pl.BlockSpec((1, NH, 2 * NA_KH, GRID_W, LANES), lambda b, r: (i, 0, 0, 0, 0),
                     pipeline_mode=pl.Buffered(1)),
    ]
    return pl.pallas_call(
        _attn_na_sample_kernel,
        grid=(DEC_BATCH, nrb),
        in_specs=in_specs + w_in,
        out_specs=[pl.BlockSpec((NA_KBLK, D), lambda b, r: (b * nrb + r, 0))] + w_out,
        out_shape=[jax.ShapeDtypeStruct((NS, D), BF)] + w_shape,
        compiler_params=_params(("arbitrary", "arbitrary")),
        name="attn_na_sample",
    )(q, *([kt] * NA_NKB), *([v] * NA_NKB), cache_kt, cache_vt, ptab, *[w for w, _ in weights])


N_DC = 2 * NA_KW - 1
N_DR = 2 * NA_KH - 1
N_ENT = N_DR + 1
DC_PAD = 32
N_SPLIT = 3
TILE_ELEMS = GRID_W * LANES


def _na_bias_kernel(lhs_ref, sel_ref, ok_ref, o_ref):
    t = _dot(lhs_ref[0], sel_ref[...])
    ok = ok_ref[...] > 0.0
    for h in range(NH):
        rows = slice(h * N_ENT, (h + 1) * N_ENT)
        o_ref[0, rows, :] = jnp.where(ok, t[rows, :], NEG)


def _na_bias_tables(rpb):
    n_a = rpb.shape[0]
    qc = jnp.arange(GRID_W)[:, None]
    lane = jnp.arange(LANES)[None, :]
    kc = lane % GRID_W
    half = lane // GRID_W
    c_start = jnp.clip(qc - NA_KW // 2, 0, GRID_W - NA_KW)
    ok = (kc >= c_start) & (kc < c_start + NA_KW)
    slot = (half * DC_PAD + kc - qc + (NA_KW - 1)).reshape(-1)
    sel = (jnp.arange(2 * DC_PAD)[:, None] == slot[None, :]) & ok.reshape(1, -1)
    sel = jnp.tile(sel.astype(BF), (N_SPLIT, 1))
    d = jnp.arange(N_ENT)[:, None, None]
    ok_d = ok[None] & jnp.where(half[None] == 0, d >= 1, d <= N_DR - 1)
    ok_d = ok_d.reshape(N_ENT, TILE_ELEMS).astype(F32)
    pad = jnp.zeros((n_a, NH, 1, N_DC), F32)
    left = jnp.concatenate([pad, rpb], axis=2)
    right = jnp.concatenate([rpb, pad], axis=2)
    padc = lambda a: jnp.pad(a, ((0, 0), (0, 0), (0, 0), (0, DC_PAD - N_DC)))
    both = jnp.concatenate([padc(left), padc(right)], axis=-1).reshape(n_a, NH * N_ENT, 2 * DC_PAD)
    pieces = []
    rest = both
    for _ in range(N_SPLIT):
        piece = rest.astype(BF)
        pieces.append(piece)
        rest = rest - piece.astype(F32)
    lhs = jnp.concatenate(pieces, axis=-1)
    k = N_SPLIT * 2 * DC_PAD
    out = pl.pallas_call(
        _na_bias_kernel,
        grid=(n_a,),
        in_specs=[
            pl.BlockSpec((1, NH * N_ENT, k), lambda a: (a, 0, 0)),
            pl.BlockSpec((k, TILE_ELEMS), lambda a: (0, 0)),
            pl.BlockSpec((N_ENT, TILE_ELEMS), lambda a: (0, 0)),
        ],
        out_specs=pl.BlockSpec((1, NH * N_ENT, TILE_ELEMS), lambda a: (a, 0, 0)),
        out_shape=jax.ShapeDtypeStruct((n_a, NH * N_ENT, TILE_ELEMS), F32),
        compiler_params=_params(("arbitrary",)),
        name="na_bias",
    )(lhs, sel, ok_d)
    return out.reshape(n_a, NH, N_ENT, GRID_W, LANES)


HALO_O = 16


def _mix_ffn_kernel(*refs, halo, tiles_per_seq):
    if halo:
        (o_ref, op_ref, on_ref, x_ref, xp_ref, xn_ref, mod_ref, gmix_ref, wo_ref,
         gpre_ref, gpost_ref, wup_ref, cw_ref, cb_ref, wdn_ref, out_ref, act_scr) = refs
    else:
        (o_ref, x_ref, mod_ref, gmix_ref, wo_ref,
         gpre_ref, gpost_ref, wup_ref, cw_ref, cb_ref, wdn_ref, out_ref, act_scr) = refs
    tm = x_ref.shape[0]
    m = mod_ref[0]
    gate_mix = m[:, 2 * D:3 * D]
    sh = m[:, 3 * D:4 * D]
    sc = m[:, 4 * D:5 * D]
    gate = m[:, 5 * D:6 * D]
    zeros = jnp.zeros((SUBLANES, D), F32)

    if halo:
        o_ext = jnp.concatenate([op_ref[...].astype(F32)[HALO_O - SUBLANES:], o_ref[...].astype(F32),
                                 on_ref[...].astype(F32)[:SUBLANES]], axis=0).astype(BF)
        x_ext = jnp.concatenate([xp_ref[...], x_ref[...], xn_ref[...]], axis=0)
    else:
        o_ext = o_ref[...]
        x_ext = x_ref[...]
    mrows = x_ext.shape[0]
    cut = (mrows // 2 + 15) // 16 * 16
    spans = [(0, cut), (cut, mrows)]
    ys = [_dot(o_ext[a:b], wo_ref[...]) for a, b in spans]
    g_mix = gate_mix * gmix_ref[...]
    g_pre = gpre_ref[...] * (1.0 + sc)
    x_parts, h_parts = [], []
    for (a, b), yv in zip(spans, ys):
        xv = x_ext[a:b] + _rms(yv, g_mix)
        x_parts.append(xv)
        h_parts.append(_rms(xv, g_pre) + sh)
    x_ext = jnp.concatenate(x_parts, axis=0)
    h = jnp.concatenate(h_parts, axis=0)
    if halo:
        t = pl.program_id(0) % tiles_per_seq
        x = x_ext[SUBLANES:SUBLANES + tm]
        hp = jnp.where(t == 0, zeros, h[:SUBLANES])
        hn = jnp.where(t == tiles_per_seq - 1, zeros, h[SUBLANES + tm:])
        hext = jnp.concatenate([hp, h[SUBLANES:SUBLANES + tm], hn], axis=0).astype(BF)
        starts = [SUBLANES]
        span = tm
    else:
        x = x_ext
        parts = [zeros]
        for s in range(tm // SEQ):
            parts += [h[s * SEQ:(s + 1) * SEQ], zeros]
        hext = jnp.concatenate(parts, axis=0).astype(BF)
        starts = [SUBLANES + s * (SEQ + SUBLANES) for s in range(tm // SEQ)]
        span = SEQ
    mext = hext.shape[0]

    def token_rows(v):
        parts = [v[r0:r0 + span] for r0 in starts]
        return parts[0] if len(parts) == 1 else jnp.concatenate(parts, axis=0)

    def conv(u, c0):
        cols = slice(c0, c0 + FF_CHUNK)
        prev = token_rows(pltpu.roll(u, 1, 0))
        nxt = token_rows(pltpu.roll(u, mext - 1, 0))
        return (prev * cw_ref[0, 0:1, cols] + token_rows(u) * cw_ref[0, 1:2, cols]
                + nxt * cw_ref[0, 2:3, cols] + cb_ref[0, :, cols])

    for j in range(N_CHUNK):
        ca = j * FF_CHUNK
        cg = DFF + j * FF_CHUNK
        a = conv(_dot(hext, wup_ref[:, ca:ca + FF_CHUNK]), ca)
        gt = conv(_dot(hext, wup_ref[:, cg:cg + FF_CHUNK]), cg)
        ha = 0.5 * a
        act_scr[:, ca:ca + FF_CHUNK] = ((ha + ha * jnp.tanh(ha)) * gt).astype(BF)
    g_post = gate * gpost_ref[...]
    half = tm // 2
    ys = [_dot(act_scr[r:r + half, :], wdn_ref[...]) for r in (0, half)]
    for r, yv in zip((0, half), ys):
        out_ref[r:r + half, :] = x[r:r + half] + _rms(yv, g_post)


def _mix_ffn(o, x, modl, mod_map, gmix, w_o, gpre, gpost, wup, cw, cb, wdn, l, *, tm, halo):
    n = x.shape[0]
    tps = DEC_SEQ // tm

    def prev_blk(rows):
        r = tm // rows
        return lambda t: (jnp.maximum(t * r - 1, 0), 0)

    def next_blk(rows):
        r = tm // rows
        return lambda t: (jnp.minimum((t + 1) * r, n // rows - 1), 0)

    in_specs = [pl.BlockSpec((tm, D), lambda t: (t, 0))]
    args = [o]
    if halo:
        in_specs += [pl.BlockSpec((HALO_O, D), prev_blk(HALO_O)), pl.BlockSpec((HALO_O, D), next_blk(HALO_O))]
        args += [o, o]
    in_specs.append(pl.BlockSpec((tm, D), lambda t: (t, 0)))
    args.append(x)
    if halo:
        in_specs += [pl.BlockSpec((SUBLANES, D), prev_blk(SUBLANES)),
                     pl.BlockSpec((SUBLANES, D), next_blk(SUBLANES))]
        args += [x, x]
    const = dict(pipeline_mode=pl.Buffered(1))
    in_specs += [
        pl.BlockSpec((1, 1, 6 * D), mod_map),
        pl.BlockSpec((1, D), lambda t: (0, 0)),
        pl.BlockSpec((D, D), lambda t: (0, 0), **const),
        pl.BlockSpec((1, D), lambda t: (0, 0)),
        pl.BlockSpec((1, D), lambda t: (0, 0)),
        pl.BlockSpec((D, 2 * DFF), lambda t: (0, 0), **const),
        pl.BlockSpec((1, 3, 2 * DFF), lambda t: (l, 0, 0)),
        pl.BlockSpec((1, 1, 2 * DFF), lambda t: (l, 0, 0)),
        pl.BlockSpec((DFF, D), lambda t: (0, 0), **const),
    ]
    args += [modl, gmix.reshape(1, D), w_o, gpre.reshape(1, D), gpost.reshape(1, D), wup, cw, cb, wdn]
    return pl.pallas_call(
        functools.partial(_mix_ffn_kernel, halo=halo, tiles_per_seq=tps),
        grid=(n // tm,),
        in_specs=in_specs,
        out_specs=pl.BlockSpec((tm, D), lambda t: (t, 0)),
        out_shape=jax.ShapeDtypeStruct((n, D), F32),
        scratch_shapes=[pltpu.VMEM((tm, DFF), BF)],
        compiler_params=_params(("arbitrary",)),
        name="mix_ffn",
    )(*args)


def _rope_tables():
    f32 = np.float32
    t = np.arange(DEC_SEQ)
    rows = (t // GRID_W).astype(f32)
    cols = (t % GRID_W).astype(f32)
    quarter = HD // 4
    freqs = f32(ROPE_BASE) ** (-np.arange(quarter, dtype=f32) / f32(quarter))
    lane = np.arange(LANES)
    d = lane % HD
    e = d % (HD // 2)
    is_x2 = (e >= quarter)[None, :]
    pos = np.where((d // (HD // 2) == 0)[None, :], rows[:, None], cols[:, None])
    ang = (pos * freqs[e % quarter][None, :]).astype(f32)
    cos = np.cos(ang).astype(f32)
    sin = np.sin(ang).astype(f32)
    zero = f32(0.0)
    return (jnp.asarray(cos), jnp.asarray(np.where(is_x2, zero, -sin)),
            jnp.asarray(np.where(is_x2, sin, zero)))


def _block_diag_ones():
    r = jnp.arange(HEAD_BLOCK)
    return (r[:, None] // HD == r[None, :] // HD).astype(BF)


def _cache_t(cache):
    b, n, p, h, d = cache.shape
    return cache.transpose(0, 1, 3, 4, 2).reshape(b, n, h * d, p)


def _cache_out(kt, heads):
    b, n, _, s = kt.shape
    return kt.reshape(b, n, heads, HD, s).transpose(0, 1, 4, 2, 3)


def kernel(x_prompt, x_sample, cache_na_k, cache_na_v, cache_gqa_k, cache_gqa_v, c, c_ctx,
           ada_w, ada_b, norm_mix_pre, norm_mix_post, norm_ffn_pre, norm_ffn_post,
           na_w_qkv, na_w_o, na_rpb, gqa_w_qkv, gqa_w_o, gqa_q_norm, gqa_k_norm,
           ffn_w_up, ffn_conv_w, ffn_conv_b, ffn_w_down):
    n_a = cache_na_k.shape[1]
    n_b = cache_gqa_k.shape[1]
    cond8 = jnp.concatenate([c_ctx[None], c, jnp.zeros((SUBLANES - 1 - DEC_BATCH, D), F32)], axis=0)
    mod, w_qkv = _adaln(cond8, ada_w, ada_b, [(na_w_qkv, 0)])
    w_qkv = w_qkv[None]

    tm_p = tm_s = tm_fp = tm_fs = TOKEN_TILE
    map_p = lambda t: (0, 0, 0)
    map_s = lambda t: (1 + t // (DEC_SEQ // tm_s), 0, 0)
    map_fs = lambda t: (1 + t // (DEC_SEQ // tm_fs), 0, 0)

    cna_kt, cna_vt = _cache_t(cache_na_k), _cache_t(cache_na_v)
    cg_kt, cg_vt = _cache_t(cache_gqa_k), _cache_t(cache_gqa_v)
    rope_tabs = _rope_tables()
    bd = _block_diag_ones()
    ptabs = _na_bias_tables(na_rpb)
    kvg = NKV * HD

    cb = ffn_conv_b.reshape(DEPTH, 1, 2 * DFF)

    xp = x_prompt.reshape(NP, D)
    xs = x_sample.reshape(NS, D)
    na_kv = None
    gq_kv = None
    for l in range(DEPTH):
        i = l // 2
        modl = mod[l].reshape(SUBLANES, 1, 6 * D)
        casts = [(na_w_o if l % 2 == 0 else gqa_w_o, i), (ffn_w_down, l)]
        if l + 1 < DEPTH:
            casts.append((gqa_w_qkv if (l + 1) % 2 else na_w_qkv, (l + 1) // 2))
        if l % 2 == 0:
            qp, vp, kt, vt = _qkv_prompt(xp, modl, norm_mix_pre[l], w_qkv, i, n_a, na_kv,
                                         gqa=False, tm=tm_p)
            na_kv = (kt, vt)
            qs, kts, vs = _qkv_sample_na(xs, modl, map_s, norm_mix_pre[l], w_qkv, tm=tm_s)
            op, wup = _attn_prompt(qp, kt, vp, i, 1, [(ffn_w_up, l)])
            os_, w_o, wdn, *w_nxt = _attn_na_sample(qs, kts, vs, cna_kt, cna_vt, i, ptabs, casts)
        else:
            qg = (jnp.tile(gqa_q_norm[i], NH) * Q_SCALE).reshape(1, D)
            kg = jnp.tile(gqa_k_norm[i], NKV)
            qkg = jnp.concatenate([qg[0], kg]).reshape(1, D + kvg)
            kg2 = jnp.broadcast_to(kg[:, None], (kvg, SEQ))
            qp, vp, kt, vt = _qkv_prompt(xp, modl, norm_mix_pre[l], w_qkv, i, n_b, gq_kv,
                                         gqa=True, tm=tm_p, extra=(bd, qg, kg2))
            gq_kv = (kt, vt)
            qs, kts, vs = _qkv_sample_gqa(xs, modl, map_s, norm_mix_pre[l], w_qkv, bd, qkg,
                                          rope_tabs, tm=tm_s)
            op, wup = _attn_prompt(qp, kt, vp, i, NH // NKV, [(ffn_w_up, l)])
            os_, w_o, wdn, *w_nxt = _attn_gqa_sample(qs, kts, vs, cg_kt, cg_vt, i, casts)
        if w_nxt:
            w_qkv = w_nxt[0][None]
        xp = _mix_ffn(op, xp, modl, map_p, norm_mix_post[l], w_o, norm_ffn_pre[l], norm_ffn_post[l],
                      wup, ffn_conv_w, cb, wdn, l, tm=tm_fp, halo=False)
        xs = _mix_ffn(os_, xs, modl, map_fs, norm_mix_post[l], w_o, norm_ffn_pre[l], norm_ffn_post[l],
                      wup, ffn_conv_w, cb, wdn, l, tm=tm_fs, halo=True)

    return (xp.reshape(BATCH, SEQ, D), xs.reshape(DEC_BATCH, DEC_SEQ, D),
            _cache_out(na_kv[0], NH), _cache_out(na_kv[1], NH),
            _cache_out(gq_kv[0], NKV), _cache_out(gq_kv[1], NKV))
```

```python
import functools

import jax
import jax.numpy as jnp
import numpy as np
from jax import lax
from jax.experimental import pallas as pl
from jax.experimental.pallas import tpu as pltpu

D = 1024
HD = 64
NH = 16
NKV = 4
DFF = 2816
DEPTH = 4
GRID_W = 64
ROWS = 32
NA_KH = 8
NA_KW = 16
ROPE_BASE = 10000.0
EPS = 1e-6
BATCH, SEQ = 32, 256
DEC_BATCH, DEC_SEQ = 2, 2048
PAST = 256
NP = BATCH * SEQ
NS = DEC_BATCH * DEC_SEQ

BF = jnp.bfloat16
F32 = jnp.float32
NEG = -1e30
LANES = 128
SUBLANES = 8
Q_SCALE = HD ** -0.5
MXU_TILE = 256
HEAD_BLOCK = MXU_TILE
FF_CHUNK = MXU_TILE
N_CHUNK = DFF // FF_CHUNK
TOKEN_TILE = 512
V7X_VMEM_BYTES = 64 * 1024 * 1024
VMEM_LIMIT = V7X_VMEM_BYTES - 14 * 1024 * 1024


def _params(sem):
    return pltpu.CompilerParams(dimension_semantics=sem, vmem_limit_bytes=VMEM_LIMIT)


def _rms(x, g):
    return x * lax.rsqrt(jnp.mean(x * x, axis=-1, keepdims=True) + EPS) * g


def _dot(a, b):
    return jnp.dot(a, b, preferred_element_type=F32)


def _dot_tn_nt(a, b):
    return lax.dot_general(a, b, (((0,), (1,)), ((), ())), preferred_element_type=F32)


def _qkv_weight_specs(gqa):
    if gqa:
        kvw = NKV * HD
        return [pl.BlockSpec((1, D, D), lambda t: (0, 0, 0)),
                pl.BlockSpec((1, D, kvw), lambda t: (0, 0, D // kvw)),
                pl.BlockSpec((1, D, kvw), lambda t: (0, 0, D // kvw + 1))]
    return [pl.BlockSpec((1, D, D), functools.partial(lambda t, c: (0, 0, c), c=c)) for c in range(3)]


def _mod_kernel(cond_ref, w_ref, b_ref, *rest):
    n_w = (len(rest) - 1) // 2
    o_ref = rest[n_w]
    _cast_slabs(rest[:n_w], rest[n_w + 1:])
    s = cond_ref[...]
    s = s * jax.nn.sigmoid(s)
    o_ref[0] = _dot(s.astype(BF), w_ref[0].astype(BF)) + b_ref[0]


def _adaln(cond8, ada_w, ada_b, weights=()):
    tn = 1536
    nn = 6 * D // tn
    w_in, w_out, w_shape = _cast_specs(weights, DEPTH * nn, lambda l, n: l * nn + n)
    return pl.pallas_call(
        _mod_kernel,
        grid=(DEPTH, nn),
        in_specs=[
            pl.BlockSpec((SUBLANES, D), lambda l, n: (0, 0)),
            pl.BlockSpec((1, D, tn), lambda l, n: (l, 0, n)),
            pl.BlockSpec((1, 1, tn), lambda l, n: (l, 0, n)),
        ] + w_in,
        out_specs=[pl.BlockSpec((1, SUBLANES, tn), lambda l, n: (l, 0, n))] + w_out,
        out_shape=[jax.ShapeDtypeStruct((DEPTH, SUBLANES, 6 * D), F32)] + w_shape,
        compiler_params=_params(("arbitrary", "arbitrary")),
        name="adaln",
    )(cond8, ada_w, ada_b.reshape(DEPTH, 1, 6 * D), *[w for w, _ in weights])


def _prenorm(x_ref, mod_ref, g_ref):
    m = mod_ref[0]
    return (_rms(x_ref[...], g_ref[...] * (1.0 + m[:, D:2 * D])) + m[:, 0:D]).astype(BF)


def _head_rms_lanes(blk, bd_ref):
    ss = _dot((blk * blk).astype(BF), bd_ref[...])
    return blk * lax.rsqrt(ss * (1.0 / HD) + EPS)


def _qkv_prompt_kernel(*refs, gqa, slot):
    if gqa:
        (x_ref, mod_ref, g_ref, wq_ref, wk_ref, wv_ref, bd_ref, qg_ref, kg_ref,
         q_ref, v_ref, kt_ref, vt_ref) = refs
    else:
        x_ref, mod_ref, g_ref, wq_ref, wk_ref, wv_ref, q_ref, v_ref, kt_ref, vt_ref = refs
    hb = _prenorm(x_ref, mod_ref, g_ref)
    q = _dot(hb, wq_ref[0])
    v = _dot(hb, wv_ref[0])
    kt = _dot_tn_nt(wk_ref[0], hb)
    if gqa:
        for t in range(D // HEAD_BLOCK):
            sl = slice(t * HEAD_BLOCK, (t + 1) * HEAD_BLOCK)
            q_ref[:, sl] = (_head_rms_lanes(q[:, sl], bd_ref) * qg_ref[:, sl]).astype(q_ref.dtype)
        heads = []
        for h in range(NKV):
            blk = kt[h * HD:(h + 1) * HD, :]
            heads.append(blk * lax.rsqrt(jnp.mean(blk * blk, axis=0, keepdims=True) + EPS))
        kt = jnp.concatenate(heads, axis=0)
    else:
        q_ref[...] = (q * Q_SCALE).astype(q_ref.dtype)
    v_ref[...] = v.astype(v_ref.dtype)
    for s in range(x_ref.shape[0] // SEQ):
        sl = slice(s * SEQ, (s + 1) * SEQ)
        kts = kt[:, sl]
        if gqa:
            kts = kts * kg_ref[...]
        kt_ref[s, slot] = kts
        vt_ref[s, slot] = v[sl, :].T
        for other in range(kt_ref.shape[1]):
            if other != slot:
                kt_ref[s, other] = jnp.zeros_like(kts)
                vt_ref[s, other] = jnp.zeros_like(kts)


def _qkv_prompt(x, modl, g, w, i, n_layers, prev, *, gqa, tm, extra=()):
    kvw = NKV * HD if gqa else D
    in_specs = [
        pl.BlockSpec((tm, D), lambda t: (t, 0)),
        pl.BlockSpec((1, 1, 6 * D), lambda t: (0, 0, 0)),
        pl.BlockSpec((1, D), lambda t: (0, 0)),
    ] + _qkv_weight_specs(gqa)
    args = [x, modl, g.reshape(1, D), w, w, w]
    if gqa:
        in_specs += [pl.BlockSpec((HEAD_BLOCK, HEAD_BLOCK), lambda t: (0, 0)),
                     pl.BlockSpec((1, D), lambda t: (0, 0)),
                     pl.BlockSpec((kvw, SEQ), lambda t: (0, 0))]
        args += list(extra)
    n_in = len(args)
    aliases = {}
    if prev is None:
        kv_spec = pl.BlockSpec((tm // SEQ, n_layers, kvw, SEQ), lambda t: (t, 0, 0, 0))
        slot = i
    else:
        in_specs += [pl.BlockSpec(memory_space=pl.ANY), pl.BlockSpec(memory_space=pl.ANY)]
        args += list(prev)
        aliases = {n_in: 2, n_in + 1: 3}
        kv_spec = pl.BlockSpec((tm // SEQ, 1, kvw, SEQ), lambda t: (t, i, 0, 0))
        slot = 0
    kv_shape = jax.ShapeDtypeStruct((BATCH, n_layers, kvw, SEQ), F32)

    def body(*refs):
        _qkv_prompt_kernel(*(refs[:n_in] + refs[len(args):]), gqa=gqa, slot=slot)

    return pl.pallas_call(
        body,
        grid=(NP // tm,),
        in_specs=in_specs,
        out_specs=[
            pl.BlockSpec((tm, D), lambda t: (t, 0)),
            pl.BlockSpec((tm, kvw), lambda t: (t, 0)),
            kv_spec, kv_spec,
        ],
        out_shape=[
            jax.ShapeDtypeStruct((NP, D), BF),
            jax.ShapeDtypeStruct((NP, kvw), BF),
            kv_shape, kv_shape,
        ],
        input_output_aliases=aliases,
        compiler_params=_params(("arbitrary",)),
        name="qkv_prompt_gqa" if gqa else "qkv_prompt_na",
    )(*args)


def _qkv_sample_na_kernel(x_ref, mod_ref, g_ref, wq_ref, wk_ref, wv_ref, q_ref, kt_ref, v_ref):
    hb = _prenorm(x_ref, mod_ref, g_ref)
    q_ref[...] = (_dot(hb, wq_ref[0]) * Q_SCALE).astype(q_ref.dtype)
    v_ref[...] = _dot(hb, wv_ref[0]).astype(v_ref.dtype)
    kt_ref[...] = _dot_tn_nt(wk_ref[0], hb).astype(kt_ref.dtype)


def _qkv_sample_na(x, modl, mod_map, g, w, *, tm):
    return pl.pallas_call(
        _qkv_sample_na_kernel,
        grid=(NS // tm,),
        in_specs=[
            pl.BlockSpec((tm, D), lambda t: (t, 0)),
            pl.BlockSpec((1, 1, 6 * D), mod_map),
            pl.BlockSpec((1, D), lambda t: (0, 0)),
        ] + _qkv_weight_specs(False),
        out_specs=[
            pl.BlockSpec((tm, D), lambda t: (t, 0)),
            pl.BlockSpec((D, tm), lambda t: (0, t)),
            pl.BlockSpec((tm, D), lambda t: (t, 0)),
        ],
        out_shape=[
            jax.ShapeDtypeStruct((NS, D), BF),
            jax.ShapeDtypeStruct((D, NS), BF),
            jax.ShapeDtypeStruct((NS, D), BF),
        ],
        compiler_params=_params(("arbitrary",)),
        name="qkv_sample_na",
    )(x, modl, g.reshape(1, D), w, w, w)


def _qkv_sample_gqa_kernel(x_ref, mod_ref, g_ref, w_ref, bd_ref, qkg_ref, cos_ref, s1_ref, s2_ref,
                           q_ref, kt_ref, v_ref):
    qkv = _dot(_prenorm(x_ref, mod_ref, g_ref), w_ref[0])
    nqk = D + NKV * HD
    for t in range(nqk // LANES):
        if t % 2 == 0:
            nrm = _head_rms_lanes(qkv[:, (t // 2) * HEAD_BLOCK:(t // 2 + 1) * HEAD_BLOCK], bd_ref)
        xt = nrm[:, (t % 2) * LANES:(t % 2 + 1) * LANES] * qkg_ref[:, t * LANES:(t + 1) * LANES]
        xt = (xt * cos_ref[...]
              + pltpu.roll(xt, LANES - 16, 1) * s1_ref[...]
              + pltpu.roll(xt, 16, 1) * s2_ref[...])
        if t < D // LANES:
            q_ref[:, t * LANES:(t + 1) * LANES] = xt.astype(q_ref.dtype)
        else:
            tk = t - D // LANES
            kt_ref[tk * LANES:(tk + 1) * LANES, :] = xt.T.astype(kt_ref.dtype)
    v_ref[...] = qkv[:, nqk:].astype(v_ref.dtype)


def _qkv_sample_gqa(x, modl, mod_map, g, w, bd, qkg, rope_tabs, *, tm):
    nq = w.shape[2]
    kvw = NKV * HD
    tps = DEC_SEQ // tm
    in_specs = [
        pl.BlockSpec((tm, D), lambda t: (t, 0)),
        pl.BlockSpec((1, 1, 6 * D), mod_map),
        pl.BlockSpec((1, D), lambda t: (0, 0)),
        pl.BlockSpec((1, D, nq), lambda t: (0, 0, 0)),
        pl.BlockSpec((HEAD_BLOCK, HEAD_BLOCK), lambda t: (0, 0)),
        pl.BlockSpec((1, D + kvw), lambda t: (0, 0)),
    ]
    in_specs += [pl.BlockSpec((tm, LANES), lambda t: (t % tps, 0)) for _ in rope_tabs]
    return pl.pallas_call(
        _qkv_sample_gqa_kernel,
        grid=(NS // tm,),
        in_specs=in_specs,
        out_specs=[
            pl.BlockSpec((tm, D), lambda t: (t, 0)),
            pl.BlockSpec((kvw, tm), lambda t: (0, t)),
            pl.BlockSpec((tm, kvw), lambda t: (t, 0)),
        ],
        out_shape=[
            jax.ShapeDtypeStruct((NS, D), BF),
            jax.ShapeDtypeStruct((kvw, NS), BF),
            jax.ShapeDtypeStruct((NS, kvw), BF),
        ],
        compiler_params=_params(("arbitrary",)),
        name="qkv_sample_gqa",
    )(x, modl, g.reshape(1, D), w, bd, qkg, *rope_tabs)


def _lane_lo(rows):
    return lax.broadcasted_iota(jnp.int32, (rows, LANES), 1) < HD


def _pipelined(n, first, second):
    nxt = first(0)
    for i in range(n):
        cur = nxt
        if i + 1 < n:
            nxt = first(i + 1)
        second(i, cur)


def _swap_halves(a, axis):
    h = a.shape[axis] // 2
    lo, hi = (a[:h], a[h:]) if axis == 0 else (a[:, :h], a[:, h:])
    return jnp.concatenate([hi, lo], axis=axis)


def _attn_grouped(problems, group, stack, *, ones_sum):
    tq = problems[0][0].shape[0]
    lo = _lane_lo(tq)
    pending = {}
    per = NH // stack

    def parts(u):
        g = u * stack // group
        heads = range(u * stack, (u + 1) * stack)
        same = [h for h in heads if h % 2 == g % 2]
        other = [h for h in heads if h % 2 != g % 2]
        return g, [(p, swapped) for p, swapped in ((same, False), (other, True)) if p]

    def scores(n):
        q_ref, _, key_tiles, _ = problems[n // per]
        g, plist = parts(n % per)
        out = []
        for heads, swapped in plist:
            qs = [q_ref[:, (h // 2) * LANES:(h // 2 + 1) * LANES] for h in heads]
            qs = qs[0] if len(qs) == 1 else jnp.concatenate(qs, axis=0)
            mine = _lane_lo(qs.shape[0])
            if heads[0] % 2:
                mine = jnp.logical_not(mine)
            qsel = jnp.where(mine, qs, jnp.zeros_like(qs))
            kts = key_tiles(g)
            if swapped:
                kts = [_swap_halves(kt, 0) for kt in kts]
            out.append([_dot(qsel, kt) for kt in kts])
        return out

    def finish(n, sss):
        _, o_ref, _, value_tiles = problems[n // per]
        g, plist = parts(n % per)
        for (heads, swapped), ss in zip(plist, sss):
            m = ss[0].max(axis=-1, keepdims=True)
            for s in ss[1:]:
                m = jnp.maximum(m, s.max(axis=-1, keepdims=True))
            es = [jnp.exp(s - m) for s in ss]
            pv = None
            for e, v in zip(es, value_tiles(g)):
                if swapped:
                    v = _swap_halves(v, 1)
                if ones_sum:
                    mine = _lane_lo(v.shape[0])
                    if heads[0] % 2:
                        mine = jnp.logical_not(mine)
                    v = jnp.where(mine, v, jnp.ones_like(v))
                d = _dot(e.astype(BF), v)
                pv = d if pv is None else pv + d
            if ones_sum:
                pv = pv / pltpu.roll(pv, HD, 1)
            else:
                l = es[0].sum(axis=-1, keepdims=True)
                for e in es[1:]:
                    l = l + e.sum(axis=-1, keepdims=True)
                pv = pv / l
            for j, h in enumerate(heads):
                pending[h] = pv[j * tq:(j + 1) * tq]
        for h in sorted({h - h % 2 for h in range((n % per) * stack, (n % per + 1) * stack)}):
            if h in pending and h + 1 in pending:
                sl = slice((h // 2) * LANES, (h // 2 + 1) * LANES)
                o_ref[:, sl] = jnp.where(lo, pending.pop(h), pending.pop(h + 1)).astype(o_ref.dtype)

    _pipelined(per * len(problems), scores, finish)


ATTN_P_SEQS = 4


def _attn_prompt_kernel(q_ref, kt_ref, v_ref, *rest, group):
    n_w = (len(rest) - 1) // 2
    o_ref = rest[n_w]
    _cast_slabs(rest[:n_w], rest[n_w + 1:])

    def tile(g):
        return slice((g // 2) * LANES, (g // 2 + 1) * LANES)

    def problem(s):
        rows = pl.ds(s * SEQ, SEQ)
        return (q_ref.at[rows, :], o_ref.at[rows, :],
                lambda g: [kt_ref[s, 0, tile(g), :].astype(BF)],
                lambda g: [v_ref[pl.ds(s * SEQ, SEQ), tile(g)]])
    _attn_grouped([problem(s) for s in range(ATTN_P_SEQS)], group, group, ones_sum=False)


def _attn_prompt(q, kt_all, v, i, group, weights):
    kvw = v.shape[1]
    rows = ATTN_P_SEQS * SEQ
    steps = BATCH // ATTN_P_SEQS
    w_in, w_out, w_shape = _cast_specs(weights, steps, lambda b: b)
    return pl.pallas_call(
        functools.partial(_attn_prompt_kernel, group=group),
        grid=(steps,),
        in_specs=[
            pl.BlockSpec((rows, D), lambda b: (b, 0)),
            pl.BlockSpec((ATTN_P_SEQS, 1, kvw, SEQ), lambda b: (b, i, 0, 0)),
            pl.BlockSpec((rows, kvw), lambda b: (b, 0)),
        ] + w_in,
        out_specs=[pl.BlockSpec((rows, D), lambda b: (b, 0))] + w_out,
        out_shape=[jax.ShapeDtypeStruct((NP, D), BF)] + w_shape,
        compiler_params=_params(("arbitrary",)),
        name="attn_prompt",
    )(q, kt_all, v, *[w for w, _ in weights])


def _cast_specs(weights, steps, slab):
    w_in, w_out, w_shape = [], [], []
    for w, layer in weights:
        _, r, c = w.shape
        w_in.append(pl.BlockSpec((1, r // steps, c),
                                 functools.partial(lambda *ids, layer: (layer, slab(*ids), 0), layer=layer)))
        w_out.append(pl.BlockSpec((r // steps, c), lambda *ids: (slab(*ids), 0)))
        w_shape.append(jax.ShapeDtypeStruct((r, c), BF))
    return w_in, w_out, w_shape


def _cast_slabs(w_refs, wb_refs):
    for w_ref, wb_ref in zip(w_refs, wb_refs):
        wb_ref[...] = w_ref[0].astype(BF)


def _attn_gqa_sample_kernel(q_ref, kt_ref, v_ref, kct_ref, vct_ref, *rest):
    n_w = (len(rest) - 1) // 2
    o_ref = rest[n_w]
    _cast_slabs(rest[:n_w], rest[n_w + 1:])
    vc = vct_ref[0, 0].T.astype(BF)

    def tile(g):
        return slice((g // 2) * LANES, (g // 2 + 1) * LANES)
    _attn_grouped([(q_ref, o_ref,
                    lambda g: [kt_ref[tile(g), :], kct_ref[0, 0, tile(g), :].astype(BF)],
                    lambda g: [v_ref[:, tile(g)], vc[:, tile(g)]])],
                  NH // NKV, NH // NKV, ones_sum=True)


GQA_TQ = 256


def _attn_gqa_sample(q, kt, v, cache_kt, cache_vt, i, weights=()):
    nq = DEC_SEQ // GQA_TQ
    kvw = NKV * HD
    w_in, w_out, w_shape = _cast_specs(weights, DEC_BATCH * nq, lambda b, t: b * nq + t)
    return pl.pallas_call(
        _attn_gqa_sample_kernel,
        grid=(DEC_BATCH, nq),
        in_specs=[
            pl.BlockSpec((GQA_TQ, D), lambda b, t: (b * nq + t, 0)),
            pl.BlockSpec((kvw, DEC_SEQ), lambda b, t: (0, b)),
            pl.BlockSpec((DEC_SEQ, kvw), lambda b, t: (b, 0)),
            pl.BlockSpec((1, 1, kvw, PAST), lambda b, t: (b, i, 0, 0)),
            pl.BlockSpec((1, 1, kvw, PAST), lambda b, t: (b, i, 0, 0)),
        ] + w_in,
        out_specs=[pl.BlockSpec((GQA_TQ, D), lambda b, t: (b * nq + t, 0))] + w_out,
        out_shape=[jax.ShapeDtypeStruct((NS, D), BF)] + w_shape,
        compiler_params=_params(("arbitrary", "arbitrary")),
        name="attn_gqa_sample",
    )(q, kt, v, cache_kt, cache_vt, *[w for w, _ in weights])


NA_QROWS = 4
NA_KBLK = NA_QROWS * GRID_W
NA_NKB = 3
NA_PAIRS = NA_QROWS // 2


def _attn_na_sample_kernel(*refs):
    q_ref = refs[0]
    kt_refs = refs[1:1 + NA_NKB]
    v_refs = refs[1 + NA_NKB:1 + 2 * NA_NKB]
    kct_ref, vct_ref, p_ref = refs[1 + 2 * NA_NKB:4 + 2 * NA_NKB]
    rest = refs[4 + 2 * NA_NKB:]
    n_w = (len(rest) - 1) // 2
    o_ref = rest[n_w]
    _cast_slabs(rest[:n_w], rest[n_w + 1:])
    rb = pl.program_id(1)
    kb0 = jnp.clip(rb - 1, 0, ROWS // NA_QROWS - NA_NKB)
    tq = q_ref.shape[0]
    lo = _lane_lo(tq)
    hi = jnp.logical_not(lo)
    lo_w = _lane_lo(GRID_W)
    vc = vct_ref[0, 0].T.astype(BF)
    ones = jnp.ones((tq, LANES), BF)

    tiles = []
    for i in range(NA_QROWS):
        qr = rb * NA_QROWS + i
        rs = jnp.clip(qr - NA_KH // 2, 0, ROWS - NA_KH)
        for j in range(NA_NKB):
            for jj in range(NA_PAIRS):
                kra = (kb0 + j) * NA_QROWS + 2 * jj
                krb = kra + 1
                idx = jnp.clip(kra - qr + NA_KH, 0, 2 * NA_KH - 1)
                va = jnp.logical_and(kra >= rs, kra < rs + NA_KH).astype(jnp.int32)
                vb = jnp.logical_and(krb >= rs, krb < rs + NA_KH).astype(jnp.int32)
                tiles.append((idx, jnp.where(lo_w, va, vb) > 0))

    def scores(h):
        rows = slice((h // 2) * LANES, (h // 2 + 1) * LANES)
        q2 = q_ref[:, rows]
        qsel = jnp.where(lo if h % 2 == 0 else hi, q2, jnp.zeros_like(q2))
        s_lat = [_dot(qsel, r[rows, :]) for r in kt_refs]
        out_rows = []
        for i in range(NA_QROWS):
            rsl = slice(i * GRID_W, (i + 1) * GRID_W)
            row = []
            for j in range(NA_NKB):
                for jj in range(NA_PAIRS):
                    idx, valid = tiles[(i * NA_NKB + j) * NA_PAIRS + jj]
                    t = s_lat[j][rsl, jj * LANES:(jj + 1) * LANES] + p_ref[0, h, idx]
                    row.append(jnp.where(valid, t, NEG))
            out_rows.append(jnp.concatenate(row, axis=1))
        return jnp.concatenate(out_rows, axis=0), _dot(qsel, kct_ref[0, 0, rows, :].astype(BF))

    def weighted_values(h, s, s_ctx):
        rows = slice((h // 2) * LANES, (h // 2 + 1) * LANES)
        mine = lo if h % 2 == 0 else hi
        m = jnp.maximum(s.max(axis=-1, keepdims=True), s_ctx.max(axis=-1, keepdims=True))
        e = jnp.exp(s - m).astype(BF)
        e_ctx = jnp.exp(s_ctx - m).astype(BF)
        pv = _dot(e_ctx, jnp.where(mine, vc[:, rows], ones))
        for j in range(NA_NKB):
            vj = jnp.where(mine, v_refs[j][:, rows], ones)
            pv = pv + _dot(e[:, j * NA_KBLK:(j + 1) * NA_KBLK], vj)
        return pv / pltpu.roll(pv, HD, 1)

    nxt = scores(0)
    o2 = None
    for h in range(NH):
        cur = nxt
        if h + 1 < NH:
            nxt = scores(h + 1)
        pv = weighted_values(h, *cur)
        if h % 2 == 0:
            o2 = pv
        else:
            o_ref[:, (h // 2) * LANES:(h // 2 + 1) * LANES] = jnp.where(lo, o2, pv).astype(o_ref.dtype)


def _attn_na_sample(q, kt, v, cache_kt, cache_vt, i, ptab, weights=()):
    nrb = ROWS // NA_QROWS
    w_in, w_out, w_shape = _cast_specs(weights, DEC_BATCH * nrb, lambda b, r: b * nrb + r)

    def kb(b, r, j):
        return b * nrb + jnp.clip(r - 1, 0, nrb - NA_NKB) + j

    in_specs = [pl.BlockSpec((NA_KBLK, D), lambda b, r: (b * nrb + r, 0))]
    in_specs += [pl.BlockSpec((D, NA_KBLK), functools.partial(lambda b, r, j: (0, kb(b, r, j)), j=j))
                 for j in range(NA_NKB)]
    in_specs += [pl.BlockSpec((NA_KBLK, D), functools.partial(lambda b, r, j: (kb(b, r, j), 0), j=j))
                 for j in range(NA_NKB)]
    in_specs += [
        pl.BlockSpec((1, 1, D, PAST), lambda b, r: (b, i, 0, 0)),
        pl.BlockSpec((1, 1, D, PAST), lambda b, r: (b, i, 0, 0)),
        pl.BlockSpec((1, NH, 2 * NA_KH, GRID_W, LANES), lambda b, r: (i, 0, 0, 0, 0),
                     pipeline_mode=pl.Buffered(1)),
    ]
    return pl.pallas_call(
        _attn_na_sample_kernel,
        grid=(DEC_BATCH, nrb),
        in_specs=in_specs + w_in,
        out_specs=[pl.BlockSpec((NA_KBLK, D), lambda b, r: (b * nrb + r, 0))] + w_out,
        out_shape=[jax.ShapeDtypeStruct((NS, D), BF)] + w_shape,
        compiler_params=_params(("arbitrary", "arbitrary")),
        name="attn_na_sample",
    )(q, *([kt] * NA_NKB), *([v] * NA_NKB), cache_kt, cache_vt, ptab, *[w for w, _ in weights])


N_DC = 2 * NA_KW - 1
N_DR = 2 * NA_KH - 1
N_ENT = N_DR + 1
DC_PAD = 32
N_SPLIT = 3
TILE_ELEMS = GRID_W * LANES


def _na_bias_kernel(lhs_ref, sel_ref, ok_ref, o_ref):
    t = _dot(lhs_ref[0], sel_ref[...])
    ok = ok_ref[...] > 0.0
    for h in range(NH):
        rows = slice(h * N_ENT, (h + 1) * N_ENT)
        o_ref[0, rows, :] = jnp.where(ok, t[rows, :], NEG)


def _na_bias_tables(rpb):
    n_a = rpb.shape[0]
    qc = jnp.arange(GRID_W)[:, None]
    lane = jnp.arange(LANES)[None, :]
    kc = lane % GRID_W
    half = lane // GRID_W
    c_start = jnp.clip(qc - NA_KW // 2, 0, GRID_W - NA_KW)
    ok = (kc >= c_start) & (kc < c_start + NA_KW)
    slot = (half * DC_PAD + kc - qc + (NA_KW - 1)).reshape(-1)
    sel = (jnp.arange(2 * DC_PAD)[:, None] == slot[None, :]) & ok.reshape(1, -1)
    sel = jnp.tile(sel.astype(BF), (N_SPLIT, 1))
    d = jnp.arange(N_ENT)[:, None, None]
    ok_d = ok[None] & jnp.where(half[None] == 0, d >= 1, d <= N_DR - 1)
    ok_d = ok_d.reshape(N_ENT, TILE_ELEMS).astype(F32)
    pad = jnp.zeros((n_a, NH, 1, N_DC), F32)
    left = jnp.concatenate([pad, rpb], axis=2)
    right = jnp.concatenate([rpb, pad], axis=2)
    padc = lambda a: jnp.pad(a, ((0, 0), (0, 0), (0, 0), (0, DC_PAD - N_DC)))
    both = jnp.concatenate([padc(left), padc(right)], axis=-1).reshape(n_a, NH * N_ENT, 2 * DC_PAD)
    pieces = []
    rest = both
    for _ in range(N_SPLIT):
        piece = rest.astype(BF)
        pieces.append(piece)
        rest = rest - piece.astype(F32)
    lhs = jnp.concatenate(pieces, axis=-1)
    k = N_SPLIT * 2 * DC_PAD
    out = pl.pallas_call(
        _na_bias_kernel,
        grid=(n_a,),
        in_specs=[
            pl.BlockSpec((1, NH * N_ENT, k), lambda a: (a, 0, 0)),
            pl.BlockSpec((k, TILE_ELEMS), lambda a: (0, 0)),
            pl.BlockSpec((N_ENT, TILE_ELEMS), lambda a: (0, 0)),
        ],
        out_specs=pl.BlockSpec((1, NH * N_ENT, TILE_ELEMS), lambda a: (a, 0, 0)),
        out_shape=jax.ShapeDtypeStruct((n_a, NH * N_ENT, TILE_ELEMS), F32),
        compiler_params=_params(("arbitrary",)),
        name="na_bias",
    )(lhs, sel, ok_d)
    return out.reshape(n_a, NH, N_ENT, GRID_W, LANES)


HALO_O = 16


def _mix_ffn_kernel(*refs, halo, tiles_per_seq):
    refs, (wo_ref, wup_ref, wdn_ref, w_sem) = refs[:-4], refs[-4:]
    if halo:
        (o_ref, op_ref, on_ref, x_ref, xp_ref, xn_ref, mod_ref, gmix_ref, wo_hbm,
         gpre_ref, gpost_ref, wup_hbm, cw_ref, cb_ref, wdn_hbm, out_ref, act_scr) = refs
    else:
        (o_ref, x_ref, mod_ref, gmix_ref, wo_hbm,
         gpre_ref, gpost_ref, wup_hbm, cw_ref, cb_ref, wdn_hbm, out_ref, act_scr) = refs
    w_copies = [pltpu.make_async_copy(src, dst, w_sem.at[k])
                for k, (src, dst) in enumerate(((wo_hbm, wo_ref), (wup_hbm, wup_ref), (wdn_hbm, wdn_ref)))]
    first_step = pl.program_id(0) == 0

    @pl.when(first_step)
    def _():
        for c in w_copies:
            c.start()
        w_copies[0].wait()

    tm = x_ref.shape[0]
    m = mod_ref[0]
    gate_mix = m[:, 2 * D:3 * D]
    sh = m[:, 3 * D:4 * D]
    sc = m[:, 4 * D:5 * D]
    gate = m[:, 5 * D:6 * D]
    zeros = jnp.zeros((SUBLANES, D), F32)

    if halo:
        o_ext = jnp.concatenate([op_ref[...].astype(F32)[HALO_O - SUBLANES:], o_ref[...].astype(F32),
                                 on_ref[...].astype(F32)[:SUBLANES]], axis=0).astype(BF)
        x_ext = jnp.concatenate([xp_ref[...], x_ref[...], xn_ref[...]], axis=0)
    else:
        o_ext = o_ref[...]
        x_ext = x_ref[...]
    mrows = x_ext.shape[0]
    cut = (mrows // 2 + 15) // 16 * 16
    spans = [(0, cut), (cut, mrows)]
    ys = [_dot(o_ext[a:b], wo_ref[...]) for a, b in spans]
    g_mix = gate_mix * gmix_ref[...]
    g_pre = gpre_ref[...] * (1.0 + sc)
    x_parts, h_parts = [], []
    for (a, b), yv in zip(spans, ys):
        xv = x_ext[a:b] + _rms(yv, g_mix)
        x_parts.append(xv)
        h_parts.append(_rms(xv, g_pre) + sh)
    x_ext = jnp.concatenate(x_parts, axis=0)
    h = jnp.concatenate(h_parts, axis=0)
    if halo:
        t = pl.program_id(0) % tiles_per_seq
        x = x_ext[SUBLANES:SUBLANES + tm]
        hp = jnp.where(t == 0, zeros, h[:SUBLANES])
        hn = jnp.where(t == tiles_per_seq - 1, zeros, h[SUBLANES + tm:])
        hext = jnp.concatenate([hp, h[SUBLANES:SUBLANES + tm], hn], axis=0).astype(BF)
        starts = [SUBLANES]
        span = tm
    else:
        x = x_ext
        parts = [zeros]
        for s in range(tm // SEQ):
            parts += [h[s * SEQ:(s + 1) * SEQ], zeros]
        hext = jnp.concatenate(parts, axis=0).astype(BF)
        starts = [SUBLANES + s * (SEQ + SUBLANES) for s in range(tm // SEQ)]
        span = SEQ
    mext = hext.shape[0]

    def token_rows(v):
        parts = [v[r0:r0 + span] for r0 in starts]
        return parts[0] if len(parts) == 1 else jnp.concatenate(parts, axis=0)

    def conv(u, c0):
        cols = slice(c0, c0 + FF_CHUNK)
        prev = token_rows(pltpu.roll(u, 1, 0))
        nxt = token_rows(pltpu.roll(u, mext - 1, 0))
        return (prev * cw_ref[0, 0:1, cols] + token_rows(u) * cw_ref[0, 1:2, cols]
                + nxt * cw_ref[0, 2:3, cols] + cb_ref[0, :, cols])

    @pl.when(first_step)
    def _():
        w_copies[1].wait()
        w_copies[2].wait()

    for j in range(N_CHUNK):
        ca = j * FF_CHUNK
        cg = DFF + j * FF_CHUNK
        a = conv(_dot(hext, wup_ref[:, ca:ca + FF_CHUNK]), ca)
        gt = conv(_dot(hext, wup_ref[:, cg:cg + FF_CHUNK]), cg)
        ha = 0.5 * a
        act_scr[:, ca:ca + FF_CHUNK] = ((ha + ha * jnp.tanh(ha)) * gt).astype(BF)
    g_post = gate * gpost_ref[...]
    half = tm // 2
    ys = [_dot(act_scr[r:r + half, :], wdn_ref[...]) for r in (0, half)]
    for r, yv in zip((0, half), ys):
        out_ref[r:r + half, :] = x[r:r + half] + _rms(yv, g_post)


def _mix_ffn(o, x, modl, mod_map, gmix, w_o, gpre, gpost, wup, cw, cb, wdn, l, *, tm, halo):
    n = x.shape[0]
    tps = DEC_SEQ // tm

    def prev_blk(rows):
        r = tm // rows
        return lambda t: (jnp.maximum(t * r - 1, 0), 0)

    def next_blk(rows):
        r = tm // rows
        return lambda t: (jnp.minimum((t + 1) * r, n // rows - 1), 0)

    in_specs = [pl.BlockSpec((tm, D), lambda t: (t, 0))]
    args = [o]
    if halo:
        in_specs += [pl.BlockSpec((HALO_O, D), prev_blk(HALO_O)), pl.BlockSpec((HALO_O, D), next_blk(HALO_O))]
        args += [o, o]
    in_specs.append(pl.BlockSpec((tm, D), lambda t: (t, 0)))
    args.append(x)
    if halo:
        in_specs += [pl.BlockSpec((SUBLANES, D), prev_blk(SUBLANES)),
                     pl.BlockSpec((SUBLANES, D), next_blk(SUBLANES))]
        args += [x, x]
    in_hbm = pl.BlockSpec(memory_space=pl.ANY)
    in_specs += [
        pl.BlockSpec((1, 1, 6 * D), mod_map),
        pl.BlockSpec((1, D), lambda t: (0, 0)),
        in_hbm,
        pl.BlockSpec((1, D), lambda t: (0, 0)),
        pl.BlockSpec((1, D), lambda t: (0, 0)),
        in_hbm,
        pl.BlockSpec((1, 3, 2 * DFF), lambda t: (l, 0, 0)),
        pl.BlockSpec((1, 1, 2 * DFF), lambda t: (l, 0, 0)),
        in_hbm,
    ]
    args += [modl, gmix.reshape(1, D), w_o, gpre.reshape(1, D), gpost.reshape(1, D), wup, cw, cb, wdn]
    return pl.pallas_call(
        functools.partial(_mix_ffn_kernel, halo=halo, tiles_per_seq=tps),
        grid=(n // tm,),
        in_specs=in_specs,
        out_specs=pl.BlockSpec((tm, D), lambda t: (t, 0)),
        out_shape=jax.ShapeDtypeStruct((n, D), F32),
        scratch_shapes=[pltpu.VMEM((tm, DFF), BF), pltpu.VMEM((D, D), BF), pltpu.VMEM((D, 2 * DFF), BF),
                        pltpu.VMEM((DFF, D), BF), pltpu.SemaphoreType.DMA((3,))],
        compiler_params=_params(("arbitrary",)),
        name="mix_ffn",
    )(*args)


def _rope_tables():
    f32 = np.float32
    t = np.arange(DEC_SEQ)
    rows = (t // GRID_W).astype(f32)
    cols = (t % GRID_W).astype(f32)
    quarter = HD // 4
    freqs = f32(ROPE_BASE) ** (-np.arange(quarter, dtype=f32) / f32(quarter))
    lane = np.arange(LANES)
    d = lane % HD
    e = d % (HD // 2)
    is_x2 = (e >= quarter)[None, :]
    pos = np.where((d // (HD // 2) == 0)[None, :], rows[:, None], cols[:, None])
    ang = (pos * freqs[e % quarter][None, :]).astype(f32)
    cos = np.cos(ang).astype(f32)
    sin = np.sin(ang).astype(f32)
    zero = f32(0.0)
    return (jnp.asarray(cos), jnp.asarray(np.where(is_x2, zero, -sin)),
            jnp.asarray(np.where(is_x2, sin, zero)))


def _block_diag_ones():
    r = jnp.arange(HEAD_BLOCK)
    return (r[:, None] // HD == r[None, :] // HD).astype(BF)


def _cache_t(cache):
    b, n, p, h, d = cache.shape
    return cache.transpose(0, 1, 3, 4, 2).reshape(b, n, h * d, p)


def _cache_out(kt, heads):
    b, n, _, s = kt.shape
    return kt.reshape(b, n, heads, HD, s).transpose(0, 1, 4, 2, 3)


def kernel(x_prompt, x_sample, cache_na_k, cache_na_v, cache_gqa_k, cache_gqa_v, c, c_ctx,
           ada_w, ada_b, norm_mix_pre, norm_mix_post, norm_ffn_pre, norm_ffn_post,
           na_w_qkv, na_w_o, na_rpb, gqa_w_qkv, gqa_w_o, gqa_q_norm, gqa_k_norm,
           ffn_w_up, ffn_conv_w, ffn_conv_b, ffn_w_down):
    n_a = cache_na_k.shape[1]
    n_b = cache_gqa_k.shape[1]
    cond8 = jnp.concatenate([c_ctx[None], c, jnp.zeros((SUBLANES - 1 - DEC_BATCH, D), F32)], axis=0)
    mod, w_qkv = _adaln(cond8, ada_w, ada_b, [(na_w_qkv, 0)])
    w_qkv = w_qkv[None]

    tm_p = tm_s = tm_fp = tm_fs = TOKEN_TILE
    map_p = lambda t: (0, 0, 0)
    map_s = lambda t: (1 + t // (DEC_SEQ // tm_s), 0, 0)
    map_fs = lambda t: (1 + t // (DEC_SEQ // tm_fs), 0, 0)

    cna_kt, cna_vt = _cache_t(cache_na_k), _cache_t(cache_na_v)
    cg_kt, cg_vt = _cache_t(cache_gqa_k), _cache_t(cache_gqa_v)
    rope_tabs = _rope_tables()
    bd = _block_diag_ones()
    ptabs = _na_bias_tables(na_rpb)
    kvg = NKV * HD

    cb = ffn_conv_b.reshape(DEPTH, 1, 2 * DFF)

    xp = x_prompt.reshape(NP, D)
    xs = x_sample.reshape(NS, D)
    na_kv = None
    gq_kv = None
    for l in range(DEPTH):
        i = l // 2
        modl = mod[l].reshape(SUBLANES, 1, 6 * D)
        casts = [(na_w_o if l % 2 == 0 else gqa_w_o, i), (ffn_w_down, l)]
        if l + 1 < DEPTH:
            casts.append((gqa_w_qkv if (l + 1) % 2 else na_w_qkv, (l + 1) // 2))
        if l % 2 == 0:
            qp, vp, kt, vt = _qkv_prompt(xp, modl, norm_mix_pre[l], w_qkv, i, n_a, na_kv,
                                         gqa=False, tm=tm_p)
            na_kv = (kt, vt)
            qs, kts, vs = _qkv_sample_na(xs, modl, map_s, norm_mix_pre[l], w_qkv, tm=tm_s)
            op, wup = _attn_prompt(qp, kt, vp, i, 1, [(ffn_w_up, l)])
            os_, w_o, wdn, *w_nxt = _attn_na_sample(qs, kts, vs, cna_kt, cna_vt, i, ptabs, casts)
        else:
            qg = (jnp.tile(gqa_q_norm[i], NH) * Q_SCALE).reshape(1, D)
            kg = jnp.tile(gqa_k_norm[i], NKV)
            qkg = jnp.concatenate([qg[0], kg]).reshape(1, D + kvg)
            kg2 = jnp.broadcast_to(kg[:, None], (kvg, SEQ))
            qp, vp, kt, vt = _qkv_prompt(xp, modl, norm_mix_pre[l], w_qkv, i, n_b, gq_kv,
                                         gqa=True, tm=tm_p, extra=(bd, qg, kg2))
            gq_kv = (kt, vt)
            qs, kts, vs = _qkv_sample_gqa(xs, modl, map_s, norm_mix_pre[l], w_qkv, bd, qkg,
                                          rope_tabs, tm=tm_s)
            op, wup = _attn_prompt(qp, kt, vp, i, NH // NKV, [(ffn_w_up, l)])
            os_, w_o, wdn, *w_nxt = _attn_gqa_sample(qs, kts, vs, cg_kt, cg_vt, i, casts)
        if w_nxt:
            w_qkv = w_nxt[0][None]
        xp = _mix_ffn(op, xp, modl, map_p, norm_mix_post[l], w_o, norm_ffn_pre[l], norm_ffn_post[l],
                      wup, ffn_conv_w, cb, wdn, l, tm=tm_fp, halo=False)
        xs = _mix_ffn(os_, xs, modl, map_fs, norm_mix_post[l], w_o, norm_ffn_pre[l], norm_ffn_post[l],
                      wup, ffn_conv_w, cb, wdn, l, tm=tm_fs, halo=True)

    return (xp.reshape(BATCH, SEQ, D), xs.reshape(DEC_BATCH, DEC_SEQ, D),
            _cache_out(na_kv[0], NH), _cache_out(na_kv[1], NH),
            _cache_out(gq_kv[0], NKV), _cache_out(gq_kv[1], NKV))
```
